```python
import numpy as np
import jax
import jax.numpy as jnp
from jax import lax

D_MODEL = 1024
BATCH = 4
SEQ = 4096
DEPTH = 2
DEC_BATCH = 32
DEC_SEQ = 1
PAST_LEN = 8192
PAGE_SIZE = 128

HEAD_DIM = 64
BRANCH_W = D_MODEL // 2
N_HEADS = BRANCH_W // HEAD_DIM
N_KV = N_HEADS // 4
GROUP = N_HEADS // N_KV
ROT_DIM = HEAD_DIM // 4
ROPE_THETA = 500000.0
CMP_LEN = 32
CMP_STRIDE = 16
CMP_RATIO = CMP_LEN // CMP_STRIDE
CMP_HIDDEN = 2 * HEAD_DIM
SLC_BLOCK = 64
N_SELECT = 16
WINDOW = 512
Q_BLOCK = 128
CONV_CH = BRANCH_W
CONV_W = 3
MEM_LEN = 256
MEM_HEADS = 4
MEM_HEAD_DIM = BRANCH_W // MEM_HEADS
N_BRANCH = 3
D_FF = ((8 * D_MODEL + 767) // 768) * 256
N_IN = N_HEADS * HEAD_DIM + 6 * N_KV * HEAD_DIM + 3 * N_HEADS + 3 * CONV_CH + MEM_HEADS * MEM_HEAD_DIM + N_BRANCH * D_MODEL
EPS = 1e-6

kernel_name = 'nsa_shortconv_memxattn_hybrid_step'


def rmsnorm(x, g):
    xf = x.astype(jnp.float32)
    y = xf * lax.rsqrt(jnp.mean(xf * xf, axis=-1, keepdims=True) + EPS)
    return (y * g.astype(jnp.float32)).astype(x.dtype)


def rope(x, pos):
    half = ROT_DIM // 2
    inv = ROPE_THETA ** (-jnp.arange(half, dtype=jnp.float32) / half)
    ang = pos.astype(jnp.float32)[:, None] * inv[None, :]
    cos = jnp.cos(ang)[:, None, :]
    sin = jnp.sin(ang)[:, None, :]
    xr = x[..., :ROT_DIM].astype(jnp.float32)
    x1, x2 = xr[..., :half], xr[..., half:]
    rot = jnp.concatenate([x1 * cos - x2 * sin, x2 * cos + x1 * sin], axis=-1)
    return jnp.concatenate([rot.astype(x.dtype), x[..., ROT_DIM:]], axis=-1)


def masked_softmax(s, mask):
    s = jnp.where(mask, s.astype(jnp.float32), -1e30)
    return jnp.where(mask, jax.nn.softmax(s, axis=-1), 0.0)


def split_in(h):
    sizes = [N_HEADS * HEAD_DIM, 6 * N_KV * HEAD_DIM, 3 * N_HEADS, CONV_CH, CONV_CH, CONV_CH,
             MEM_HEADS * MEM_HEAD_DIM]
    return jnp.split(h, np.cumsum(sizes).tolist(), axis=-1)


def prep(x, pos, p):
    B, T, _ = x.shape
    xn = rmsnorm(x, p['norm_mix'])
    h = jnp.einsum('btd,de->bte', xn, p['w_in'])
    q, kv, ng, cx, cb, cc, mq, mg = split_in(h)
    q = rmsnorm(q.reshape(B, T, N_HEADS, HEAD_DIM), p['q_norm'])
    kv = kv.reshape(B, T, 3, 2, N_KV, HEAD_DIM)
    k_slc = rope(rmsnorm(kv[:, :, 1, 0], p['k_norm'][1]), pos)
    k_win = rope(rmsnorm(kv[:, :, 2, 0], p['k_norm'][2]), pos)
    return dict(
        q=q, q_rot=rope(q, pos),
        rows=jnp.stack([kv[:, :, 0, 0], kv[:, :, 0, 1], k_slc, kv[:, :, 1, 1]], axis=2),
        win=jnp.stack([k_win, kv[:, :, 2, 1]], axis=2),
        ng=jax.nn.sigmoid(ng).reshape(B, T, 3, N_HEADS),
        u=cc * cx, cb=cb,
        mq=rmsnorm(mq.reshape(B, T, MEM_HEADS, MEM_HEAD_DIM), p['mem_q_norm']),
        mg=jax.nn.sigmoid(mg).reshape(B, T, N_BRANCH, D_MODEL))


def compress(rows, pe, w1, w2):
    B, T = rows.shape[:2]
    n_sub = T // CMP_STRIDE
    nc = n_sub - CMP_RATIO + 1
    sub = rows[:, :n_sub * CMP_STRIDE].reshape(B, n_sub, CMP_STRIDE, N_KV, HEAD_DIM)
    w1r = w1.reshape(CMP_RATIO, CMP_STRIDE, HEAD_DIM, CMP_HIDDEN)
    per = pe.reshape(CMP_RATIO, CMP_STRIDE, 1, HEAD_DIM)
    h = jnp.einsum('bnsgd,sdh->bngh', sub[:, :nc] + per[0], w1r[0])
    for r in range(1, CMP_RATIO):
        h = h + jnp.einsum('bnsgd,sdh->bngh', sub[:, r:r + nc] + per[r], w1r[r])
    return jnp.einsum('bngh,he->bnge', jax.nn.silu(h), w2)


def compress_kv(k_raw, v_raw, p):
    kc = rmsnorm(compress(k_raw, p['cmp_pe'][0], p['cmp_w1'][0], p['cmp_w2'][0]), p['k_norm'][0])
    vc = compress(v_raw, p['cmp_pe'][1], p['cmp_w1'][1], p['cmp_w2'][1])
    return kc, vc


def cmp_attend(q, q_pos, kc, vc):
    B, Tq = q.shape[:2]
    nc = kc.shape[1]
    ends = jnp.arange(nc) * CMP_STRIDE + CMP_LEN - 1
    mask = ends[None, :] <= q_pos[:, None]
    qg = q.reshape(B, Tq, N_KV, GROUP, HEAD_DIM)
    s = jnp.einsum('bqghd,bcgd->bqghc', qg, kc) * HEAD_DIM ** -0.5
    pr = masked_softmax(s, mask[None, :, None, None, :])
    o = jnp.einsum('bqghc,bcgd->bqghd', pr.astype(vc.dtype), vc)
    return o.reshape(B, Tq, N_HEADS, HEAD_DIM), jnp.sum(pr, axis=3)


def select_blocks(imp, q_pos, n_keys):
    nc = imp.shape[-1]
    ns = -(-n_keys // SLC_BLOCK)
    i = jnp.arange(nc)[:, None]
    j = jnp.arange(ns)[None, :]
    overlap = ((i * CMP_STRIDE < (j + 1) * SLC_BLOCK) & (i * CMP_STRIDE + CMP_LEN > j * SLC_BLOCK)).astype(jnp.float32)
    score = jnp.einsum('bqgc,cs->bqgs', imp.astype(jnp.float32), overlap)
    qb = (q_pos // SLC_BLOCK)[:, None]
    forced = (j == 0) | (j == qb) | (j == qb - 1)
    eligible = j * SLC_BLOCK <= q_pos[:, None]
    score = jnp.where(forced[None, :, None, :], jnp.inf,
                      jnp.where(eligible[None, :, None, :], score, -jnp.inf))
    _, idx = lax.top_k(score, min(N_SELECT, ns))
    return idx


def attend_gathered(q, q_pos, k, v, k_pos):
    B, Tq = q.shape[:2]
    qg = q.reshape(B, Tq, N_KV, GROUP, HEAD_DIM)
    s = jnp.einsum('bqghd,bqgkd->bqghk', qg, k) * HEAD_DIM ** -0.5
    mask = (k_pos <= q_pos[None, :, None, None])[:, :, :, None, :]
    pr = masked_softmax(s, mask)
    o = jnp.einsum('bqghk,bqgkd->bqghd', pr.astype(v.dtype), v)
    return o.reshape(B, Tq, N_HEADS, HEAD_DIM)


def window_attend(q, q_pos, k, v, k_pos):
    B, Tq = q.shape[:2]
    qg = q.reshape(B, Tq, N_KV, GROUP, HEAD_DIM)
    s = jnp.einsum('bqghd,bkgd->bqghk', qg, k) * HEAD_DIM ** -0.5
    d = q_pos[:, None] - k_pos[None, :]
    mask = (d >= 0) & (d < WINDOW) & (k_pos[None, :] >= 0)
    pr = masked_softmax(s, mask[None, :, None, None, :])
    o = jnp.einsum('bqghk,bkgd->bqghd', pr.astype(v.dtype), v)
    return o.reshape(B, Tq, N_HEADS, HEAD_DIM)


def merge_blocks(o):
    nb, B, qb = o.shape[:3]
    return jnp.moveaxis(o, 0, 1).reshape(B, nb * qb, N_HEADS, HEAD_DIM)


def combine_nsa(ng, o_c, o_s, o_w):
    o = ng[:, :, 0, :, None] * o_c + ng[:, :, 1, :, None] * o_s + ng[:, :, 2, :, None] * o_w
    return o.reshape(o.shape[0], o.shape[1], N_HEADS * HEAD_DIM)


def nsa_prompt(t, p):
    q, q_rot, rows = t['q'], t['q_rot'], t['rows']
    B, T = q.shape[:2]
    pos = jnp.arange(T)
    kc, vc = compress_kv(rows[:, :, 0], rows[:, :, 1], p)
    o_c, imp = cmp_attend(q, pos, kc, vc)
    idx = select_blocks(imp, pos, T)
    k_s, v_s = rows[:, :, 2], rows[:, :, 3]
    bi = jnp.arange(B)[:, None, None, None]
    gi = jnp.arange(N_KV)[None, None, :, None]
    offs = jnp.arange(SLC_BLOCK)
    starts = jnp.arange(T // Q_BLOCK) * Q_BLOCK

    def slc_block(start):
        qb = lax.dynamic_slice_in_dim(q_rot, start, Q_BLOCK, 1)
        ib = lax.dynamic_slice_in_dim(idx, start, Q_BLOCK, 1)
        kpos = (ib[..., None] * SLC_BLOCK + offs).reshape(B, Q_BLOCK, N_KV, -1)
        kposc = jnp.minimum(kpos, T - 1)
        return attend_gathered(qb, start + jnp.arange(Q_BLOCK), k_s[bi, kposc, gi], v_s[bi, kposc, gi], kpos)

    o_s = merge_blocks(lax.map(slc_block, starts))
    wp = jnp.pad(t['win'], ((0, 0), (WINDOW, 0), (0, 0), (0, 0), (0, 0)))

    def win_block(start):
        qb = lax.dynamic_slice_in_dim(q_rot, start, Q_BLOCK, 1)
        kvb = lax.dynamic_slice_in_dim(wp, start, WINDOW + Q_BLOCK, 1)
        kpos = start - WINDOW + jnp.arange(WINDOW + Q_BLOCK)
        return window_attend(qb, start + jnp.arange(Q_BLOCK), kvb[:, :, 0], kvb[:, :, 1], kpos)

    o_w = merge_blocks(lax.map(win_block, starts))
    return combine_nsa(t['ng'], o_c, o_s, o_w)


def nsa_sample(t, pool, layer, win_buf, page_table, p):
    q, q_rot, rows = t['q'], t['q_rot'], t['rows']
    B, Tn = q.shape[:2]
    n_pages = page_table.shape[1]
    pos = PAST_LEN + jnp.arange(Tn)
    past_c = pool[layer, page_table, :, :2].reshape(B, PAST_LEN, 2, N_KV, HEAD_DIM)
    full_c = jnp.concatenate([past_c, rows[:, :, :2]], axis=1)
    kc, vc = compress_kv(full_c[:, :, 0], full_c[:, :, 1], p)
    o_c, imp = cmp_attend(q, pos, kc, vc)
    idx = select_blocks(imp, pos, PAST_LEN + Tn)
    bi = jnp.arange(B)[:, None, None, None]
    gi = jnp.arange(N_KV)[None, None, :, None]
    kpos = (idx[..., None] * SLC_BLOCK + jnp.arange(SLC_BLOCK)).reshape(B, Tn, N_KV, -1)
    phys = page_table[bi, jnp.minimum(kpos // PAGE_SIZE, n_pages - 1)]
    off = kpos % PAGE_SIZE
    rel = jnp.clip(kpos - PAST_LEN, 0, Tn - 1)
    is_past = (kpos < PAST_LEN)[..., None]
    k_s = jnp.where(is_past, pool[layer, phys, off, 2, gi], rows[bi, rel, 2, gi])
    v_s = jnp.where(is_past, pool[layer, phys, off, 3, gi], rows[bi, rel, 3, gi])
    o_s = attend_gathered(q_rot, pos, k_s, v_s, kpos)
    w_buf = win_buf.shape[1]
    wfull = jnp.concatenate([win_buf, t['win']], axis=1)
    kpos_w = PAST_LEN - w_buf + jnp.arange(w_buf + Tn)
    o_w = window_attend(q_rot, pos, wfull[:, :, 0], wfull[:, :, 1], kpos_w)
    return combine_nsa(t['ng'], o_c, o_s, o_w), wfull[:, -w_buf:]


def dwconv(u_ext, w):
    T = u_ext.shape[1] - (CONV_W - 1)
    y = u_ext[:, :T] * w[0]
    for k in range(1, CONV_W):
        y = y + u_ext[:, k:k + T] * w[k]
    return y


def mem_kv_from(mem, p):
    B, M, _ = mem.shape
    kv = jnp.einsum('bmd,de->bme', rmsnorm(mem, p['norm_mem']), p['w_mem_kv'])
    kv = kv.reshape(B, M, 2, MEM_HEADS, MEM_HEAD_DIM)
    return jnp.stack([rmsnorm(kv[:, :, 0], p['mem_k_norm']), kv[:, :, 1]], axis=2)


def mem_attend(mq, mem_kv):
    B, T = mq.shape[:2]
    s = jnp.einsum('bthd,bmhd->bthm', mq, mem_kv[:, :, 0]) * MEM_HEAD_DIM ** -0.5
    pr = jax.nn.softmax(s.astype(jnp.float32), axis=-1)
    o = jnp.einsum('bthm,bmhd->bthd', pr.astype(mem_kv.dtype), mem_kv[:, :, 1])
    return o.reshape(B, T, MEM_HEADS * MEM_HEAD_DIM)


def finish(x, o_nsa, z, o_mem, mg, p):
    br = jnp.stack([o_nsa, z, o_mem], axis=2)
    proj = jnp.einsum('btnc,ncd->btnd', br, p['w_branch'])
    h = x + jnp.einsum('btd,de->bte', jnp.sum(mg * proj, axis=2), p['w_out'])
    gu = jnp.einsum('btd,df->btf', rmsnorm(h, p['norm_ffn']), p['w_gate_up'])
    g, u = jnp.split(gu, 2, axis=-1)
    return h + jnp.einsum('btf,fd->btd', jax.nn.silu(g) * u, p['w_down'])


def setup_inputs(seed: int = 0) -> dict:
    key = jax.random.key(seed)
    ks = jax.random.split(key, 32)
    f32 = jnp.float32
    n_pages = PAST_LEN // PAGE_SIZE
    n_used = DEC_BATCH * n_pages
    n_phys = n_used + (n_used + 3) // 4
    w_buf = min(WINDOW, PAST_LEN)

    def nrm(k, shape, scale):
        return (scale * jax.random.normal(k, shape)).astype(f32)

    def gain(k, shape):
        return (1.0 + 0.02 * jax.random.normal(k, shape)).astype(f32)

    page_table = jax.random.permutation(ks[0], n_phys)[:n_used].reshape(DEC_BATCH, n_pages).astype(jnp.int32)
    return {
        'x_prompt': nrm(ks[1], (BATCH, SEQ, D_MODEL), 1.0),
        'x_sample': nrm(ks[2], (DEC_BATCH, DEC_SEQ, D_MODEL), 1.0),
        'cache_nsa_kv': nrm(ks[3], (DEPTH, n_phys, PAGE_SIZE, 4, N_KV, HEAD_DIM), 1.0),
        'cache_win_kv': nrm(ks[4], (DEPTH, DEC_BATCH, w_buf, 2, N_KV, HEAD_DIM), 1.0),
        'state_conv': nrm(ks[5], (DEPTH, DEC_BATCH, CONV_W - 1, CONV_CH), 1.0),
        'cache_mem_kv': nrm(ks[6], (DEPTH, DEC_BATCH, MEM_LEN, 2, MEM_HEADS, MEM_HEAD_DIM), 1.0),
        'page_table': page_table,
        'mem_prompt': nrm(ks[7], (BATCH, MEM_LEN, D_MODEL), 1.0),
        'norm_mix': gain(ks[8], (DEPTH, D_MODEL)),
        'w_in': nrm(ks[9], (DEPTH, D_MODEL, N_IN), D_MODEL ** -0.5),
        'q_norm': gain(ks[10], (DEPTH, HEAD_DIM)),
        'k_norm': gain(ks[11], (DEPTH, 3, HEAD_DIM)),
        'cmp_pe': nrm(ks[12], (DEPTH, 2, CMP_LEN, HEAD_DIM), 0.1),
        'cmp_w1': nrm(ks[13], (DEPTH, 2, CMP_LEN * HEAD_DIM, CMP_HIDDEN), (CMP_LEN * HEAD_DIM) ** -0.5),
        'cmp_w2': nrm(ks[14], (DEPTH, 2, CMP_HIDDEN, HEAD_DIM), CMP_HIDDEN ** -0.5),
        'conv_w': nrm(ks[15], (DEPTH, CONV_W, CONV_CH), CONV_W ** -0.5),
        'norm_mem': gain(ks[16], (DEPTH, D_MODEL)),
        'w_mem_kv': nrm(ks[17], (DEPTH, D_MODEL, 2 * MEM_HEADS * MEM_HEAD_DIM), D_MODEL ** -0.5),
        'mem_q_norm': gain(ks[18], (DEPTH, MEM_HEAD_DIM)),
        'mem_k_norm': gain(ks[19], (DEPTH, MEM_HEAD_DIM)),
        'w_branch': nrm(ks[20], (DEPTH, N_BRANCH, BRANCH_W, D_MODEL), BRANCH_W ** -0.5),
        'w_out': nrm(ks[21], (DEPTH, D_MODEL, D_MODEL), D_MODEL ** -0.5),
        'norm_ffn': gain(ks[22], (DEPTH, D_MODEL)),
        'w_gate_up': nrm(ks[23], (DEPTH, D_MODEL, 2 * D_FF), D_MODEL ** -0.5),
        'w_down': nrm(ks[24], (DEPTH, D_FF, D_MODEL), D_FF ** -0.5),
    }


def reference(x_prompt, x_sample, cache_nsa_kv, cache_win_kv, state_conv, cache_mem_kv, page_table, mem_prompt,
              norm_mix, w_in, q_norm, k_norm, cmp_pe, cmp_w1, cmp_w2, conv_w, norm_mem, w_mem_kv,
              mem_q_norm, mem_k_norm, w_branch, w_out, norm_ffn, w_gate_up, w_down):
    xp, xs = x_prompt, x_sample
    pos_p = jnp.arange(xp.shape[1])
    pos_s = PAST_LEN + jnp.arange(xs.shape[1])
    w_prompt = min(WINDOW, xp.shape[1])
    rows_p, rows_s, win_p, win_s, conv_p, conv_s, mem_p = [], [], [], [], [], [], []
    for l in range(DEPTH):
        p = dict(norm_mix=norm_mix[l], w_in=w_in[l], q_norm=q_norm[l], k_norm=k_norm[l], cmp_pe=cmp_pe[l],
                 cmp_w1=cmp_w1[l], cmp_w2=cmp_w2[l], norm_mem=norm_mem[l], w_mem_kv=w_mem_kv[l],
                 mem_q_norm=mem_q_norm[l], mem_k_norm=mem_k_norm[l], w_branch=w_branch[l], w_out=w_out[l],
                 norm_ffn=norm_ffn[l], w_gate_up=w_gate_up[l], w_down=w_down[l])
        t = prep(xp, pos_p, p)
        o_nsa = nsa_prompt(t, p)
        u_ext = jnp.pad(t['u'], ((0, 0), (CONV_W - 1, 0), (0, 0)))
        z = t['cb'] * dwconv(u_ext, conv_w[l])
        mkv = mem_kv_from(mem_prompt, p)
        o_mem = mem_attend(t['mq'], mkv)
        xp = finish(xp, o_nsa, z, o_mem, t['mg'], p)
        rows_p.append(t['rows'])
        win_p.append(t['win'][:, -w_prompt:])
        conv_p.append(u_ext[:, -(CONV_W - 1):])
        mem_p.append(mkv)
        t = prep(xs, pos_s, p)
        o_nsa, new_win = nsa_sample(t, cache_nsa_kv, l, cache_win_kv[l], page_table, p)
        u_ext = jnp.concatenate([state_conv[l], t['u']], axis=1)
        z = t['cb'] * dwconv(u_ext, conv_w[l])
        o_mem = mem_attend(t['mq'], cache_mem_kv[l])
        xs = finish(xs, o_nsa, z, o_mem, t['mg'], p)
        rows_s.append(t['rows'])
        win_s.append(new_win)
        conv_s.append(u_ext[:, -(CONV_W - 1):])
    return (xp, xs, jnp.stack(rows_p), jnp.stack(rows_s), jnp.stack(win_p), jnp.stack(win_s),
            jnp.stack(conv_p), jnp.stack(conv_s), jnp.stack(mem_p))
```

```python
import functools

import numpy as np
import jax
import jax.numpy as jnp
from jax import lax
from jax.experimental import pallas as pl
from jax.experimental.pallas import tpu as pltpu

F32 = jnp.float32
BF16 = jnp.bfloat16

D_MODEL = 1024
DEPTH = 2
PAST_LEN = 8192
PAGE_SIZE = 128
HEAD_DIM = 64
BRANCH_W = D_MODEL // 2
N_HEADS = BRANCH_W // HEAD_DIM
N_KV = N_HEADS // 4
GROUP = N_HEADS // N_KV
ROT_DIM = HEAD_DIM // 4
ROPE_THETA = 500000.0
CMP_LEN = 32
CMP_STRIDE = 16
CMP_RATIO = CMP_LEN // CMP_STRIDE
CMP_HIDDEN = 2 * HEAD_DIM
SLC_BLOCK = 64
N_SELECT = 16
WINDOW = 512
CONV_W = 3
MEM_LEN = 256
MEM_HEADS = 4
MEM_HEAD_DIM = BRANCH_W // MEM_HEADS
N_BRANCH = 3
D_FF = ((8 * D_MODEL + 767) // 768) * 256
EPS = 1e-6

LANES = 128
VMEM_LIMIT = 56 * 1024 * 1024

TM = 512
TQ = 256
FF_CHUNK = 256
NEG = -1e30
M_INIT = -3e38

_C_Q = 0
_C_KV = 512
_C_CX = 1280
_C_CB = 1792
_C_CC = 2304
_C_MQ = 2816
_C_NG = 3328
_N_A = 3456


def _const_spec(shape):
    nd = len(shape)
    return pl.BlockSpec(shape, lambda *_: (0,) * nd, pipeline_mode=pl.Buffered(1))


def _dot(a, b):
    return jnp.dot(a, b, preferred_element_type=F32)


def _dot_nt(a, b):
    return lax.dot_general(a, b, (((1,), (1,)), ((), ())), preferred_element_type=F32)


def _split_bf16(v):
    hi = v.astype(BF16)
    lo = (v - hi.astype(F32)).astype(BF16)
    return hi, lo


def _dot_exact01(v, m01):
    hi, lo = _split_bf16(v)
    return _dot(hi, m01) + _dot(lo, m01)


def _group_rms(v, e01, group, gain):
    ms = _dot_exact01(v * v, e01) * (1.0 / group)
    return v * lax.rsqrt(ms + EPS) * gain


def _rope_cols(v, c, s1, s2):
    outs = []
    for j in range(v.shape[1] // LANES):
        col = v[:, j * LANES:(j + 1) * LANES]
        outs.append(col * c + pltpu.roll(col, ROT_DIM // 2, 1) * s1
                    + pltpu.roll(col, LANES - ROT_DIM // 2, 1) * s2)
    return outs[0] if len(outs) == 1 else jnp.concatenate(outs, axis=1)


def _prep_kernel(sample, tiles_per_seq, *refs):
    if sample:
        (x_ref, nmix_ref, w_ref, qg_ref, kg_ref, mqg_ref, rope_ref, e64_ref, e128_ref, cw_ref,
         s0_ref, s1_ref,
         q_out, qr_out, rows_out, win_out, kvb_out, z_out, u_out, mq_out, ng_out) = refs
    else:
        (x_ref, nmix_ref, w_ref, qg_ref, kg_ref, mqg_ref, rope_ref, e64_ref, e128_ref, cw_ref,
         q_out, qr_out, rows_out, win_out, kvb_out, z_out, u_out, mq_out, ng_out,
         carry_ref) = refs
    tm = x_ref.shape[0]
    x = x_ref[...]
    ms = jnp.sum(x * x, axis=-1, keepdims=True) * (1.0 / D_MODEL)
    xn = (x * lax.rsqrt(ms + EPS) * nmix_ref[...]).astype(BF16)

    rc, rs1, rs2 = rope_ref[0], rope_ref[1], rope_ref[2]
    e64 = e64_ref[...]

    q = _dot(xn, w_ref[:, _C_Q:_C_Q + 512])
    qn = _group_rms(q, e64, HEAD_DIM, qg_ref[...])
    q_out[...] = (qn * HEAD_DIM ** -0.5).astype(BF16)
    qr_out[...] = (_rope_cols(qn, rc, rs1, rs2) * HEAD_DIM ** -0.5).astype(BF16)

    kv = _dot(xn, w_ref[:, _C_KV:_C_KV + 768])
    e64k = e64_ref[0:LANES, 0:LANES]
    k_slc = _rope_cols(_group_rms(kv[:, 256:384], e64k, HEAD_DIM, kg_ref[0:1, :]), rc, rs1, rs2)
    k_win = _rope_cols(_group_rms(kv[:, 512:640], e64k, HEAD_DIM, kg_ref[1:2, :]), rc, rs1, rs2)
    v_slc = kv[:, 384:512]
    v_win = kv[:, 640:768]
    rows_out[:, 0:256] = kv[:, 0:256]
    rows_out[:, 256:384] = k_slc
    rows_out[:, 384:512] = v_slc
    win_out[:, 0:128] = k_win
    win_out[:, 128:256] = v_win
    kvb_out[:, 0:128] = k_slc.astype(BF16)
    kvb_out[:, 128:256] = v_slc.astype(BF16)
    kvb_out[:, 256:384] = k_win.astype(BF16)
    kvb_out[:, 384:512] = v_win.astype(BF16)

    cx = _dot(xn, w_ref[:, _C_CX:_C_CX + 512])
    cc = _dot(xn, w_ref[:, _C_CC:_C_CC + 512])
    cb = _dot(xn, w_ref[:, _C_CB:_C_CB + 512])
    u = cc * cx
    w0, w1, w2 = cw_ref[0:1, :], cw_ref[1:2, :], cw_ref[2:3, :]
    if sample:
        z = cb * (s0_ref[...] * w0 + s1_ref[...] * w1 + u * w2)
        u_out[...] = u
    else:
        @pl.when(pl.program_id(0) % tiles_per_seq == 0)
        def _():
            carry_ref[...] = jnp.zeros_like(carry_ref)
        prev = carry_ref[...]
        row = lax.broadcasted_iota(jnp.int32, u.shape, 0)
        u1 = jnp.where(row == 0, prev[7:8, :], pltpu.roll(u, 1, 0))
        u2 = jnp.where(row == 0, prev[6:7, :],
                       jnp.where(row == 1, prev[7:8, :], pltpu.roll(u, 2, 0)))
        z = cb * (u2 * w0 + u1 * w1 + u * w2)
        tail = u[tm - 8:tm, :]
        carry_ref[...] = tail
        u_out[...] = tail
    z_out[...] = z.astype(BF16)

    mq = _dot(xn, w_ref[:, _C_MQ:_C_MQ + 512])
    mq_out[...] = _group_rms(mq, e128_ref[...], MEM_HEAD_DIM, mqg_ref[...]).astype(BF16)

    ng_out[...] = jax.nn.sigmoid(_dot(xn, w_ref[:, _C_NG:_C_NG + 128]))


def _prep_call(x, pw, rope_tab, sample, seq_len, state=None):
    n = x.shape[0]
    tm = n if sample else TM
    tps = 1 if sample else seq_len // tm
    grid = (n // tm,)
    tok = lambda w: pl.BlockSpec((tm, w), lambda i: (i, 0))
    in_specs = [tok(D_MODEL), _const_spec((1, D_MODEL)), _const_spec((D_MODEL, _N_A)),
                _const_spec((1, 512)), _const_spec((2, LANES)), _const_spec((1, 512)),
                pl.BlockSpec((3, tm, LANES), lambda i: (0, i % tps, 0)),
                _const_spec((512, 512)), _const_spec((512, 512)), _const_spec((CONV_W, 512))]
    args = [x, pw['nmix'], pw['w_a'], pw['q_gain'], pw['k_gain'], pw['mq_gain'], rope_tab,
            pw['e64'], pw['e128'], pw['conv_w']]
    scratch = []
    if sample:
        in_specs += [tok(512), tok(512)]
        args += [state[:, 0, :], state[:, 1, :]]
        u_shape, u_spec = (n, 512), tok(512)
    else:
        scratch = [pltpu.VMEM((8, 512), F32)]
        u_shape, u_spec = (n // tm * 8, 512), pl.BlockSpec((8, 512), lambda i: (i, 0))
    out_shape = [jax.ShapeDtypeStruct((n, 512), BF16), jax.ShapeDtypeStruct((n, 512), BF16),
                 jax.ShapeDtypeStruct((n, 512), F32), jax.ShapeDtypeStruct((n, 256), F32),
                 jax.ShapeDtypeStruct((n, 512), BF16), jax.ShapeDtypeStruct((n, 512), BF16),
                 jax.ShapeDtypeStruct(u_shape, F32), jax.ShapeDtypeStruct((n, 512), BF16),
                 jax.ShapeDtypeStruct((n, LANES), F32)]
    out_specs = [tok(512), tok(512), tok(512), tok(256), tok(512), tok(512), u_spec, tok(512),
                 tok(LANES)]
    return pl.pallas_call(
        functools.partial(_prep_kernel, sample, tps),
        grid=grid, in_specs=in_specs, out_specs=out_specs, out_shape=out_shape,
        scratch_shapes=scratch,
        compiler_params=pltpu.CompilerParams(dimension_semantics=("arbitrary",),
                                             vmem_limit_bytes=VMEM_LIMIT),
        name="prep_sample" if sample else "prep_prompt",
    )(*args)


def _compress(xk_ref, xv_ref, n_sub, w1_ref, pe_ref, w2_ref, kg_ref, e64k):
    outs = []
    for kv, x_ref in enumerate((xk_ref, xv_ref)):
        acc = [None, None]
        for s in range(CMP_STRIDE):
            xs = x_ref[pl.ds(s, n_sub, stride=CMP_STRIDE), :]
            for r in range(CMP_RATIO):
                idx = (kv * CMP_RATIO + r) * CMP_STRIDE + s
                lhs = (xs + pe_ref[idx:idx + 1, :]).astype(BF16)
                d = _dot(lhs, w1_ref[idx])
                acc[r] = d if acc[r] is None else acc[r] + d
        h = acc[0] + pltpu.roll(acc[1], n_sub - 1, 0)
        o = _dot(jax.nn.silu(h).astype(BF16), w2_ref[kv])
        outs.append(o)
    kc = _group_rms(outs[0], e64k, HEAD_DIM, kg_ref[...])
    return kc, outs[1]


def _compress_prompt_kernel(xk_ref, xv_ref, w1_ref, pe_ref, w2_ref, kg_ref, e64_ref,
                            kc_out, vc_out):
    n_sub = kc_out.shape[1]
    kc, vc = _compress(xk_ref.at[0], xv_ref.at[0], n_sub, w1_ref, pe_ref, w2_ref, kg_ref,
                       e64_ref[...])
    kc_out[0] = kc.astype(BF16)
    vc_out[0] = vc.astype(BF16)


def _compress_prompt_call(rows3, pw):
    b, t, _ = rows3.shape
    n_sub = t // CMP_STRIDE
    out = jax.ShapeDtypeStruct((b, n_sub, LANES), BF16)
    return pl.pallas_call(
        _compress_prompt_kernel,
        grid=(b,),
        in_specs=[pl.BlockSpec((1, t, LANES), lambda i: (i, 0, 0)),
                  pl.BlockSpec((1, t, LANES), lambda i: (i, 0, 1)),
                  _const_spec((2 * CMP_RATIO * CMP_STRIDE, LANES, 2 * CMP_HIDDEN)),
                  _const_spec((2 * CMP_RATIO * CMP_STRIDE, LANES)),
                  _const_spec((2, 2 * CMP_HIDDEN, LANES)),
                  _const_spec((1, LANES)), _const_spec((LANES, LANES))],
        out_specs=[pl.BlockSpec((1, n_sub, LANES), lambda i: (i, 0, 0))] * 2,
        out_shape=[out, out],
        compiler_params=pltpu.CompilerParams(dimension_semantics=("arbitrary",),
                                             vmem_limit_bytes=VMEM_LIMIT),
        name="compress_prompt",
    )(rows3, rows3, pw['cmp_w1'], pw['cmp_pe'], pw['cmp_w2'], pw['kc_gain'], pw['e64k'])


def _flash_init(m_ref, l_ref, acc_ref):
    m_ref[...] = jnp.full(m_ref.shape, M_INIT, F32)
    l_ref[...] = jnp.zeros(l_ref.shape, F32)
    acc_ref[...] = jnp.zeros(acc_ref.shape, F32)


def _flash_step(s, v, m_ref, l_ref, acc_ref):
    m_prev = m_ref[...]
    m_next = jnp.maximum(m_prev, jnp.max(s, axis=1, keepdims=True))
    p = jnp.exp(s - jnp.concatenate([m_next] * (s.shape[1] // LANES), axis=1))
    alpha = jnp.exp(m_prev - m_next)
    l_ref[...] = alpha * l_ref[...] + jnp.sum(p, axis=1, keepdims=True)
    acc_ref[...] = alpha * acc_ref[...] + _dot(p.astype(BF16), v)
    m_ref[...] = m_next


def _rank_select(score_t, n_blocks, n_sel):
    jidx = lax.broadcasted_iota(jnp.int32, score_t.shape, 0)
    cnt = jnp.zeros(score_t.shape, F32)
    for i in range(n_blocks):
        bi = score_t[i:i + 1, :]
        tie = jnp.where(jidx > i, 1.0, 0.0)
        cnt = cnt + jnp.where(bi > score_t, 1.0, jnp.where(bi == score_t, tie, 0.0))
    return jnp.where(cnt < n_sel, 1.0, 0.0)


def _nsa_prompt_kernel(q_ref, qr_ref, kvb_ref, kc_ref, vc_ref, ng_ref, ov_ref, cm_ref, gx_ref,
                       o_ref, m_ref, l_ref, acc_ref, ob_ref):
    tq = q_ref.shape[1]
    mrows = GROUP * tq
    n_cmp = kc_ref.shape[1]
    i = pl.program_id(1)
    q0 = i * tq
    lane = lax.broadcasted_iota(jnp.int32, (tq, LANES), 1)
    low = lane < HEAD_DIM

    def stack(qb, g):
        keep = low if g == 0 else jnp.logical_not(low)
        parts = [jnp.where(keep, qb[:, LANES * j:LANES * (j + 1)], jnp.zeros((tq, LANES), BF16))
                 for j in range(GROUP)]
        return jnp.concatenate(parts, axis=0)

    def rel_pos(k0, width):
        row = lax.broadcasted_iota(jnp.int32, (mrows, width), 0)
        col = lax.broadcasted_iota(jnp.int32, (mrows, width), 1)
        return (q0 - k0) + (row & (tq - 1)) - col

    q = q_ref[0]
    qr = qr_ref[0]
    kc = kc_ref[0]
    vc = vc_ref[0]

    for g in range(N_KV):
        s = _dot_nt(stack(q, g), kc)
        row = lax.broadcasted_iota(jnp.int32, (mrows, n_cmp), 0)
        col = lax.broadcasted_iota(jnp.int32, (mrows, n_cmp), 1)
        vis = col * CMP_STRIDE + (CMP_LEN - 1) <= q0 + (row & (tq - 1))
        sm = jnp.where(vis, s, NEG)
        e = jnp.where(vis, jnp.exp(sm - jnp.max(sm, axis=1, keepdims=True)), 0.0)
        lsum = jnp.sum(e, axis=1, keepdims=True)
        pr = e / jnp.where(lsum > 0.0, lsum, 1.0)
        ob_ref[g] = _dot(pr.astype(BF16), vc)
        imp = pr[0:tq]
        for hh in range(1, GROUP):
            imp = imp + pr[hh * tq:(hh + 1) * tq]

        sc = _dot_exact01(imp, ov_ref[...])
        tpos = q0 + lax.broadcasted_iota(jnp.int32, (tq, LANES), 0)
        qblk = tpos // SLC_BLOCK
        forced = (lane == 0) | (lane == qblk) | (lane == qblk - 1)
        elig = lane * SLC_BLOCK <= tpos
        sc = jnp.where(forced, jnp.inf, jnp.where(elig, sc, -jnp.inf))
        n_blk = kvb_ref.shape[1] // SLC_BLOCK
        sc_t = jnp.transpose(sc)[0:n_blk]
        sel_t = _rank_select(sc_t, n_blk, N_SELECT)
        blk_t = lax.broadcasted_iota(jnp.int32, sc_t.shape, 0)
        tpos_t = q0 + lax.broadcasted_iota(jnp.int32, sc_t.shape, 1)
        drop_t = jnp.where(blk_t * SLC_BLOCK <= tpos_t, 1.0 - sel_t, 1.0)
        if n_blk < LANES:
            drop_t = jnp.concatenate([drop_t, jnp.zeros((LANES - n_blk, tq), F32)], axis=0)
        drop = jnp.transpose(drop_t).astype(BF16)
        drop4 = jnp.concatenate([drop] * GROUP, axis=0)

        qa = jnp.concatenate([stack(qr, g), drop4], axis=1)
        _flash_init(m_ref, l_ref, acc_ref)

        def slc_tile(k0, diagonal):
            ka = jnp.concatenate([kvb_ref[0, pl.ds(k0, tq), 0:LANES], cm_ref[pl.ds(k0, tq), :]],
                                 axis=1)
            st = _dot_nt(qa, ka)
            if diagonal:
                st = jnp.where(rel_pos(k0, tq) >= 0, st, NEG)
            _flash_step(st, kvb_ref[0, pl.ds(k0, tq), LANES:2 * LANES], m_ref, l_ref, acc_ref)

        def slc_body(kk, carry):
            slc_tile(pl.multiple_of(kk * tq, tq), False)
            return carry

        lax.fori_loop(0, i, slc_body, 0)
        slc_tile(pl.multiple_of(q0, tq), True)
        ob_ref[N_KV + g] = acc_ref[...] / l_ref[...]

        qw = stack(qr, g)
        _flash_init(m_ref, l_ref, acc_ref)

        def win_tile(k0, masked):
            st = _dot_nt(qw, kvb_ref[0, pl.ds(k0, tq), 2 * LANES:3 * LANES])
            if masked:
                d = rel_pos(k0, tq)
                st = jnp.where((d >= 0) & (d < WINDOW), st, NEG)
            _flash_step(st, kvb_ref[0, pl.ds(k0, tq), 3 * LANES:4 * LANES], m_ref, l_ref, acc_ref)

        win_tile(pl.multiple_of(q0, tq), True)
        n_back = WINDOW // tq
        for back in range(1, n_back + 1):
            @pl.when(i >= back)
            def _():
                win_tile(pl.multiple_of(q0 - back * tq, tq), back == n_back)
        ob_ref[2 * N_KV + g] = acc_ref[...] / l_ref[...]

    gh, gl = _split_bf16(ng_ref[0])
    for j in range(GROUP):
        tot = None
        for br in range(3):
            gx = gx_ref[:, (br * GROUP + j) * LANES:(br * GROUP + j + 1) * LANES]
            gate = _dot(gh, gx) + _dot(gl, gx)
            o_lo = ob_ref[br * N_KV + 0, j * tq:(j + 1) * tq, :]
            o_hi = ob_ref[br * N_KV + 1, j * tq:(j + 1) * tq, :]
            term = gate * jnp.where(low, o_lo, o_hi)
            tot = term if tot is None else tot + term
        o_ref[0, :, j * LANES:(j + 1) * LANES] = tot.astype(BF16)


def _nsa_prompt_call(q3, qr3, kvb3, kc, vc, ng3, pw):
    b, t, _ = q3.shape
    tq = min(TQ, t)
    n_cmp = kc.shape[1]
    blk = lambda w: pl.BlockSpec((1, tq, w), lambda bi, i: (bi, i, 0))
    per_b = lambda r, w: pl.BlockSpec((1, r, w), lambda bi, i: (bi, 0, 0))
    return pl.pallas_call(
        _nsa_prompt_kernel,
        grid=(b, t // tq),
        in_specs=[blk(512), blk(512), per_b(t, 512), per_b(n_cmp, LANES), per_b(n_cmp, LANES),
                  blk(LANES), _const_spec((n_cmp, LANES)), _const_spec((t, LANES)),
                  _const_spec((LANES, 3 * GROUP * LANES))],
        out_specs=blk(512),
        out_shape=jax.ShapeDtypeStruct((b, t, 512), BF16),
        scratch_shapes=[pltpu.VMEM((GROUP * tq, LANES), F32)] * 3
        + [pltpu.VMEM((3 * N_KV, GROUP * tq, LANES), F32)],
        compiler_params=pltpu.CompilerParams(dimension_semantics=("arbitrary", "arbitrary"),
                                             vmem_limit_bytes=VMEM_LIMIT),
        name="nsa_prompt",
    )(q3, qr3, kvb3, kc, vc, ng3, pw['ov_p'], pw['cm_p'], pw['gate_x'])


def _memkv_kernel(mem_ref, nm_ref, w_ref, kg_ref, kv_out):
    x = mem_ref[0]
    ms = jnp.sum(x * x, axis=-1, keepdims=True) * (1.0 / D_MODEL)
    xn = (x * lax.rsqrt(ms + EPS) * nm_ref[...]).astype(BF16)
    kv = _dot(xn, w_ref[...])
    for h in range(MEM_HEADS):
        k = kv[:, h * LANES:(h + 1) * LANES]
        kms = jnp.sum(k * k, axis=-1, keepdims=True) * (1.0 / MEM_HEAD_DIM)
        kv_out[0, :, h * LANES:(h + 1) * LANES] = k * lax.rsqrt(kms + EPS) * kg_ref[...]
    kv_out[0, :, BRANCH_W:2 * BRANCH_W] = kv[:, BRANCH_W:2 * BRANCH_W]


def _memkv_call(mem, pw):
    b, m, _ = mem.shape
    return pl.pallas_call(
        _memkv_kernel,
        grid=(b,),
        in_specs=[pl.BlockSpec((1, m, D_MODEL), lambda i: (i, 0, 0)), _const_spec((1, D_MODEL)),
                  _const_spec((D_MODEL, 2 * BRANCH_W)), _const_spec((1, LANES))],
        out_specs=pl.BlockSpec((1, m, 2 * BRANCH_W), lambda i: (i, 0, 0)),
        out_shape=jax.ShapeDtypeStruct((b, m, 2 * BRANCH_W), F32),
        compiler_params=pltpu.CompilerParams(dimension_semantics=("arbitrary",),
                                             vmem_limit_bytes=VMEM_LIMIT),
        name="mem_kv",
    )(mem, pw['nmem'], pw['w_mem'], pw['mk_gain'])


def _mem_attend(mq, kv_ref):
    outs = []
    for h in range(MEM_HEADS):
        k = kv_ref[:, h * LANES:(h + 1) * LANES].astype(BF16)
        v = kv_ref[:, BRANCH_W + h * LANES:BRANCH_W + (h + 1) * LANES].astype(BF16)
        s = _dot_nt(mq[:, h * LANES:(h + 1) * LANES], k) * MEM_HEAD_DIM ** -0.5
        e = jnp.exp(s - jnp.max(s, axis=1, keepdims=True))
        pr = e / jnp.sum(e, axis=1, keepdims=True)
        outs.append(_dot(pr.astype(BF16), v))
    return jnp.concatenate(outs, axis=1)


def _mem_attn_kernel(mq_ref, kv_ref, o_ref):
    o_ref[0] = _mem_attend(mq_ref[0], kv_ref.at[0]).astype(BF16)


def _mem_attn_call(mq3, mkv):
    b, t, _ = mq3.shape
    tq = min(TM, t)
    return pl.pallas_call(
        _mem_attn_kernel,
        grid=(b, t // tq),
        in_specs=[pl.BlockSpec((1, tq, 512), lambda bi, i: (bi, i, 0)),
                  pl.BlockSpec((1, MEM_LEN, 2 * BRANCH_W), lambda bi, i: (bi, 0, 0))],
        out_specs=pl.BlockSpec((1, tq, 512), lambda bi, i: (bi, i, 0)),
        out_shape=jax.ShapeDtypeStruct((b, t, 512), BF16),
        compiler_params=pltpu.CompilerParams(dimension_semantics=("arbitrary", "arbitrary"),
                                             vmem_limit_bytes=VMEM_LIMIT),
        name="mem_attn",
    )(mq3, mkv)


def _finish_kernel(x_ref, on_ref, z_ref, om_ref, nmix_ref, wmg_ref, wbr_ref, wout_ref, h_out):
    x = x_ref[...]
    ms = jnp.sum(x * x, axis=-1, keepdims=True) * (1.0 / D_MODEL)
    xn = (x * lax.rsqrt(ms + EPS) * nmix_ref[...]).astype(BF16)
    merged = None
    for n, br_ref in enumerate((on_ref, z_ref, om_ref)):
        gate = jax.nn.sigmoid(_dot(xn, wmg_ref[:, n * D_MODEL:(n + 1) * D_MODEL]))
        term = gate * _dot(br_ref[...], wbr_ref[n])
        merged = term if merged is None else merged + term
    h_out[...] = x + _dot(merged.astype(BF16), wout_ref[...])


def _finish_call(x, o_nsa, z, o_mem, pw):
    n = x.shape[0]
    tm = min(TM, n)
    tok = lambda w: pl.BlockSpec((tm, w), lambda i: (i, 0))
    return pl.pallas_call(
        _finish_kernel,
        grid=(n // tm,),
        in_specs=[tok(D_MODEL), tok(512), tok(512), tok(512), _const_spec((1, D_MODEL)),
                  _const_spec((D_MODEL, N_BRANCH * D_MODEL)),
                  _const_spec((N_BRANCH, BRANCH_W, D_MODEL)), _const_spec((D_MODEL, D_MODEL))],
        out_specs=tok(D_MODEL),
        out_shape=jax.ShapeDtypeStruct((n, D_MODEL), F32),
        compiler_params=pltpu.CompilerParams(dimension_semantics=("arbitrary",),
                                             vmem_limit_bytes=VMEM_LIMIT),
        name="finish",
    )(x, o_nsa, z, o_mem, pw['nmix'], pw['w_mg'], pw['w_br'], pw['w_out'])


def _ffn_kernel(h_ref, nf_ref, wgu_ref, wd_ref, y_out):
    h = h_ref[...]
    ms = jnp.sum(h * h, axis=-1, keepdims=True) * (1.0 / D_MODEL)
    hn = (h * lax.rsqrt(ms + EPS) * nf_ref[...]).astype(BF16)
    acc = h
    for c in range(D_FF // FF_CHUNK):
        g = _dot(hn, wgu_ref[:, c * FF_CHUNK:(c + 1) * FF_CHUNK])
        u = _dot(hn, wgu_ref[:, D_FF + c * FF_CHUNK:D_FF + (c + 1) * FF_CHUNK])
        a = (jax.nn.silu(g) * u).astype(BF16)
        acc = acc + _dot(a, wd_ref[c * FF_CHUNK:(c + 1) * FF_CHUNK, :])
    y_out[...] = acc


def _ffn_call(h, pw):
    n = h.shape[0]
    tm = min(TM, n)
    tok = pl.BlockSpec((tm, D_MODEL), lambda i: (i, 0))
    return pl.pallas_call(
        _ffn_kernel,
        grid=(n // tm,),
        in_specs=[tok, _const_spec((1, D_MODEL)), _const_spec((D_MODEL, 2 * D_FF)),
                  _const_spec((D_FF, D_MODEL))],
        out_specs=tok,
        out_shape=jax.ShapeDtypeStruct((n, D_MODEL), F32),
        compiler_params=pltpu.CompilerParams(dimension_semantics=("arbitrary",),
                                             vmem_limit_bytes=VMEM_LIMIT),
        name="ffn",
    )(h, pw['nffn'], pw['w_gu'], pw['w_down'])


def _nsa_sample_kernel(layer_base, pt_ref, pool_ref, q_ref, qr_ref, rows_ref, wnew_ref, wbuf_ref,
                       mq_ref, mkv_ref, ng_ref, w1_ref, pe_ref, w2_ref, kg_ref, e64_ref, ov_ref,
                       cm_ref, gx_ref,
                       o_ref, wout_ref, om_ref,
                       kraw_ref, vraw_ref, slc_ref, cmp_sem, slc_sem):
    b = pl.program_id(0)
    nb = pl.num_programs(0)
    n_pages = pt_ref.shape[1]
    n_past = n_pages * PAGE_SIZE
    n_sub = n_past // CMP_STRIDE

    def cmp_copies(bb, slot, j):
        page = layer_base + pt_ref[bb, j]
        dst = pl.ds(pl.multiple_of(j * PAGE_SIZE, PAGE_SIZE), PAGE_SIZE)
        return (pltpu.make_async_copy(pool_ref.at[page, :, 0:LANES],
                                      kraw_ref.at[slot, dst, :], cmp_sem.at[slot]),
                pltpu.make_async_copy(pool_ref.at[page, :, LANES:2 * LANES],
                                      vraw_ref.at[slot, dst, :], cmp_sem.at[slot]))

    def slc_copy(bb, j):
        page = layer_base + pt_ref[bb, j]
        dst = pl.ds(pl.multiple_of(j * PAGE_SIZE, PAGE_SIZE), PAGE_SIZE)
        return pltpu.make_async_copy(pool_ref.at[page, :, 2 * LANES:4 * LANES],
                                     slc_ref.at[dst, :], slc_sem.at[0])

    def start_cmp(bb, slot):
        def body(j, c):
            for cp in cmp_copies(bb, slot, j):
                cp.start()
            return c
        lax.fori_loop(0, n_pages, body, 0)

    def wait_cmp(bb, slot):
        def body(j, c):
            for cp in cmp_copies(bb, slot, j):
                cp.wait()
            return c
        lax.fori_loop(0, n_pages, body, 0)

    slot = b % 2

    @pl.when(b == 0)
    def _():
        start_cmp(b, slot)

    def slc_start_body(j, c):
        slc_copy(b, j).start()
        return c
    lax.fori_loop(0, n_pages, slc_start_body, 0)

    @pl.when(b + 1 < nb)
    def _():
        start_cmp(b + 1, 1 - slot)

    wait_cmp(b, slot)

    row8 = lax.broadcasted_iota(jnp.int32, (8, LANES), 0)
    lane8 = lax.broadcasted_iota(jnp.int32, (8, LANES), 1)
    low = lane8 < HEAD_DIM
    own_half = (row8 < GROUP) == low

    def stack8(qrow):
        out = jnp.zeros((8, LANES), F32)
        for j in range(GROUP):
            pj = jnp.broadcast_to(qrow[:, j * LANES:(j + 1) * LANES], (8, LANES))
            out = jnp.where((row8 & (GROUP - 1)) == j, pj, out)
        return jnp.where(own_half, out, 0.0).astype(BF16)

    q8 = stack8(q_ref[0])
    qr8 = stack8(qr_ref[0])
    e64k = e64_ref[...]

    kc, vc = _compress(kraw_ref.at[slot], vraw_ref.at[slot], n_sub, w1_ref, pe_ref, w2_ref,
                       kg_ref, e64k)
    s = _dot_nt(q8, kc.astype(BF16))
    col = lax.broadcasted_iota(jnp.int32, s.shape, 1)
    rowc = lax.broadcasted_iota(jnp.int32, s.shape, 0)
    vis = col < n_sub - CMP_RATIO + 1
    sm = jnp.where(vis, s, NEG)
    e = jnp.where(vis, jnp.exp(sm - jnp.max(sm, axis=1, keepdims=True)), 0.0)
    pr = e / jnp.sum(e, axis=1, keepdims=True)
    o_c = _dot(pr.astype(BF16), vc.astype(BF16))
    imp8 = jnp.zeros(s.shape, F32)
    for g in range(N_KV):
        imp_g = jnp.sum(jnp.where(rowc // GROUP == g, pr, 0.0), axis=0, keepdims=True)
        imp8 = jnp.where(rowc == g, jnp.broadcast_to(imp_g, s.shape), imp8)

    n_blk_pad = ov_ref.shape[1]
    q_blk = n_past // SLC_BLOCK
    sc = _dot_exact01(imp8, ov_ref[...])
    bl = lax.broadcasted_iota(jnp.int32, sc.shape, 1)
    forced = (bl == 0) | (bl == q_blk) | (bl == q_blk - 1)
    sc = jnp.where(forced, jnp.inf, jnp.where(bl <= q_blk, sc, -jnp.inf))
    sc_t = jnp.transpose(sc)
    drops = []
    for g in range(N_KV):
        colv = jnp.broadcast_to(sc_t[:, g:g + 1], (n_blk_pad, n_blk_pad))
        rowv = jnp.broadcast_to(sc[g:g + 1, :], (n_blk_pad, n_blk_pad))
        ii = lax.broadcasted_iota(jnp.int32, colv.shape, 0)
        jj = lax.broadcasted_iota(jnp.int32, colv.shape, 1)
        tie = jnp.where(ii < jj, 1.0, 0.0)
        beats = jnp.where(colv > rowv, 1.0, jnp.where(colv == rowv, tie, 0.0))
        cnt = jnp.sum(beats, axis=0, keepdims=True)
        drops.append(jnp.where((cnt < N_SELECT) & (bl[0:1] <= q_blk), 0.0, 1.0))
    drop8 = jnp.where(row8 < GROUP, jnp.broadcast_to(drops[0][:, 0:LANES], (8, LANES)),
                      jnp.broadcast_to(drops[1][:, 0:LANES], (8, LANES))).astype(BF16)

    def slc_wait_body(j, c):
        slc_copy(b, j).wait()
        return c
    lax.fori_loop(0, n_pages, slc_wait_body, 0)

    qa = jnp.concatenate([qr8, drop8], axis=1)
    ka = jnp.concatenate([slc_ref[:, 0:LANES].astype(BF16), cm_ref[...]], axis=1)
    s_past = _dot_nt(qa, ka)
    rows = rows_ref[0]
    k_new = rows[:, 2 * LANES:3 * LANES].astype(BF16).astype(F32)
    v_new = rows[:, 3 * LANES:4 * LANES].astype(BF16).astype(F32)
    s_new = jnp.sum(qr8.astype(F32) * k_new, axis=1, keepdims=True)
    m = jnp.maximum(jnp.max(s_past, axis=1, keepdims=True), s_new)
    e_past = jnp.exp(s_past - m)
    e_new = jnp.exp(s_new - m)
    den = jnp.sum(e_past, axis=1, keepdims=True) + e_new
    pr_past = (e_past / den).astype(BF16)
    pr_new = (e_new / den).astype(BF16).astype(F32)
    o_s = _dot(pr_past, slc_ref[:, LANES:2 * LANES].astype(BF16)) + pr_new * v_new

    wb = wbuf_ref[0]
    wrow = lax.broadcasted_iota(jnp.int32, wb.shape, 0)
    w_upd = jnp.where(wrow == wb.shape[0] - 1, wnew_ref[0], pltpu.roll(wb, wb.shape[0] - 1, 0))
    wout_ref[0] = w_upd
    s_w = _dot_nt(qr8, w_upd[:, 0:LANES].astype(BF16))
    e_w = jnp.exp(s_w - jnp.max(s_w, axis=1, keepdims=True))
    pr_w = e_w / jnp.sum(e_w, axis=1, keepdims=True)
    o_w = _dot(pr_w.astype(BF16), w_upd[:, LANES:2 * LANES].astype(BF16))

    gh, gl = _split_bf16(jnp.broadcast_to(ng_ref[0], (8, LANES)))
    tot = jnp.zeros((8, LANES), F32)
    for br, o8 in enumerate((o_c, o_s, o_w)):
        pair = jnp.where(low, o8, pltpu.roll(o8, GROUP, 0))
        gate8 = jnp.zeros((8, LANES), F32)
        for j in range(GROUP):
            gx = gx_ref[:, (br * GROUP + j) * LANES:(br * GROUP + j + 1) * LANES]
            gate8 = jnp.where(row8 == j, _dot(gh, gx) + _dot(gl, gx), gate8)
        tot = tot + gate8 * pair
    o_ref[0] = tot

    mq8 = jnp.broadcast_to(mq_ref[0], (8, BRANCH_W)).astype(BF16)
    om_ref[0] = _mem_attend(mq8, mkv_ref.at[0])


def _nsa_sample_call(layer, page_table, pool3, q4, qr4, rows_s, win_new, win_buf, mq4, mkv,
                     ng_s, pw):
    nb, n_pages = page_table.shape
    n_past = n_pages * PAGE_SIZE
    n_phys = pool3.shape[0] // DEPTH
    w_buf = win_buf.shape[1]
    n_blk_pad = pw['ov_s'].shape[1]
    per_b = lambda *shape: pl.BlockSpec((1,) + shape, lambda b, pt: (b,) + (0,) * len(shape))
    grid_spec = pltpu.PrefetchScalarGridSpec(
        num_scalar_prefetch=1,
        grid=(nb,),
        in_specs=[pl.BlockSpec(memory_space=pl.ANY),
                  per_b(1, 512), per_b(1, 512), per_b(1, 512), per_b(1, 256),
                  per_b(w_buf, 256), per_b(1, 512), per_b(MEM_LEN, 2 * BRANCH_W),
                  per_b(1, LANES),
                  _const_spec((2 * CMP_RATIO * CMP_STRIDE, LANES, 2 * CMP_HIDDEN)),
                  _const_spec((2 * CMP_RATIO * CMP_STRIDE, LANES)),
                  _const_spec((2, 2 * CMP_HIDDEN, LANES)),
                  _const_spec((1, LANES)), _const_spec((LANES, LANES)),
                  _const_spec((n_past // CMP_STRIDE, n_blk_pad)), _const_spec((n_past, LANES)),
                  _const_spec((LANES, 3 * GROUP * LANES))],
        out_specs=[per_b(8, LANES), per_b(w_buf, 256), per_b(8, BRANCH_W)],
        scratch_shapes=[pltpu.VMEM((2, n_past, LANES), F32), pltpu.VMEM((2, n_past, LANES), F32),
                        pltpu.VMEM((n_past, 2 * LANES), F32),
                        pltpu.SemaphoreType.DMA((2,)), pltpu.SemaphoreType.DMA((1,))],
    )
    return pl.pallas_call(
        functools.partial(_nsa_sample_kernel, layer * n_phys),
        grid_spec=grid_spec,
        out_shape=[jax.ShapeDtypeStruct((nb, 8, LANES), F32),
                   jax.ShapeDtypeStruct((nb, w_buf, 256), F32),
                   jax.ShapeDtypeStruct((nb, 8, BRANCH_W), F32)],
        compiler_params=pltpu.CompilerParams(dimension_semantics=("arbitrary",),
                                             vmem_limit_bytes=VMEM_LIMIT),
        name="nsa_sample",
    )(page_table, pool3, q4, qr4, rows_s, win_new, win_buf, mq4, mkv, ng_s,
      pw['cmp_w1'], pw['cmp_pe'], pw['cmp_w2'], pw['kc_gain'], pw['e64k'], pw['ov_s'],
      pw['cm_s'], pw['gate_x'])


def _pair_perm():
    p = np.arange(BRANCH_W)
    j, ln = p // LANES, p % LANES
    head = np.where(ln < HEAD_DIM, j, GROUP + j)
    return head * HEAD_DIM + ln % HEAD_DIM


def _block_diag_ones(width, group):
    i = np.arange(width)
    return (i[:, None] // group == i[None, :] // group).astype(np.float32)


def _overlap(n_cmp_rows, n_cmp_valid, n_blk_cols):
    i = np.arange(n_cmp_rows)[:, None]
    j = np.arange(n_blk_cols)[None, :]
    ov = (i * CMP_STRIDE < (j + 1) * SLC_BLOCK) & (i * CMP_STRIDE + CMP_LEN > j * SLC_BLOCK)
    return (ov & (i < n_cmp_valid)).astype(np.float32)


def _block_mask_rows(n_keys):
    k = np.arange(n_keys)[:, None]
    j = np.arange(LANES)[None, :]
    return np.where(k // SLC_BLOCK == j, NEG, 0.0).astype(np.float32)


def _gate_expand():
    x = np.zeros((LANES, 3 * GROUP * LANES), np.float32)
    for br in range(3):
        for j in range(GROUP):
            for ln in range(LANES):
                head = j if ln < HEAD_DIM else GROUP + j
                x[br * N_HEADS + head, (br * GROUP + j) * LANES + ln] = 1.0
    return x


def _rope_tables(pos):
    half = ROT_DIM // 2
    inv = ROPE_THETA ** (-jnp.arange(half, dtype=F32) / half)
    ang = pos.astype(F32)[:, None] * inv[None, :]
    cos, sin = jnp.cos(ang), jnp.sin(ang)
    n = pos.shape[0]
    one = jnp.ones((n, HEAD_DIM - ROT_DIM), F32)
    zero = jnp.zeros((n, HEAD_DIM - ROT_DIM), F32)
    zh = jnp.zeros((n, half), F32)
    c = jnp.concatenate([cos, cos, one], axis=1)
    s1 = jnp.concatenate([zh, sin, zero], axis=1)
    s2 = jnp.concatenate([-sin, zh, zero], axis=1)
    tab = jnp.stack([c, s1, s2])
    return jnp.concatenate([tab, tab], axis=2)


def _layer_params(l, seq_len, n_past, norm_mix, w_in, q_norm, k_norm, cmp_pe, cmp_w1, cmp_w2,
                  conv_w, norm_mem, w_mem_kv, mem_q_norm, mem_k_norm, w_branch, w_out, norm_ffn,
                  w_gate_up, w_down):
    perm = _pair_perm()
    wi = w_in[l]
    o_kv, o_ng = BRANCH_W, BRANCH_W + 768
    o_cx = o_ng + 3 * N_HEADS
    o_cb, o_cc, o_mq, o_mg = o_cx + 512, o_cx + 1024, o_cx + 1536, o_cx + 2048
    w_a = jnp.concatenate([
        wi[:, perm], wi[:, o_kv:o_ng], wi[:, o_cx:o_cb], wi[:, o_cb:o_cc], wi[:, o_cc:o_mq],
        wi[:, o_mq:o_mg], wi[:, o_ng:o_cx], jnp.zeros((D_MODEL, LANES - 3 * N_HEADS), F32)],
        axis=1).astype(BF16)
    w1 = cmp_w1[l].reshape(2, CMP_RATIO, CMP_STRIDE, HEAD_DIM, CMP_HIDDEN)
    zw = jnp.zeros_like(w1)
    w1bd = jnp.concatenate([jnp.concatenate([w1, zw], axis=-1),
                            jnp.concatenate([zw, w1], axis=-1)], axis=-2)
    w2 = cmp_w2[l]
    z2 = jnp.zeros_like(w2)
    w2bd = jnp.concatenate([jnp.concatenate([w2, z2], axis=-1),
                            jnp.concatenate([z2, w2], axis=-1)], axis=-2)
    pe = cmp_pe[l].reshape(2 * CMP_RATIO * CMP_STRIDE, HEAD_DIM)
    n_sub_p = seq_len // CMP_STRIDE
    n_sub_s = n_past // CMP_STRIDE
    n_blk_s = -(-(n_past // SLC_BLOCK + 1) // LANES) * LANES
    wbr = w_branch[l]
    return dict(
        nmix=norm_mix[l][None, :], w_a=w_a, w_mg=wi[:, o_mg:].astype(BF16),
        q_gain=jnp.tile(q_norm[l], N_HEADS)[None, :],
        k_gain=jnp.stack([jnp.tile(k_norm[l, 1], N_KV), jnp.tile(k_norm[l, 2], N_KV)]),
        kc_gain=jnp.tile(k_norm[l, 0], N_KV)[None, :],
        mq_gain=jnp.tile(mem_q_norm[l], MEM_HEADS)[None, :],
        mk_gain=mem_k_norm[l][None, :],
        conv_w=conv_w[l],
        cmp_w1=w1bd.reshape(2 * CMP_RATIO * CMP_STRIDE, LANES, 2 * CMP_HIDDEN).astype(BF16),
        cmp_pe=jnp.concatenate([pe, pe], axis=1),
        cmp_w2=w2bd.astype(BF16),
        nmem=norm_mem[l][None, :], w_mem=w_mem_kv[l].astype(BF16),
        w_br=jnp.concatenate([wbr[0][perm][None], wbr[1:]], axis=0).astype(BF16),
        w_out=w_out[l].astype(BF16),
        nffn=norm_ffn[l][None, :], w_gu=w_gate_up[l].astype(BF16), w_down=w_down[l].astype(BF16),
        e64=jnp.asarray(_block_diag_ones(512, HEAD_DIM), BF16),
        e64k=jnp.asarray(_block_diag_ones(LANES, HEAD_DIM), BF16),
        e128=jnp.asarray(_block_diag_ones(512, MEM_HEAD_DIM), BF16),
        ov_p=jnp.asarray(_overlap(n_sub_p, n_sub_p - CMP_RATIO + 1, LANES), BF16),
        cm_p=jnp.asarray(_block_mask_rows(seq_len), BF16),
        ov_s=jnp.asarray(_overlap(n_sub_s, n_sub_s - CMP_RATIO + 1, n_blk_s), BF16),
        cm_s=jnp.asarray(_block_mask_rows(n_past), BF16),
        gate_x=jnp.asarray(_gate_expand(), BF16),
    )


def kernel(x_prompt, x_sample, cache_nsa_kv, cache_win_kv, state_conv, cache_mem_kv, page_table,
           mem_prompt, norm_mix, w_in, q_norm, k_norm, cmp_pe, cmp_w1, cmp_w2, conv_w, norm_mem,
           w_mem_kv, mem_q_norm, mem_k_norm, w_branch, w_out, norm_ffn, w_gate_up, w_down):
    bp, t, _ = x_prompt.shape
    bs, ts, _ = x_sample.shape
    assert ts == 1 and t % TM == 0 and t % TQ == 0
    n_pages = page_table.shape[1]
    n_past = n_pages * PAGE_SIZE
    w_buf = cache_win_kv.shape[2]
    assert w_buf == WINDOW and n_past >= WINDOW and (n_past // SLC_BLOCK) == LANES

    xp = x_prompt.reshape(bp * t, D_MODEL)
    xs = x_sample.reshape(bs, D_MODEL)
    rope_p = _rope_tables(jnp.arange(t))
    rope_s = _rope_tables(jnp.full((bs,), n_past, jnp.int32))
    pool3 = cache_nsa_kv.reshape(DEPTH * cache_nsa_kv.shape[1], PAGE_SIZE, 4 * N_KV * HEAD_DIM)

    rows_p, rows_s, win_p, win_s, conv_p, conv_s, mem_p = [], [], [], [], [], [], []
    for l in range(DEPTH):
        pw = _layer_params(l, t, n_past, norm_mix, w_in, q_norm, k_norm, cmp_pe, cmp_w1, cmp_w2,
                           conv_w, norm_mem, w_mem_kv, mem_q_norm, mem_k_norm, w_branch, w_out,
                           norm_ffn, w_gate_up, w_down)
        q, qr, rows, win, kvb, z, utail, mq, ng = _prep_call(xp, pw, rope_p, False, t)
        rows3 = rows.reshape(bp, t, 512)
        kc, vc = _compress_prompt_call(rows3, pw)
        o_nsa = _nsa_prompt_call(q.reshape(bp, t, 512), qr.reshape(bp, t, 512),
                                 kvb.reshape(bp, t, 512), kc, vc, ng.reshape(bp, t, LANES), pw)
        mkv = _memkv_call(mem_prompt, pw)
        o_mem = _mem_attn_call(mq.reshape(bp, t, 512), mkv)
        h = _finish_call(xp, o_nsa.reshape(bp * t, 512), z, o_mem.reshape(bp * t, 512), pw)
        xp = _ffn_call(h, pw)
        rows_p.append(rows3.reshape(bp, t, 4, N_KV, HEAD_DIM))
        win_p.append(win.reshape(bp, t, 2, N_KV, HEAD_DIM)[:, t - min(WINDOW, t):])
        conv_p.append(utail.reshape(bp, t // TM, 8, 512)[:, -1, 8 - (CONV_W - 1):])
        mem_p.append(mkv.reshape(bp, MEM_LEN, 2, MEM_HEADS, MEM_HEAD_DIM))
        q, qr, rows, win, _, z, u, mq, ng = _prep_call(xs, pw, rope_s, True, 1,
                                                       state=state_conv[l])
        o_nsa, new_win, o_mem = _nsa_sample_call(
            l, page_table, pool3, q.astype(F32).reshape(bs, 1, 512),
            qr.astype(F32).reshape(bs, 1, 512), rows.reshape(bs, 1, 512),
            win.reshape(bs, 1, 256), cache_win_kv[l].reshape(bs, w_buf, 256),
            mq.astype(F32).reshape(bs, 1, 512),
            cache_mem_kv[l].reshape(bs, MEM_LEN, 2 * BRANCH_W), ng.reshape(bs, 1, LANES), pw)
        o_nsa = o_nsa[:, 0:GROUP, :].reshape(bs, 512).astype(BF16)
        o_mem = o_mem[:, 0, :].astype(BF16)
        h = _finish_call(xs, o_nsa, z, o_mem, pw)
        xs = _ffn_call(h, pw)
        rows_s.append(rows.reshape(bs, 1, 4, N_KV, HEAD_DIM))
        win_s.append(new_win.reshape(bs, w_buf, 2, N_KV, HEAD_DIM))
        conv_s.append(jnp.stack([state_conv[l][:, 1, :], u], axis=1))

    return (xp.reshape(bp, t, D_MODEL), xs.reshape(bs, 1, D_MODEL), jnp.stack(rows_p),
            jnp.stack(rows_s), jnp.stack(win_p), jnp.stack(win_s), jnp.stack(conv_p),
            jnp.stack(conv_s), jnp.stack(mem_p))
```

```python
import functools

import numpy as np
import jax
import jax.numpy as jnp
from jax import lax
from jax.experimental import pallas as pl
from jax.experimental.pallas import tpu as pltpu

F32 = jnp.float32
BF16 = jnp.bfloat16

D_MODEL = 1024
DEPTH = 2
PAST_LEN = 8192
PAGE_SIZE = 128
HEAD_DIM = 64
BRANCH_W = D_MODEL // 2
N_HEADS = BRANCH_W // HEAD_DIM
N_KV = N_HEADS // 4
GROUP = N_HEADS // N_KV
ROT_DIM = HEAD_DIM // 4
ROPE_THETA = 500000.0
CMP_LEN = 32
CMP_STRIDE = 16
CMP_RATIO = CMP_LEN // CMP_STRIDE
CMP_HIDDEN = 2 * HEAD_DIM
SLC_BLOCK = 64
N_SELECT = 16
WINDOW = 512
CONV_W = 3
MEM_LEN = 256
MEM_HEADS = 4
MEM_HEAD_DIM = BRANCH_W // MEM_HEADS
N_BRANCH = 3
D_FF = ((8 * D_MODEL + 767) // 768) * 256
EPS = 1e-6

LANES = 128
VMEM_LIMIT = 56 * 1024 * 1024

TM = 512
TQ = 512
FF_CHUNK = 256
NEG = -1e30
M_INIT = -3e38

_C_Q = 0
_C_KV = 512
_C_CX = 1280
_C_CB = 1792
_C_CC = 2304
_C_MQ = 2816
_C_NG = 3328
_N_A = 3456


def _const_spec(shape):
    nd = len(shape)
    return pl.BlockSpec(shape, lambda *_: (0,) * nd, pipeline_mode=pl.Buffered(1))


def _dot(a, b):
    return jnp.dot(a, b, preferred_element_type=F32)


def _dot_nt(a, b):
    return lax.dot_general(a, b, (((1,), (1,)), ((), ())), preferred_element_type=F32)


def _split_bf16(v):
    hi = v.astype(BF16)
    lo = (v - hi.astype(F32)).astype(BF16)
    return hi, lo


def _dot_exact01(v, m01):
    hi, lo = _split_bf16(v)
    return _dot(hi, m01) + _dot(lo, m01)


def _group_rms(v, e01, group, gain):
    ms = _dot_exact01(v * v, e01) * (1.0 / group)
    return v * lax.rsqrt(ms + EPS) * gain


def _rope_cols(v, c, s1, s2):
    outs = []
    for j in range(v.shape[1] // LANES):
        col = v[:, j * LANES:(j + 1) * LANES]
        outs.append(col * c + pltpu.roll(col, ROT_DIM // 2, 1) * s1
                    + pltpu.roll(col, LANES - ROT_DIM // 2, 1) * s2)
    return outs[0] if len(outs) == 1 else jnp.concatenate(outs, axis=1)


def _prep_kernel(sample, tiles_per_seq, *refs):
    if sample:
        (x_ref, nmix_ref, w_ref, qg_ref, kg_ref, mqg_ref, rope_ref, e64_ref, e128_ref, cw_ref,
         s0_ref, s1_ref,
         q_out, qr_out, rows_out, win_out, kvb_out, z_out, u_out, mq_out, ng_out) = refs
    else:
        (x_ref, nmix_ref, w_ref, qg_ref, kg_ref, mqg_ref, rope_ref, e64_ref, e128_ref, cw_ref,
         q_out, qr_out, rows_out, win_out, kvb_out, z_out, u_out, mq_out, ng_out,
         carry_ref) = refs
    tm = x_ref.shape[0]
    x = x_ref[...]
    ms = jnp.sum(x * x, axis=-1, keepdims=True) * (1.0 / D_MODEL)
    xn = (x * lax.rsqrt(ms + EPS) * nmix_ref[...]).astype(BF16)

    rc, rs1, rs2 = rope_ref[0], rope_ref[1], rope_ref[2]
    e64 = e64_ref[...]

    q = _dot(xn, w_ref[:, _C_Q:_C_Q + 512])
    qn = _group_rms(q, e64, HEAD_DIM, qg_ref[...])
    q_out[...] = (qn * HEAD_DIM ** -0.5).astype(BF16)
    qr_out[...] = (_rope_cols(qn, rc, rs1, rs2) * HEAD_DIM ** -0.5).astype(BF16)

    kv = _dot(xn, w_ref[:, _C_KV:_C_KV + 768])
    e64k = e64_ref[0:LANES, 0:LANES]
    k_slc = _rope_cols(_group_rms(kv[:, 256:384], e64k, HEAD_DIM, kg_ref[0:1, :]), rc, rs1, rs2)
    k_win = _rope_cols(_group_rms(kv[:, 512:640], e64k, HEAD_DIM, kg_ref[1:2, :]), rc, rs1, rs2)
    v_slc = kv[:, 384:512]
    v_win = kv[:, 640:768]
    rows_out[:, 0:256] = kv[:, 0:256]
    rows_out[:, 256:384] = k_slc
    rows_out[:, 384:512] = v_slc
    win_out[:, 0:128] = k_win
    win_out[:, 128:256] = v_win
    kvb_out[:, 0:128] = k_slc.astype(BF16)
    kvb_out[:, 128:256] = v_slc.astype(BF16)
    kvb_out[:, 256:384] = k_win.astype(BF16)
    kvb_out[:, 384:512] = v_win.astype(BF16)

    cx = _dot(xn, w_ref[:, _C_CX:_C_CX + 512])
    cc = _dot(xn, w_ref[:, _C_CC:_C_CC + 512])
    cb = _dot(xn, w_ref[:, _C_CB:_C_CB + 512])
    u = cc * cx
    w0, w1, w2 = cw_ref[0:1, :], cw_ref[1:2, :], cw_ref[2:3, :]
    if sample:
        z = cb * (s0_ref[...] * w0 + s1_ref[...] * w1 + u * w2)
        u_out[...] = u
    else:
        @pl.when(pl.program_id(0) % tiles_per_seq == 0)
        def _():
            carry_ref[...] = jnp.zeros_like(carry_ref)
        prev = carry_ref[...]
        row = lax.broadcasted_iota(jnp.int32, u.shape, 0)
        u1 = jnp.where(row == 0, prev[7:8, :], pltpu.roll(u, 1, 0))
        u2 = jnp.where(row == 0, prev[6:7, :],
                       jnp.where(row == 1, prev[7:8, :], pltpu.roll(u, 2, 0)))
        z = cb * (u2 * w0 + u1 * w1 + u * w2)
        tail = u[tm - 8:tm, :]
        carry_ref[...] = tail
        u_out[...] = tail
    z_out[...] = z.astype(BF16)

    mq = _dot(xn, w_ref[:, _C_MQ:_C_MQ + 512])
    mq_out[...] = _group_rms(mq, e128_ref[...], MEM_HEAD_DIM, mqg_ref[...]).astype(BF16)

    ng_out[...] = jax.nn.sigmoid(_dot(xn, w_ref[:, _C_NG:_C_NG + 128]))


def _prep_call(x, pw, rope_tab, sample, seq_len, state=None):
    n = x.shape[0]
    tm = n if sample else TM
    tps = 1 if sample else seq_len // tm
    grid = (n // tm,)
    tok = lambda w: pl.BlockSpec((tm, w), lambda i: (i, 0))
    in_specs = [tok(D_MODEL), _const_spec((1, D_MODEL)), _const_spec((D_MODEL, _N_A)),
                _const_spec((1, 512)), _const_spec((2, LANES)), _const_spec((1, 512)),
                pl.BlockSpec((3, tm, LANES), lambda i: (0, i % tps, 0)),
                _const_spec((512, 512)), _const_spec((512, 512)), _const_spec((CONV_W, 512))]
    args = [x, pw['nmix'], pw['w_a'], pw['q_gain'], pw['k_gain'], pw['mq_gain'], rope_tab,
            pw['e64'], pw['e128'], pw['conv_w']]
    scratch = []
    if sample:
        in_specs += [tok(512), tok(512)]
        args += [state[:, 0, :], state[:, 1, :]]
        u_shape, u_spec = (n, 512), tok(512)
    else:
        scratch = [pltpu.VMEM((8, 512), F32)]
        u_shape, u_spec = (n // tm * 8, 512), pl.BlockSpec((8, 512), lambda i: (i, 0))
    out_shape = [jax.ShapeDtypeStruct((n, 512), BF16), jax.ShapeDtypeStruct((n, 512), BF16),
                 jax.ShapeDtypeStruct((n, 512), F32), jax.ShapeDtypeStruct((n, 256), F32),
                 jax.ShapeDtypeStruct((n, 512), BF16), jax.ShapeDtypeStruct((n, 512), BF16),
                 jax.ShapeDtypeStruct(u_shape, F32), jax.ShapeDtypeStruct((n, 512), BF16),
                 jax.ShapeDtypeStruct((n, LANES), F32)]
    out_specs = [tok(512), tok(512), tok(512), tok(256), tok(512), tok(512), u_spec, tok(512),
                 tok(LANES)]
    return pl.pallas_call(
        functools.partial(_prep_kernel, sample, tps),
        grid=grid, in_specs=in_specs, out_specs=out_specs, out_shape=out_shape,
        scratch_shapes=scratch,
        compiler_params=pltpu.CompilerParams(dimension_semantics=("arbitrary",),
                                             vmem_limit_bytes=VMEM_LIMIT),
        name="prep_sample" if sample else "prep_prompt",
    )(*args)


def _compress(load_xs, n_sub, w1_ref, pe_ref, w2_ref, kg_ref, e64k):
    outs = []
    for kv in range(2):
        acc = [None, None]
        for s in range(0, CMP_STRIDE, 2):
            xa, xb = load_xs(kv, s), load_xs(kv, s + 1)
            for r in range(CMP_RATIO):
                idx = (kv * CMP_RATIO + r) * CMP_STRIDE + s
                lhs = jnp.concatenate([(xa + pe_ref[idx:idx + 1, :]).astype(BF16),
                                       (xb + pe_ref[idx + 1:idx + 2, :]).astype(BF16)], axis=1)
                d = _dot(lhs, w1_ref[idx // 2])
                acc[r] = d if acc[r] is None else acc[r] + d
        h = acc[0] + pltpu.roll(acc[1], n_sub - 1, 0)
        o = _dot(jax.nn.silu(h).astype(BF16), w2_ref[kv])
        outs.append(o)
    kc = _group_rms(outs[0], e64k, HEAD_DIM, kg_ref[...])
    return kc, outs[1]


def _compress_prompt_kernel(xk_ref, xv_ref, w1_ref, pe_ref, w2_ref, kg_ref, e64_ref,
                            kc_out, vc_out):
    n_sub = kc_out.shape[1]
    x_refs = (xk_ref, xv_ref)
    kc, vc = _compress(lambda kv, s: x_refs[kv][0, pl.ds(s, n_sub, stride=CMP_STRIDE), :],
                       n_sub, w1_ref, pe_ref, w2_ref, kg_ref, e64_ref[...])
    kc_out[0] = kc.astype(BF16)
    vc_out[0] = vc.astype(BF16)


def _compress_prompt_call(rows3, pw):
    b, t, _ = rows3.shape
    n_sub = t // CMP_STRIDE
    out = jax.ShapeDtypeStruct((b, n_sub, LANES), BF16)
    return pl.pallas_call(
        _compress_prompt_kernel,
        grid=(b,),
        in_specs=[pl.BlockSpec((1, t, LANES), lambda i: (i, 0, 0)),
                  pl.BlockSpec((1, t, LANES), lambda i: (i, 0, 1)),
                  _const_spec((CMP_RATIO * CMP_STRIDE, 2 * LANES, 2 * CMP_HIDDEN)),
                  _const_spec((2 * CMP_RATIO * CMP_STRIDE, LANES)),
                  _const_spec((2, 2 * CMP_HIDDEN, LANES)),
                  _const_spec((1, LANES)), _const_spec((LANES, LANES))],
        out_specs=[pl.BlockSpec((1, n_sub, LANES), lambda i: (i, 0, 0))] * 2,
        out_shape=[out, out],
        compiler_params=pltpu.CompilerParams(dimension_semantics=("arbitrary",),
                                             vmem_limit_bytes=VMEM_LIMIT),
        name="compress_prompt",
    )(rows3, rows3, pw['cmp_w1'], pw['cmp_pe'], pw['cmp_w2'], pw['kc_gain'], pw['e64k'])


def _flash_init(m_ref, l_ref, acc_ref):
    m_ref[...] = jnp.full(m_ref.shape, M_INIT, F32)
    l_ref[...] = jnp.zeros(l_ref.shape, F32)
    acc_ref[...] = jnp.zeros(acc_ref.shape, F32)


def _flash_step(s, v, m_ref, l_ref, acc_ref):
    m_prev = m_ref[...]
    m_next = jnp.maximum(m_prev, jnp.max(s, axis=1, keepdims=True))
    p = jnp.exp(s - jnp.concatenate([m_next] * (s.shape[1] // LANES), axis=1))
    alpha = jnp.exp(m_prev - m_next)
    l_ref[...] = alpha * l_ref[...] + jnp.sum(p, axis=1, keepdims=True)
    acc_ref[...] = alpha * acc_ref[...] + _dot(p.astype(BF16), v)
    m_ref[...] = m_next


def _take_top(score_t, n_take):
    nb = score_t.shape[0]
    jidx = lax.broadcasted_iota(jnp.int32, score_t.shape, 0).astype(F32)
    taken = jnp.zeros(score_t.shape, F32)
    s = score_t
    for _ in range(n_take):
        mx = jnp.max(s, axis=0, keepdims=True)
        first = jnp.min(jnp.where(s == mx, jidx, float(nb)), axis=0, keepdims=True)
        pick = jidx == first
        taken = jnp.where(pick, 1.0, taken)
        s = jnp.where(pick, -jnp.inf, s)
    return taken


def _nsa_prompt_kernel(q_ref, qr_ref, kvb_ref, kc_ref, vc_ref, ng_ref, ovt_ref, cm_ref, gx_ref,
                       tri_ref, o_ref, m_ref, l_ref, acc_ref, ob_ref, qa_ref):
    tq = q_ref.shape[1]
    n_cmp = kc_ref.shape[1]
    n_blk = kvb_ref.shape[1] // SLC_BLOCK
    i = pl.program_id(1)
    q0 = i * tq
    lane = lax.broadcasted_iota(jnp.int32, (tq, LANES), 1)
    low = lane < HEAD_DIM

    def stack(qb, g):
        keep = low if g == 0 else jnp.logical_not(low)
        parts = [jnp.where(keep, qb[:, LANES * j:LANES * (j + 1)], jnp.zeros((tq, LANES), BF16))
                 for j in range(GROUP)]
        return jnp.concatenate(parts, axis=0)

    q = q_ref[0]
    qr = qr_ref[0]
    kc = kc_ref[0]
    vc = vc_ref[0]

    rowc = lax.broadcasted_iota(jnp.int32, (tq, n_cmp), 0)
    colc = lax.broadcasted_iota(jnp.int32, (tq, n_cmp), 1)
    vis1 = jnp.where(colc * CMP_STRIDE + (CMP_LEN - 1) <= q0 + rowc, 1.0, 0.0)
    vis = jnp.concatenate([vis1] * GROUP, axis=0) > 0.5

    for g in range(N_KV):
        s = _dot_nt(stack(q, g), kc)
        sm = jnp.where(vis, s, NEG)
        e = jnp.where(vis, jnp.exp(sm - jnp.max(sm, axis=1, keepdims=True)), 0.0)
        lsum = jnp.sum(e, axis=1, keepdims=True)
        pr = e / jnp.where(lsum > 0.0, lsum, 1.0)
        ob_ref[g] = _dot(pr.astype(BF16), vc)
        imp = pr[0:tq]
        for hh in range(1, GROUP):
            imp = imp + pr[hh * tq:(hh + 1) * tq]

        ih, il = _split_bf16(imp)
        sc_t = (_dot_nt(ovt_ref[...], ih) + _dot_nt(ovt_ref[...], il))[0:n_blk]
        blk_t = lax.broadcasted_iota(jnp.int32, sc_t.shape, 0)
        tpos_t = q0 + lax.broadcasted_iota(jnp.int32, sc_t.shape, 1)
        qblk_t = tpos_t // SLC_BLOCK
        forced_t = (blk_t == 0) | (blk_t == qblk_t) | (blk_t == qblk_t - 1)
        elig_t = blk_t * SLC_BLOCK <= tpos_t
        rest_t = _take_top(jnp.where(elig_t & jnp.logical_not(forced_t), sc_t, -jnp.inf),
                           N_SELECT - 3)
        keep_t = elig_t & (forced_t | (rest_t > 0.5))
        drop_t = jnp.where(keep_t, 0.0, 1.0)
        if n_blk < LANES:
            drop_t = jnp.concatenate([drop_t, jnp.zeros((LANES - n_blk, tq), F32)], axis=0)
        drop = jnp.transpose(drop_t).astype(BF16)
        qa_ref[g, :, 0:LANES] = stack(qr, g)
        qa_ref[g, :, LANES:2 * LANES] = jnp.concatenate([drop] * GROUP, axis=0)

    def flash(g):
        return m_ref.at[g], l_ref.at[g], acc_ref.at[g]

    def tile4(bias):
        return jnp.concatenate([bias] * GROUP, axis=0)

    for g in range(N_KV):
        _flash_init(*flash(g))

    def slc_tile(k0, diagonal):
        ka = jnp.concatenate([kvb_ref[0, pl.ds(k0, tq), 0:LANES], cm_ref[pl.ds(k0, tq), :]],
                             axis=1)
        v = kvb_ref[0, pl.ds(k0, tq), LANES:2 * LANES]
        for g in range(N_KV):
            st = _dot_nt(qa_ref[g], ka)
            if diagonal:
                st = st + tile4(tri_ref[0])
            _flash_step(st, v, *flash(g))

    def slc_body(kk, carry):
        slc_tile(pl.multiple_of(kk * tq, tq), False)
        return carry

    lax.fori_loop(0, i, slc_body, 0)
    slc_tile(pl.multiple_of(q0, tq), True)
    for g in range(N_KV):
        ob_ref[N_KV + g] = acc_ref[g] / l_ref[g]

    for g in range(N_KV):
        _flash_init(*flash(g))

    def win_tile(k0, bias):
        k = kvb_ref[0, pl.ds(k0, tq), 2 * LANES:3 * LANES]
        v = kvb_ref[0, pl.ds(k0, tq), 3 * LANES:4 * LANES]
        for g in range(N_KV):
            st = _dot_nt(qa_ref[g, :, 0:LANES], k)
            if bias is not None:
                st = st + tile4(tri_ref[bias])
            _flash_step(st, v, *flash(g))

    win_tile(pl.multiple_of(q0, tq), 0)
    n_back = WINDOW // tq
    for back in range(1, n_back + 1):
        @pl.when(i >= back)
        def _():
            win_tile(pl.multiple_of(q0 - back * tq, tq), 1 if back == n_back else None)
    for g in range(N_KV):
        ob_ref[2 * N_KV + g] = acc_ref[g] / l_ref[g]

    gh, gl = _split_bf16(ng_ref[0])
    for j in range(GROUP):
        tot = None
        for br in range(3):
            gx = gx_ref[:, (br * GROUP + j) * LANES:(br * GROUP + j + 1) * LANES]
            gate = _dot(gh, gx) + _dot(gl, gx)
            o_lo = ob_ref[br * N_KV + 0, j * tq:(j + 1) * tq, :]
            o_hi = ob_ref[br * N_KV + 1, j * tq:(j + 1) * tq, :]
            term = gate * jnp.where(low, o_lo, o_hi)
            tot = term if tot is None else tot + term
        o_ref[0, :, j * LANES:(j + 1) * LANES] = tot.astype(BF16)


def _nsa_prompt_call(q3, qr3, kvb3, kc, vc, ng3, pw):
    b, t, _ = q3.shape
    tq = min(TQ, t)
    n_cmp = kc.shape[1]
    blk = lambda w: pl.BlockSpec((1, tq, w), lambda bi, i: (bi, i, 0))
    per_b = lambda r, w: pl.BlockSpec((1, r, w), lambda bi, i: (bi, 0, 0))
    return pl.pallas_call(
        _nsa_prompt_kernel,
        grid=(b, t // tq),
        in_specs=[blk(512), blk(512), per_b(t, 512), per_b(n_cmp, LANES), per_b(n_cmp, LANES),
                  blk(LANES), _const_spec((LANES, n_cmp)), _const_spec((t, LANES)),
                  _const_spec((LANES, 3 * GROUP * LANES)), _const_spec((2, tq, tq))],
        out_specs=blk(512),
        out_shape=jax.ShapeDtypeStruct((b, t, 512), BF16),
        scratch_shapes=[pltpu.VMEM((N_KV, GROUP * tq, LANES), F32)] * 3
        + [pltpu.VMEM((3 * N_KV, GROUP * tq, LANES), F32),
           pltpu.VMEM((N_KV, GROUP * tq, 2 * LANES), BF16)],
        compiler_params=pltpu.CompilerParams(dimension_semantics=("arbitrary", "arbitrary"),
                                             vmem_limit_bytes=VMEM_LIMIT),
        name="nsa_prompt",
    )(q3, qr3, kvb3, kc, vc, ng3, pw['ov_p'], pw['cm_p'], pw['gate_x'],
      jnp.asarray(_band_bias(tq), F32))


def _memkv_kernel(mem_ref, nm_ref, w_ref, kg_ref, kv_out):
    x = mem_ref[0]
    ms = jnp.sum(x * x, axis=-1, keepdims=True) * (1.0 / D_MODEL)
    xn = (x * lax.rsqrt(ms + EPS) * nm_ref[...]).astype(BF16)
    kv = _dot(xn, w_ref[...])
    for h in range(MEM_HEADS):
        k = kv[:, h * LANES:(h + 1) * LANES]
        kms = jnp.sum(k * k, axis=-1, keepdims=True) * (1.0 / MEM_HEAD_DIM)
        kv_out[0, :, h * LANES:(h + 1) * LANES] = k * lax.rsqrt(kms + EPS) * kg_ref[...]
    kv_out[0, :, BRANCH_W:2 * BRANCH_W] = kv[:, BRANCH_W:2 * BRANCH_W]


def _memkv_call(mem, pw):
    b, m, _ = mem.shape
    return pl.pallas_call(
        _memkv_kernel,
        grid=(b,),
        in_specs=[pl.BlockSpec((1, m, D_MODEL), lambda i: (i, 0, 0)), _const_spec((1, D_MODEL)),
                  _const_spec((D_MODEL, 2 * BRANCH_W)), _const_spec((1, LANES))],
        out_specs=pl.BlockSpec((1, m, 2 * BRANCH_W), lambda i: (i, 0, 0)),
        out_shape=jax.ShapeDtypeStruct((b, m, 2 * BRANCH_W), F32),
        compiler_params=pltpu.CompilerParams(dimension_semantics=("arbitrary",),
                                             vmem_limit_bytes=VMEM_LIMIT),
        name="mem_kv",
    )(mem, pw['nmem'], pw['w_mem'], pw['mk_gain'])


def _mem_attend(mq, kv_ref):
    outs = []
    for h in range(MEM_HEADS):
        k = kv_ref[:, h * LANES:(h + 1) * LANES].astype(BF16)
        v = kv_ref[:, BRANCH_W + h * LANES:BRANCH_W + (h + 1) * LANES].astype(BF16)
        s = _dot_nt(mq[:, h * LANES:(h + 1) * LANES], k) * MEM_HEAD_DIM ** -0.5
        e = jnp.exp(s - jnp.max(s, axis=1, keepdims=True))
        pr = e / jnp.sum(e, axis=1, keepdims=True)
        outs.append(_dot(pr.astype(BF16), v))
    return jnp.concatenate(outs, axis=1)


def _mem_attn_kernel(mq_ref, kv_ref, o_ref):
    o_ref[0] = _mem_attend(mq_ref[0], kv_ref.at[0]).astype(BF16)


def _mem_attn_call(mq3, mkv):
    b, t, _ = mq3.shape
    tq = min(TM, t)
    return pl.pallas_call(
        _mem_attn_kernel,
        grid=(b, t // tq),
        in_specs=[pl.BlockSpec((1, tq, 512), lambda bi, i: (bi, i, 0)),
                  pl.BlockSpec((1, MEM_LEN, 2 * BRANCH_W), lambda bi, i: (bi, 0, 0))],
        out_specs=pl.BlockSpec((1, tq, 512), lambda bi, i: (bi, i, 0)),
        out_shape=jax.ShapeDtypeStruct((b, t, 512), BF16),
        compiler_params=pltpu.CompilerParams(dimension_semantics=("arbitrary", "arbitrary"),
                                             vmem_limit_bytes=VMEM_LIMIT),
        name="mem_attn",
    )(mq3, mkv)


def _finish_kernel(x_ref, on_ref, z_ref, om_ref, nmix_ref, wmg_ref, wbr_ref, wout_ref, h_out):
    x = x_ref[...]
    ms = jnp.sum(x * x, axis=-1, keepdims=True) * (1.0 / D_MODEL)
    xn = (x * lax.rsqrt(ms + EPS) * nmix_ref[...]).astype(BF16)
    merged = None
    for n, br_ref in enumerate((on_ref, z_ref, om_ref)):
        gate = jax.nn.sigmoid(_dot(xn, wmg_ref[:, n * D_MODEL:(n + 1) * D_MODEL]))
        term = gate * _dot(br_ref[...], wbr_ref[n])
        merged = term if merged is None else merged + term
    h_out[...] = x + _dot(merged.astype(BF16), wout_ref[...])


def _finish_call(x, o_nsa, z, o_mem, pw):
    n = x.shape[0]
    tm = min(TM, n)
    tok = lambda w: pl.BlockSpec((tm, w), lambda i: (i, 0))
    return pl.pallas_call(
        _finish_kernel,
        grid=(n // tm,),
        in_specs=[tok(D_MODEL), tok(512), tok(512), tok(512), _const_spec((1, D_MODEL)),
                  _const_spec((D_MODEL, N_BRANCH * D_MODEL)),
                  _const_spec((N_BRANCH, BRANCH_W, D_MODEL)), _const_spec((D_MODEL, D_MODEL))],
        out_specs=tok(D_MODEL),
        out_shape=jax.ShapeDtypeStruct((n, D_MODEL), F32),
        compiler_params=pltpu.CompilerParams(dimension_semantics=("arbitrary",),
                                             vmem_limit_bytes=VMEM_LIMIT),
        name="finish",
    )(x, o_nsa, z, o_mem, pw['nmix'], pw['w_mg'], pw['w_br'], pw['w_out'])


def _ffn_kernel(h_ref, nf_ref, wgu_ref, wd_ref, y_out):
    h = h_ref[...]
    ms = jnp.sum(h * h, axis=-1, keepdims=True) * (1.0 / D_MODEL)
    hn = (h * lax.rsqrt(ms + EPS) * nf_ref[...]).astype(BF16)
    acc = h
    for c in range(D_FF // FF_CHUNK):
        g = _dot(hn, wgu_ref[:, c * FF_CHUNK:(c + 1) * FF_CHUNK])
        u = _dot(hn, wgu_ref[:, D_FF + c * FF_CHUNK:D_FF + (c + 1) * FF_CHUNK])
        a = (jax.nn.silu(g) * u).astype(BF16)
        acc = acc + _dot(a, wd_ref[c * FF_CHUNK:(c + 1) * FF_CHUNK, :])
    y_out[...] = acc


def _ffn_call(h, pw):
    n = h.shape[0]
    tm = min(TM, n)
    tok = pl.BlockSpec((tm, D_MODEL), lambda i: (i, 0))
    return pl.pallas_call(
        _ffn_kernel,
        grid=(n // tm,),
        in_specs=[tok, _const_spec((1, D_MODEL)), _const_spec((D_MODEL, 2 * D_FF)),
                  _const_spec((D_FF, D_MODEL))],
        out_specs=tok,
        out_shape=jax.ShapeDtypeStruct((n, D_MODEL), F32),
        compiler_params=pltpu.CompilerParams(dimension_semantics=("arbitrary",),
                                             vmem_limit_bytes=VMEM_LIMIT),
        name="ffn",
    )(h, pw['nffn'], pw['w_gu'], pw['w_down'])


def _nsa_sample_kernel(layer_base, pt_ref, pool_ref, q_ref, qr_ref, rows_ref, wnew_ref, wbuf_ref,
                       mq_ref, mkv_ref, ng_ref, w1_ref, pe_ref, w2_ref, kg_ref, e64_ref, ov_ref,
                       cm_ref, gx_ref,
                       o_ref, wout_ref, om_ref,
                       kraw_ref, vraw_ref, slc_ref, cmp_sem, slc_sem):
    b = pl.program_id(0)
    nb = pl.num_programs(0)
    n_pages = pt_ref.shape[1]
    n_past = n_pages * PAGE_SIZE
    n_sub = n_past // CMP_STRIDE

    subs_per_page = PAGE_SIZE // CMP_STRIDE

    def cmp_copies(bb, slot, j):
        page = layer_base + pt_ref[bb, j]
        dst = pl.ds(pl.multiple_of(j * subs_per_page, subs_per_page), subs_per_page)
        return (pltpu.make_async_copy(pool_ref.at[page, :, :, 0:LANES],
                                      kraw_ref.at[slot, :, dst, :], cmp_sem.at[slot]),
                pltpu.make_async_copy(pool_ref.at[page, :, :, LANES:2 * LANES],
                                      vraw_ref.at[slot, :, dst, :], cmp_sem.at[slot]))

    def slc_copy(bb, j):
        page = layer_base + pt_ref[bb, j]
        dst = pl.ds(pl.multiple_of(j * subs_per_page, subs_per_page), subs_per_page)
        return pltpu.make_async_copy(pool_ref.at[page, :, :, 2 * LANES:4 * LANES],
                                     slc_ref.at[:, dst, :], slc_sem.at[0])

    def start_cmp(bb, slot):
        def body(j, c):
            for cp in cmp_copies(bb, slot, j):
                cp.start()
            return c
        lax.fori_loop(0, n_pages, body, 0)

    def wait_cmp(bb, slot):
        def body(j, c):
            for cp in cmp_copies(bb, slot, j):
                cp.wait()
            return c
        lax.fori_loop(0, n_pages, body, 0)

    slot = b % 2

    @pl.when(b == 0)
    def _():
        start_cmp(b, slot)

    def slc_start_body(j, c):
        slc_copy(b, j).start()
        return c
    lax.fori_loop(0, n_pages, slc_start_body, 0)

    @pl.when(b + 1 < nb)
    def _():
        start_cmp(b + 1, 1 - slot)

    wait_cmp(b, slot)

    row8 = lax.broadcasted_iota(jnp.int32, (8, LANES), 0)
    lane8 = lax.broadcasted_iota(jnp.int32, (8, LANES), 1)
    low = lane8 < HEAD_DIM
    own_half = (row8 < GROUP) == low

    def stack8(qrow):
        out = jnp.zeros((8, LANES), F32)
        for j in range(GROUP):
            pj = jnp.broadcast_to(qrow[:, j * LANES:(j + 1) * LANES], (8, LANES))
            out = jnp.where((row8 & (GROUP - 1)) == j, pj, out)
        return jnp.where(own_half, out, 0.0).astype(BF16)

    q8 = stack8(q_ref[0])
    qr8 = stack8(qr_ref[0])
    e64k = e64_ref[...]

    raw_refs = (kraw_ref, vraw_ref)
    kc, vc = _compress(lambda kv, s: raw_refs[kv][slot, s], n_sub, w1_ref, pe_ref, w2_ref,
                       kg_ref, e64k)
    s = _dot_nt(q8, kc.astype(BF16))
    col = lax.broadcasted_iota(jnp.int32, s.shape, 1)
    rowc = lax.broadcasted_iota(jnp.int32, s.shape, 0)
    vis = col < n_sub - CMP_RATIO + 1
    sm = jnp.where(vis, s, NEG)
    e = jnp.where(vis, jnp.exp(sm - jnp.max(sm, axis=1, keepdims=True)), 0.0)
    pr = e / jnp.sum(e, axis=1, keepdims=True)
    o_c = _dot(pr.astype(BF16), vc.astype(BF16))
    imp8 = jnp.zeros(s.shape, F32)
    for g in range(N_KV):
        imp_g = jnp.sum(jnp.where(rowc // GROUP == g, pr, 0.0), axis=0, keepdims=True)
        imp8 = jnp.where(rowc == g, jnp.broadcast_to(imp_g, s.shape), imp8)

    n_blk_pad = ov_ref.shape[1]
    q_blk = n_past // SLC_BLOCK
    sc = _dot_exact01(imp8, ov_ref[...])
    bl = lax.broadcasted_iota(jnp.int32, sc.shape, 1)
    forced = (bl == 0) | (bl == q_blk) | (bl == q_blk - 1)
    sc = jnp.where(forced, jnp.inf, jnp.where(bl <= q_blk, sc, -jnp.inf))
    sc_t = jnp.transpose(sc)
    drops = []
    for g in range(N_KV):
        colv = jnp.broadcast_to(sc_t[:, g:g + 1], (n_blk_pad, n_blk_pad))
        rowv = jnp.broadcast_to(sc[g:g + 1, :], (n_blk_pad, n_blk_pad))
        ii = lax.broadcasted_iota(jnp.int32, colv.shape, 0)
        jj = lax.broadcasted_iota(jnp.int32, colv.shape, 1)
        tie = jnp.where(ii < jj, 1.0, 0.0)
        beats = jnp.where(colv > rowv, 1.0, jnp.where(colv == rowv, tie, 0.0))
        cnt = jnp.sum(beats, axis=0, keepdims=True)
        drops.append(jnp.where((cnt < N_SELECT) & (bl[0:1] <= q_blk), 0.0, 1.0))
    drop8 = jnp.where(row8 < GROUP, jnp.broadcast_to(drops[0][:, 0:LANES], (8, LANES)),
                      jnp.broadcast_to(drops[1][:, 0:LANES], (8, LANES))).astype(BF16)

    def slc_wait_body(j, c):
        slc_copy(b, j).wait()
        return c
    lax.fori_loop(0, n_pages, slc_wait_body, 0)

    qa = jnp.concatenate([qr8, drop8], axis=1)
    k_past = slc_ref[:, :, 0:LANES].astype(BF16).reshape(n_past, LANES)
    v_past = slc_ref[:, :, LANES:2 * LANES].astype(BF16).reshape(n_past, LANES)
    ka = jnp.concatenate([k_past, cm_ref[...]], axis=1)
    s_past = _dot_nt(qa, ka)
    rows = rows_ref[0]
    k_new = rows[:, 2 * LANES:3 * LANES].astype(BF16).astype(F32)
    v_new = rows[:, 3 * LANES:4 * LANES].astype(BF16).astype(F32)
    s_new = jnp.sum(qr8.astype(F32) * k_new, axis=1, keepdims=True)
    m = jnp.maximum(jnp.max(s_past, axis=1, keepdims=True), s_new)
    e_past = jnp.exp(s_past - m)
    e_new = jnp.exp(s_new - m)
    den = jnp.sum(e_past, axis=1, keepdims=True) + e_new
    pr_past = (e_past / den).astype(BF16)
    pr_new = (e_new / den).astype(BF16).astype(F32)
    o_s = _dot(pr_past, v_past) + pr_new * v_new

    wb = wbuf_ref[0]
    wrow = lax.broadcasted_iota(jnp.int32, wb.shape, 0)
    w_upd = jnp.where(wrow == wb.shape[0] - 1, wnew_ref[0], pltpu.roll(wb, wb.shape[0] - 1, 0))
    wout_ref[0] = w_upd
    s_w = _dot_nt(qr8, w_upd[:, 0:LANES].astype(BF16))
    e_w = jnp.exp(s_w - jnp.max(s_w, axis=1, keepdims=True))
    pr_w = e_w / jnp.sum(e_w, axis=1, keepdims=True)
    o_w = _dot(pr_w.astype(BF16), w_upd[:, LANES:2 * LANES].astype(BF16))

    gh, gl = _split_bf16(jnp.broadcast_to(ng_ref[0], (8, LANES)))
    tot = jnp.zeros((8, LANES), F32)
    for br, o8 in enumerate((o_c, o_s, o_w)):
        pair = jnp.where(low, o8, pltpu.roll(o8, GROUP, 0))
        gate8 = jnp.zeros((8, LANES), F32)
        for j in range(GROUP):
            gx = gx_ref[:, (br * GROUP + j) * LANES:(br * GROUP + j + 1) * LANES]
            gate8 = jnp.where(row8 == j, _dot(gh, gx) + _dot(gl, gx), gate8)
        tot = tot + gate8 * pair
    o_ref[0] = tot

    mq8 = jnp.broadcast_to(mq_ref[0], (8, BRANCH_W)).astype(BF16)
    om_ref[0] = _mem_attend(mq8, mkv_ref.at[0])


def _nsa_sample_call(layer, page_table, pool3, q4, qr4, rows_s, win_new, win_buf, mq4, mkv,
                     ng_s, pw):
    nb, n_pages = page_table.shape
    n_past = n_pages * PAGE_SIZE
    n_phys = pool3.shape[0] // DEPTH
    w_buf = win_buf.shape[1]
    n_blk_pad = pw['ov_s'].shape[1]
    per_b = lambda *shape: pl.BlockSpec((1,) + shape, lambda b, pt: (b,) + (0,) * len(shape))
    grid_spec = pltpu.PrefetchScalarGridSpec(
        num_scalar_prefetch=1,
        grid=(nb,),
        in_specs=[pl.BlockSpec(memory_space=pl.ANY),
                  per_b(1, 512), per_b(1, 512), per_b(1, 512), per_b(1, 256),
                  per_b(w_buf, 256), per_b(1, 512), per_b(MEM_LEN, 2 * BRANCH_W),
                  per_b(1, LANES),
                  _const_spec((CMP_RATIO * CMP_STRIDE, 2 * LANES, 2 * CMP_HIDDEN)),
                  _const_spec((2 * CMP_RATIO * CMP_STRIDE, LANES)),
                  _const_spec((2, 2 * CMP_HIDDEN, LANES)),
                  _const_spec((1, LANES)), _const_spec((LANES, LANES)),
                  _const_spec((n_past // CMP_STRIDE, n_blk_pad)), _const_spec((n_past, LANES)),
                  _const_spec((LANES, 3 * GROUP * LANES))],
        out_specs=[per_b(8, LANES), per_b(w_buf, 256), per_b(8, BRANCH_W)],
        scratch_shapes=[pltpu.VMEM((2, CMP_STRIDE, n_past // CMP_STRIDE, LANES), F32),
                        pltpu.VMEM((2, CMP_STRIDE, n_past // CMP_STRIDE, LANES), F32),
                        pltpu.VMEM((CMP_STRIDE, n_past // CMP_STRIDE, 2 * LANES), F32),
                        pltpu.SemaphoreType.DMA((2,)), pltpu.SemaphoreType.DMA((1,))],
    )
    return pl.pallas_call(
        functools.partial(_nsa_sample_kernel, layer * n_phys),
        grid_spec=grid_spec,
        out_shape=[jax.ShapeDtypeStruct((nb, 8, LANES), F32),
                   jax.ShapeDtypeStruct((nb, w_buf, 256), F32),
                   jax.ShapeDtypeStruct((nb, 8, BRANCH_W), F32)],
        compiler_params=pltpu.CompilerParams(dimension_semantics=("arbitrary",),
                                             vmem_limit_bytes=VMEM_LIMIT),
        name="nsa_sample",
    )(page_table, pool3, q4, qr4, rows_s, win_new, win_buf, mq4, mkv, ng_s,
      pw['cmp_w1'], pw['cmp_pe'], pw['cmp_w2'], pw['kc_gain'], pw['e64k'], pw['ov_s'],
      pw['cm_s'], pw['gate_x'])


def _pair_perm():
    p = np.arange(BRANCH_W)
    j, ln = p // LANES, p % LANES
    head = np.where(ln < HEAD_DIM, j, GROUP + j)
    return head * HEAD_DIM + ln % HEAD_DIM


def _block_diag_ones(width, group):
    i = np.arange(width)
    return (i[:, None] // group == i[None, :] // group).astype(np.float32)


def _overlap(n_cmp_rows, n_cmp_valid, n_blk_cols):
    i = np.arange(n_cmp_rows)[:, None]
    j = np.arange(n_blk_cols)[None, :]
    ov = (i * CMP_STRIDE < (j + 1) * SLC_BLOCK) & (i * CMP_STRIDE + CMP_LEN > j * SLC_BLOCK)
    return (ov & (i < n_cmp_valid)).astype(np.float32)


def _block_mask_rows(n_keys, s_major=False):
    k = np.arange(n_keys)
    if s_major:
        n_sub = n_keys // CMP_STRIDE
        k = (k % n_sub) * CMP_STRIDE + k // n_sub
    k = k[:, None]
    j = np.arange(LANES)[None, :]
    return np.where(k // SLC_BLOCK == j, NEG, 0.0).astype(np.float32)


def _band_bias(tq):
    r = np.arange(tq)[:, None]
    c = np.arange(tq)[None, :]
    return np.stack([np.where(c <= r, 0.0, NEG), np.where(c > r, 0.0, NEG)]).astype(np.float32)


def _gate_expand():
    x = np.zeros((LANES, 3 * GROUP * LANES), np.float32)
    for br in range(3):
        for j in range(GROUP):
            for ln in range(LANES):
                head = j if ln < HEAD_DIM else GROUP + j
                x[br * N_HEADS + head, (br * GROUP + j) * LANES + ln] = 1.0
    return x


def _rope_tables(pos):
    half = ROT_DIM // 2
    inv = ROPE_THETA ** (-jnp.arange(half, dtype=F32) / half)
    ang = pos.astype(F32)[:, None] * inv[None, :]
    cos, sin = jnp.cos(ang), jnp.sin(ang)
    n = pos.shape[0]
    one = jnp.ones((n, HEAD_DIM - ROT_DIM), F32)
    zero = jnp.zeros((n, HEAD_DIM - ROT_DIM), F32)
    zh = jnp.zeros((n, half), F32)
    c = jnp.concatenate([cos, cos, one], axis=1)
    s1 = jnp.concatenate([zh, sin, zero], axis=1)
    s2 = jnp.concatenate([-sin, zh, zero], axis=1)
    tab = jnp.stack([c, s1, s2])
    return jnp.concatenate([tab, tab], axis=2)


def _layer_params(l, seq_len, n_past, norm_mix, w_in, q_norm, k_norm, cmp_pe, cmp_w1, cmp_w2,
                  conv_w, norm_mem, w_mem_kv, mem_q_norm, mem_k_norm, w_branch, w_out, norm_ffn,
                  w_gate_up, w_down):
    perm = _pair_perm()
    wi = w_in[l]
    o_kv, o_ng = BRANCH_W, BRANCH_W + 768
    o_cx = o_ng + 3 * N_HEADS
    o_cb, o_cc, o_mq, o_mg = o_cx + 512, o_cx + 1024, o_cx + 1536, o_cx + 2048
    w_a = jnp.concatenate([
        wi[:, perm], wi[:, o_kv:o_ng], wi[:, o_cx:o_cb], wi[:, o_cb:o_cc], wi[:, o_cc:o_mq],
        wi[:, o_mq:o_mg], wi[:, o_ng:o_cx], jnp.zeros((D_MODEL, LANES - 3 * N_HEADS), F32)],
        axis=1).astype(BF16)
    w1 = cmp_w1[l].reshape(2, CMP_RATIO, CMP_STRIDE, HEAD_DIM, CMP_HIDDEN)
    zw = jnp.zeros_like(w1)
    w1bd = jnp.concatenate([jnp.concatenate([w1, zw], axis=-1),
                            jnp.concatenate([zw, w1], axis=-1)], axis=-2)
    w2 = cmp_w2[l]
    z2 = jnp.zeros_like(w2)
    w2bd = jnp.concatenate([jnp.concatenate([w2, z2], axis=-1),
                            jnp.concatenate([z2, w2], axis=-1)], axis=-2)
    pe = cmp_pe[l].reshape(2 * CMP_RATIO * CMP_STRIDE, HEAD_DIM)
    n_sub_p = seq_len // CMP_STRIDE
    n_sub_s = n_past // CMP_STRIDE
    n_blk_s = -(-(n_past // SLC_BLOCK + 1) // LANES) * LANES
    wbr = w_branch[l]
    return dict(
        nmix=norm_mix[l][None, :], w_a=w_a, w_mg=wi[:, o_mg:].astype(BF16),
        q_gain=jnp.tile(q_norm[l], N_HEADS)[None, :],
        k_gain=jnp.stack([jnp.tile(k_norm[l, 1], N_KV), jnp.tile(k_norm[l, 2], N_KV)]),
        kc_gain=jnp.tile(k_norm[l, 0], N_KV)[None, :],
        mq_gain=jnp.tile(mem_q_norm[l], MEM_HEADS)[None, :],
        mk_gain=mem_k_norm[l][None, :],
        conv_w=conv_w[l],
        cmp_w1=w1bd.reshape(CMP_RATIO * CMP_STRIDE, 2 * LANES, 2 * CMP_HIDDEN).astype(BF16),
        cmp_pe=jnp.concatenate([pe, pe], axis=1),
        cmp_w2=w2bd.astype(BF16),
        nmem=norm_mem[l][None, :], w_mem=w_mem_kv[l].astype(BF16),
        w_br=jnp.concatenate([wbr[0][perm][None], wbr[1:]], axis=0).astype(BF16),
        w_out=w_out[l].astype(BF16),
        nffn=norm_ffn[l][None, :], w_gu=w_gate_up[l].astype(BF16), w_down=w_down[l].astype(BF16),
        e64=jnp.asarray(_block_diag_ones(512, HEAD_DIM), BF16),
        e64k=jnp.asarray(_block_diag_ones(LANES, HEAD_DIM), BF16),
        e128=jnp.asarray(_block_diag_ones(512, MEM_HEAD_DIM), BF16),
        ov_p=jnp.asarray(_overlap(n_sub_p, n_sub_p - CMP_RATIO + 1, LANES).T, BF16),
        cm_p=jnp.asarray(_block_mask_rows(seq_len), BF16),
        ov_s=jnp.asarray(_overlap(n_sub_s, n_sub_s - CMP_RATIO + 1, n_blk_s), BF16),
        cm_s=jnp.asarray(_block_mask_rows(n_past, s_major=True), BF16),
        gate_x=jnp.asarray(_gate_expand(), BF16),
    )


def kernel(x_prompt, x_sample, cache_nsa_kv, cache_win_kv, state_conv, cache_mem_kv, page_table,
           mem_prompt, norm_mix, w_in, q_norm, k_norm, cmp_pe, cmp_w1, cmp_w2, conv_w, norm_mem,
           w_mem_kv, mem_q_norm, mem_k_norm, w_branch, w_out, norm_ffn, w_gate_up, w_down):
    bp, t, _ = x_prompt.shape
    bs, ts, _ = x_sample.shape
    assert ts == 1 and t % TM == 0 and t % TQ == 0
    n_pages = page_table.shape[1]
    n_past = n_pages * PAGE_SIZE
    w_buf = cache_win_kv.shape[2]
    assert w_buf == WINDOW and n_past >= WINDOW and (n_past // SLC_BLOCK) == LANES

    xp = x_prompt.reshape(bp * t, D_MODEL)
    xs = x_sample.reshape(bs, D_MODEL)
    rope_p = _rope_tables(jnp.arange(t))
    rope_s = _rope_tables(jnp.full((bs,), n_past, jnp.int32))
    pool3 = cache_nsa_kv.reshape(DEPTH * cache_nsa_kv.shape[1], PAGE_SIZE // CMP_STRIDE,
                                 CMP_STRIDE, 4 * N_KV * HEAD_DIM).transpose(0, 2, 1, 3)

    rows_p, rows_s, win_p, win_s, conv_p, conv_s, mem_p = [], [], [], [], [], [], []
    for l in range(DEPTH):
        pw = _layer_params(l, t, n_past, norm_mix, w_in, q_norm, k_norm, cmp_pe, cmp_w1, cmp_w2,
                           conv_w, norm_mem, w_mem_kv, mem_q_norm, mem_k_norm, w_branch, w_out,
                           norm_ffn, w_gate_up, w_down)
        q, qr, rows, win, kvb, z, utail, mq, ng = _prep_call(xp, pw, rope_p, False, t)
        rows3 = rows.reshape(bp, t, 512)
        kc, vc = _compress_prompt_call(rows3, pw)
        o_nsa = _nsa_prompt_call(q.reshape(bp, t, 512), qr.reshape(bp, t, 512),
                                 kvb.reshape(bp, t, 512), kc, vc, ng.reshape(bp, t, LANES), pw)
        mkv = _memkv_call(mem_prompt, pw)
        o_mem = _mem_attn_call(mq.reshape(bp, t, 512), mkv)
        h = _finish_call(xp, o_nsa.reshape(bp * t, 512), z, o_mem.reshape(bp * t, 512), pw)
        xp = _ffn_call(h, pw)
        rows_p.append(rows3.reshape(bp, t, 4, N_KV, HEAD_DIM))
        win_p.append(win.reshape(bp, t, 2, N_KV, HEAD_DIM)[:, t - min(WINDOW, t):])
        conv_p.append(utail.reshape(bp, t // TM, 8, 512)[:, -1, 8 - (CONV_W - 1):])
        mem_p.append(mkv.reshape(bp, MEM_LEN, 2, MEM_HEADS, MEM_HEAD_DIM))
        q, qr, rows, win, _, z, u, mq, ng = _prep_call(xs, pw, rope_s, True, 1,
                                                       state=state_conv[l])
        o_nsa, new_win, o_mem = _nsa_sample_call(
            l, page_table, pool3, q.astype(F32).reshape(bs, 1, 512),
            qr.astype(F32).reshape(bs, 1, 512), rows.reshape(bs, 1, 512),
            win.reshape(bs, 1, 256), cache_win_kv[l].reshape(bs, w_buf, 256),
            mq.astype(F32).reshape(bs, 1, 512),
            cache_mem_kv[l].reshape(bs, MEM_LEN, 2 * BRANCH_W), ng.reshape(bs, 1, LANES), pw)
        o_nsa = o_nsa[:, 0:GROUP, :].reshape(bs, 512).astype(BF16)
        o_mem = o_mem[:, 0, :].astype(BF16)
        h = _finish_call(xs, o_nsa, z, o_mem, pw)
        xs = _ffn_call(h, pw)
        rows_s.append(rows.reshape(bs, 1, 4, N_KV, HEAD_DIM))
        win_s.append(new_win.reshape(bs, w_buf, 2, N_KV, HEAD_DIM))
        conv_s.append(jnp.stack([state_conv[l][:, 1, :], u], axis=1))

    return (xp.reshape(bp, t, D_MODEL), xs.reshape(bs, 1, D_MODEL), jnp.stack(rows_p),
            jnp.stack(rows_s), jnp.stack(win_p), jnp.stack(win_s), jnp.stack(conv_p),
            jnp.stack(conv_s), jnp.stack(mem_p))
```

```python
import functools

import numpy as np
import jax
import jax.numpy as jnp
from jax import lax
from jax.experimental import pallas as pl
from jax.experimental.pallas import tpu as pltpu

F32 = jnp.float32
BF16 = jnp.bfloat16

D_MODEL = 1024
DEPTH = 2
PAST_LEN = 8192
PAGE_SIZE = 128
HEAD_DIM = 64
BRANCH_W = D_MODEL // 2
N_HEADS = BRANCH_W // HEAD_DIM
N_KV = N_HEADS // 4
GROUP = N_HEADS // N_KV
ROT_DIM = HEAD_DIM // 4
ROPE_THETA = 500000.0
CMP_LEN = 32
CMP_STRIDE = 16
CMP_RATIO = CMP_LEN // CMP_STRIDE
CMP_HIDDEN = 2 * HEAD_DIM
SLC_BLOCK = 64
N_SELECT = 16
WINDOW = 512
CONV_W = 3
MEM_LEN = 256
MEM_HEADS = 4
MEM_HEAD_DIM = BRANCH_W // MEM_HEADS
N_BRANCH = 3
D_FF = ((8 * D_MODEL + 767) // 768) * 256
EPS = 1e-6

LANES = 128
VMEM_LIMIT = 56 * 1024 * 1024

TM = 512
TQ = 512
FF_CHUNK = 256
NEG = -1e30
M_INIT = -3e38

_C_Q = 0
_C_KV = 512
_C_CX = 1280
_C_CB = 1792
_C_CC = 2304
_C_MQ = 2816
_C_NG = 3328
_N_A = 3456


def _const_spec(shape):
    nd = len(shape)
    return pl.BlockSpec(shape, lambda *_: (0,) * nd, pipeline_mode=pl.Buffered(1))


def _dot(a, b):
    return jnp.dot(a, b, preferred_element_type=F32)


def _dot_nt(a, b):
    return lax.dot_general(a, b, (((1,), (1,)), ((), ())), preferred_element_type=F32)


def _split_bf16(v):
    hi = v.astype(BF16)
    lo = (v - hi.astype(F32)).astype(BF16)
    return hi, lo


def _dot_exact01(v, m01):
    hi, lo = _split_bf16(v)
    return _dot(hi, m01) + _dot(lo, m01)


def _group_rms(v, e01, group, gain):
    ms = _dot_exact01(v * v, e01) * (1.0 / group)
    return v * lax.rsqrt(ms + EPS) * gain


def _rope_cols(v, c, s1, s2):
    outs = []
    for j in range(v.shape[1] // LANES):
        col = v[:, j * LANES:(j + 1) * LANES]
        outs.append(col * c + pltpu.roll(col, ROT_DIM // 2, 1) * s1
                    + pltpu.roll(col, LANES - ROT_DIM // 2, 1) * s2)
    return outs[0] if len(outs) == 1 else jnp.concatenate(outs, axis=1)


def _prep_kernel(sample, tiles_per_seq, *refs):
    if sample:
        (x_ref, nmix_ref, w_ref, qg_ref, kg_ref, mqg_ref, rope_ref, e64_ref, e128_ref, cw_ref,
         s0_ref, s1_ref,
         q_out, qr_out, rows_out, win_out, kvb_out, z_out, u_out, mq_out, ng_out) = refs
    else:
        (x_ref, nmix_ref, w_ref, qg_ref, kg_ref, mqg_ref, rope_ref, e64_ref, e128_ref, cw_ref,
         q_out, qr_out, rows_out, win_out, kvb_out, z_out, u_out, mq_out, ng_out,
         carry_ref) = refs
    tm = x_ref.shape[0]
    x = x_ref[...]
    ms = jnp.sum(x * x, axis=-1, keepdims=True) * (1.0 / D_MODEL)
    xn = (x * lax.rsqrt(ms + EPS) * nmix_ref[...]).astype(BF16)

    rc, rs1, rs2 = rope_ref[0], rope_ref[1], rope_ref[2]
    e64 = e64_ref[...]

    q = _dot(xn, w_ref[:, _C_Q:_C_Q + 512])
    qn = _group_rms(q, e64, HEAD_DIM, qg_ref[...])
    q_out[...] = (qn * HEAD_DIM ** -0.5).astype(BF16)
    qr_out[...] = (_rope_cols(qn, rc, rs1, rs2) * HEAD_DIM ** -0.5).astype(BF16)

    kv = _dot(xn, w_ref[:, _C_KV:_C_KV + 768])
    e64k = e64_ref[0:LANES, 0:LANES]
    k_slc = _rope_cols(_group_rms(kv[:, 256:384], e64k, HEAD_DIM, kg_ref[0:1, :]), rc, rs1, rs2)
    k_win = _rope_cols(_group_rms(kv[:, 512:640], e64k, HEAD_DIM, kg_ref[1:2, :]), rc, rs1, rs2)
    v_slc = kv[:, 384:512]
    v_win = kv[:, 640:768]
    rows_out[:, 0:256] = kv[:, 0:256]
    rows_out[:, 256:384] = k_slc
    rows_out[:, 384:512] = v_slc
    win_out[:, 0:128] = k_win
    win_out[:, 128:256] = v_win
    kvb_out[:, 0:128] = k_slc.astype(BF16)
    kvb_out[:, 128:256] = v_slc.astype(BF16)
    kvb_out[:, 256:384] = k_win.astype(BF16)
    kvb_out[:, 384:512] = v_win.astype(BF16)

    cx = _dot(xn, w_ref[:, _C_CX:_C_CX + 512])
    cc = _dot(xn, w_ref[:, _C_CC:_C_CC + 512])
    cb = _dot(xn, w_ref[:, _C_CB:_C_CB + 512])
    u = cc * cx
    w0, w1, w2 = cw_ref[0:1, :], cw_ref[1:2, :], cw_ref[2:3, :]
    if sample:
        z = cb * (s0_ref[...] * w0 + s1_ref[...] * w1 + u * w2)
        u_out[...] = u
    else:
        @pl.when(pl.program_id(0) % tiles_per_seq == 0)
        def _():
            carry_ref[...] = jnp.zeros_like(carry_ref)
        prev = carry_ref[...]
        row = lax.broadcasted_iota(jnp.int32, u.shape, 0)
        u1 = jnp.where(row == 0, prev[7:8, :], pltpu.roll(u, 1, 0))
        u2 = jnp.where(row == 0, prev[6:7, :],
                       jnp.where(row == 1, prev[7:8, :], pltpu.roll(u, 2, 0)))
        z = cb * (u2 * w0 + u1 * w1 + u * w2)
        tail = u[tm - 8:tm, :]
        carry_ref[...] = tail
        u_out[...] = tail
    z_out[...] = z.astype(BF16)

    mq = _dot(xn, w_ref[:, _C_MQ:_C_MQ + 512])
    mq_out[...] = _group_rms(mq, e128_ref[...], MEM_HEAD_DIM, mqg_ref[...]).astype(BF16)

    ng_out[...] = jax.nn.sigmoid(_dot(xn, w_ref[:, _C_NG:_C_NG + 128]))


def _prep_call(x, pw, rope_tab, sample, seq_len, state=None):
    n = x.shape[0]
    tm = n if sample else TM
    tps = 1 if sample else seq_len // tm
    grid = (n // tm,)
    tok = lambda w: pl.BlockSpec((tm, w), lambda i: (i, 0))
    in_specs = [tok(D_MODEL), _const_spec((1, D_MODEL)), _const_spec((D_MODEL, _N_A)),
                _const_spec((1, 512)), _const_spec((2, LANES)), _const_spec((1, 512)),
                pl.BlockSpec((3, tm, LANES), lambda i: (0, i % tps, 0)),
                _const_spec((512, 512)), _const_spec((512, 512)), _const_spec((CONV_W, 512))]
    args = [x, pw['nmix'], pw['w_a'], pw['q_gain'], pw['k_gain'], pw['mq_gain'], rope_tab,
            pw['e64'], pw['e128'], pw['conv_w']]
    scratch = []
    if sample:
        in_specs += [tok(512), tok(512)]
        args += [state[:, 0, :], state[:, 1, :]]
        u_shape, u_spec = (n, 512), tok(512)
    else:
        scratch = [pltpu.VMEM((8, 512), F32)]
        u_shape, u_spec = (n // tm * 8, 512), pl.BlockSpec((8, 512), lambda i: (i, 0))
    out_shape = [jax.ShapeDtypeStruct((n, 512), BF16), jax.ShapeDtypeStruct((n, 512), BF16),
                 jax.ShapeDtypeStruct((n, 512), F32), jax.ShapeDtypeStruct((n, 256), F32),
                 jax.ShapeDtypeStruct((n, 512), BF16), jax.ShapeDtypeStruct((n, 512), BF16),
                 jax.ShapeDtypeStruct(u_shape, F32), jax.ShapeDtypeStruct((n, 512), BF16),
                 jax.ShapeDtypeStruct((n, LANES), F32)]
    out_specs = [tok(512), tok(512), tok(512), tok(256), tok(512), tok(512), u_spec, tok(512),
                 tok(LANES)]
    return pl.pallas_call(
        functools.partial(_prep_kernel, sample, tps),
        grid=grid, in_specs=in_specs, out_specs=out_specs, out_shape=out_shape,
        scratch_shapes=scratch,
        compiler_params=pltpu.CompilerParams(dimension_semantics=("arbitrary",),
                                             vmem_limit_bytes=VMEM_LIMIT),
        name="prep_sample" if sample else "prep_prompt",
    )(*args)


def _compress(load_xs, n_sub, w1_ref, pe_ref, w2_ref, kg_ref, e64k):
    outs = []
    for kv in range(2):
        acc = [None, None]
        for s in range(0, CMP_STRIDE, 2):
            xa, xb = load_xs(kv, s), load_xs(kv, s + 1)
            for r in range(CMP_RATIO):
                idx = (kv * CMP_RATIO + r) * CMP_STRIDE + s
                lhs = jnp.concatenate([(xa + pe_ref[idx:idx + 1, :]).astype(BF16),
                                       (xb + pe_ref[idx + 1:idx + 2, :]).astype(BF16)], axis=1)
                d = _dot(lhs, w1_ref[idx // 2])
                acc[r] = d if acc[r] is None else acc[r] + d
        h = acc[0] + pltpu.roll(acc[1], n_sub - 1, 0)
        o = _dot(jax.nn.silu(h).astype(BF16), w2_ref[kv])
        outs.append(o)
    kc = _group_rms(outs[0], e64k, HEAD_DIM, kg_ref[...])
    return kc, outs[1]


def _compress_prompt_kernel(xk_ref, xv_ref, w1_ref, pe_ref, w2_ref, kg_ref, e64_ref,
                            kc_out, vc_out):
    n_sub = kc_out.shape[1]
    x_refs = (xk_ref, xv_ref)
    kc, vc = _compress(lambda kv, s: x_refs[kv][0, pl.ds(s, n_sub, stride=CMP_STRIDE), :],
                       n_sub, w1_ref, pe_ref, w2_ref, kg_ref, e64_ref[...])
    kc_out[0] = kc.astype(BF16)
    vc_out[0] = vc.astype(BF16)


def _compress_prompt_call(rows3, pw):
    b, t, _ = rows3.shape
    n_sub = t // CMP_STRIDE
    out = jax.ShapeDtypeStruct((b, n_sub, LANES), BF16)
    return pl.pallas_call(
        _compress_prompt_kernel,
        grid=(b,),
        in_specs=[pl.BlockSpec((1, t, LANES), lambda i: (i, 0, 0)),
                  pl.BlockSpec((1, t, LANES), lambda i: (i, 0, 1)),
                  _const_spec((CMP_RATIO * CMP_STRIDE, 2 * LANES, 2 * CMP_HIDDEN)),
                  _const_spec((2 * CMP_RATIO * CMP_STRIDE, LANES)),
                  _const_spec((2, 2 * CMP_HIDDEN, LANES)),
                  _const_spec((1, LANES)), _const_spec((LANES, LANES))],
        out_specs=[pl.BlockSpec((1, n_sub, LANES), lambda i: (i, 0, 0))] * 2,
        out_shape=[out, out],
        compiler_params=pltpu.CompilerParams(dimension_semantics=("arbitrary",),
                                             vmem_limit_bytes=VMEM_LIMIT),
        name="compress_prompt",
    )(rows3, rows3, pw['cmp_w1'], pw['cmp_pe'], pw['cmp_w2'], pw['kc_gain'], pw['e64k'])


def _flash_init(m_ref, l_ref, acc_ref):
    m_ref[...] = jnp.full(m_ref.shape, M_INIT, F32)
    l_ref[...] = jnp.zeros(l_ref.shape, F32)
    acc_ref[...] = jnp.zeros(acc_ref.shape, F32)


def _flash_step(s, v, m_ref, l_ref, acc_ref):
    m_prev = m_ref[...]
    m_next = jnp.maximum(m_prev, jnp.max(s, axis=1, keepdims=True))
    p = jnp.exp(s - jnp.concatenate([m_next] * (s.shape[1] // LANES), axis=1))
    alpha = jnp.exp(m_prev - m_next)
    l_ref[...] = alpha * l_ref[...] + jnp.sum(p, axis=1, keepdims=True)
    acc_ref[...] = alpha * acc_ref[...] + _dot(p.astype(BF16), v)
    m_ref[...] = m_next


def _take_top(score_t, n_take):
    nb = score_t.shape[0]
    jidx = lax.broadcasted_iota(jnp.int32, score_t.shape, 0).astype(F32)
    taken = jnp.zeros(score_t.shape, F32)
    s = score_t
    for _ in range(n_take):
        mx = jnp.max(s, axis=0, keepdims=True)
        first = jnp.min(jnp.where(s == mx, jidx, float(nb)), axis=0, keepdims=True)
        pick = jidx == first
        taken = jnp.where(pick, 1.0, taken)
        s = jnp.where(pick, -jnp.inf, s)
    return taken


def _nsa_prompt_kernel(q_ref, qr_ref, kvb_ref, kc_ref, vc_ref, ng_ref, ovt_ref, cm_ref, gx_ref,
                       tri_ref, o_ref, m_ref, l_ref, acc_ref, ob_ref, qa_ref):
    tq = q_ref.shape[1]
    n_cmp = kc_ref.shape[1]
    n_blk = kvb_ref.shape[1] // SLC_BLOCK
    i = pl.program_id(1)
    q0 = i * tq
    lane = lax.broadcasted_iota(jnp.int32, (tq, LANES), 1)
    low = lane < HEAD_DIM

    def stack(qb, g):
        keep = low if g == 0 else jnp.logical_not(low)
        parts = [jnp.where(keep, qb[:, LANES * j:LANES * (j + 1)], jnp.zeros((tq, LANES), BF16))
                 for j in range(GROUP)]
        return jnp.concatenate(parts, axis=0)

    q = q_ref[0]
    qr = qr_ref[0]
    kc = kc_ref[0]
    vc = vc_ref[0]

    rowc = lax.broadcasted_iota(jnp.int32, (tq, n_cmp), 0)
    colc = lax.broadcasted_iota(jnp.int32, (tq, n_cmp), 1)
    vis1 = jnp.where(colc * CMP_STRIDE + (CMP_LEN - 1) <= q0 + rowc, 1.0, 0.0)
    vis = jnp.concatenate([vis1] * GROUP, axis=0) > 0.5

    for g in range(N_KV):
        s = _dot_nt(stack(q, g), kc)
        sm = jnp.where(vis, s, NEG)
        e = jnp.where(vis, jnp.exp(sm - jnp.max(sm, axis=1, keepdims=True)), 0.0)
        lsum = jnp.sum(e, axis=1, keepdims=True)
        pr = e / jnp.where(lsum > 0.0, lsum, 1.0)
        ob_ref[g] = _dot(pr.astype(BF16), vc)
        imp = pr[0:tq]
        for hh in range(1, GROUP):
            imp = imp + pr[hh * tq:(hh + 1) * tq]

        ih, il = _split_bf16(imp)
        sc_t = (_dot_nt(ovt_ref[...], ih) + _dot_nt(ovt_ref[...], il))[0:n_blk]
        blk_t = lax.broadcasted_iota(jnp.int32, sc_t.shape, 0)
        tpos_t = q0 + lax.broadcasted_iota(jnp.int32, sc_t.shape, 1)
        qblk_t = tpos_t // SLC_BLOCK
        forced_t = (blk_t == 0) | (blk_t == qblk_t) | (blk_t == qblk_t - 1)
        elig_t = blk_t * SLC_BLOCK <= tpos_t
        rest_t = _take_top(jnp.where(elig_t & jnp.logical_not(forced_t), sc_t, -jnp.inf),
                           N_SELECT - 3)
        keep_t = elig_t & (forced_t | (rest_t > 0.5))
        drop_t = jnp.where(keep_t, 0.0, 1.0)
        if n_blk < LANES:
            drop_t = jnp.concatenate([drop_t, jnp.zeros((LANES - n_blk, tq), F32)], axis=0)
        drop = jnp.transpose(drop_t).astype(BF16)
        qa_ref[g, :, 0:LANES] = stack(qr, g)
        qa_ref[g, :, LANES:2 * LANES] = jnp.concatenate([drop] * GROUP, axis=0)

    def flash(g):
        return m_ref.at[g], l_ref.at[g], acc_ref.at[g]

    def tile4(bias):
        return jnp.concatenate([bias] * GROUP, axis=0)

    for g in range(N_KV):
        _flash_init(*flash(g))

    def slc_tile(k0, diagonal):
        ka = jnp.concatenate([kvb_ref[0, pl.ds(k0, tq), 0:LANES], cm_ref[pl.ds(k0, tq), :]],
                             axis=1)
        v = kvb_ref[0, pl.ds(k0, tq), LANES:2 * LANES]
        for g in range(N_KV):
            st = _dot_nt(qa_ref[g], ka)
            if diagonal:
                st = st + tile4(tri_ref[0])
            _flash_step(st, v, *flash(g))

    def slc_body(kk, carry):
        slc_tile(pl.multiple_of(kk * tq, tq), False)
        return carry

    lax.fori_loop(0, i, slc_body, 0)
    slc_tile(pl.multiple_of(q0, tq), True)
    for g in range(N_KV):
        ob_ref[N_KV + g] = acc_ref[g] / l_ref[g]

    for g in range(N_KV):
        _flash_init(*flash(g))

    def win_tile(k0, bias):
        k = kvb_ref[0, pl.ds(k0, tq), 2 * LANES:3 * LANES]
        v = kvb_ref[0, pl.ds(k0, tq), 3 * LANES:4 * LANES]
        for g in range(N_KV):
            st = _dot_nt(qa_ref[g, :, 0:LANES], k)
            if bias is not None:
                st = st + tile4(tri_ref[bias])
            _flash_step(st, v, *flash(g))

    win_tile(pl.multiple_of(q0, tq), 0)
    n_back = WINDOW // tq
    for back in range(1, n_back + 1):
        @pl.when(i >= back)
        def _():
            win_tile(pl.multiple_of(q0 - back * tq, tq), 1 if back == n_back else None)
    for g in range(N_KV):
        ob_ref[2 * N_KV + g] = acc_ref[g] / l_ref[g]

    gh, gl = _split_bf16(ng_ref[0])
    for j in range(GROUP):
        tot = None
        for br in range(3):
            gx = gx_ref[:, (br * GROUP + j) * LANES:(br * GROUP + j + 1) * LANES]
            gate = _dot(gh, gx) + _dot(gl, gx)
            o_lo = ob_ref[br * N_KV + 0, j * tq:(j + 1) * tq, :]
            o_hi = ob_ref[br * N_KV + 1, j * tq:(j + 1) * tq, :]
            term = gate * jnp.where(low, o_lo, o_hi)
            tot = term if tot is None else tot + term
        o_ref[0, :, j * LANES:(j + 1) * LANES] = tot.astype(BF16)


def _nsa_prompt_call(q3, qr3, kvb3, kc, vc, ng3, pw):
    b, t, _ = q3.shape
    tq = min(TQ, t)
    n_cmp = kc.shape[1]
    blk = lambda w: pl.BlockSpec((1, tq, w), lambda bi, i: (bi, i, 0))
    per_b = lambda r, w: pl.BlockSpec((1, r, w), lambda bi, i: (bi, 0, 0))
    return pl.pallas_call(
        _nsa_prompt_kernel,
        grid=(b, t // tq),
        in_specs=[blk(512), blk(512), per_b(t, 512), per_b(n_cmp, LANES), per_b(n_cmp, LANES),
                  blk(LANES), _const_spec((LANES, n_cmp)), _const_spec((t, LANES)),
                  _const_spec((LANES, 3 * GROUP * LANES)), _const_spec((2, tq, tq))],
        out_specs=blk(512),
        out_shape=jax.ShapeDtypeStruct((b, t, 512), BF16),
        scratch_shapes=[pltpu.VMEM((N_KV, GROUP * tq, LANES), F32)] * 3
        + [pltpu.VMEM((3 * N_KV, GROUP * tq, LANES), F32),
           pltpu.VMEM((N_KV, GROUP * tq, 2 * LANES), BF16)],
        compiler_params=pltpu.CompilerParams(dimension_semantics=("arbitrary", "arbitrary"),
                                             vmem_limit_bytes=VMEM_LIMIT),
        name="nsa_prompt",
    )(q3, qr3, kvb3, kc, vc, ng3, pw['ov_p'], pw['cm_p'], pw['gate_x'],
      jnp.asarray(_band_bias(tq), F32))


def _memkv_kernel(mem_ref, nm_ref, w_ref, kg_ref, kv_out):
    x = mem_ref[0]
    ms = jnp.sum(x * x, axis=-1, keepdims=True) * (1.0 / D_MODEL)
    xn = (x * lax.rsqrt(ms + EPS) * nm_ref[...]).astype(BF16)
    kv = _dot(xn, w_ref[...])
    for h in range(MEM_HEADS):
        k = kv[:, h * LANES:(h + 1) * LANES]
        kms = jnp.sum(k * k, axis=-1, keepdims=True) * (1.0 / MEM_HEAD_DIM)
        kv_out[0, :, h * LANES:(h + 1) * LANES] = k * lax.rsqrt(kms + EPS) * kg_ref[...]
    kv_out[0, :, BRANCH_W:2 * BRANCH_W] = kv[:, BRANCH_W:2 * BRANCH_W]


def _memkv_call(mem, pw):
    b, m, _ = mem.shape
    return pl.pallas_call(
        _memkv_kernel,
        grid=(b,),
        in_specs=[pl.BlockSpec((1, m, D_MODEL), lambda i: (i, 0, 0)), _const_spec((1, D_MODEL)),
                  _const_spec((D_MODEL, 2 * BRANCH_W)), _const_spec((1, LANES))],
        out_specs=pl.BlockSpec((1, m, 2 * BRANCH_W), lambda i: (i, 0, 0)),
        out_shape=jax.ShapeDtypeStruct((b, m, 2 * BRANCH_W), F32),
        compiler_params=pltpu.CompilerParams(dimension_semantics=("arbitrary",),
                                             vmem_limit_bytes=VMEM_LIMIT),
        name="mem_kv",
    )(mem, pw['nmem'], pw['w_mem'], pw['mk_gain'])


def _mem_attend(mq, kv_ref):
    outs = []
    for h in range(MEM_HEADS):
        k = kv_ref[:, h * LANES:(h + 1) * LANES].astype(BF16)
        v = kv_ref[:, BRANCH_W + h * LANES:BRANCH_W + (h + 1) * LANES].astype(BF16)
        s = _dot_nt(mq[:, h * LANES:(h + 1) * LANES], k) * MEM_HEAD_DIM ** -0.5
        e = jnp.exp(s - jnp.max(s, axis=1, keepdims=True))
        pr = e / jnp.sum(e, axis=1, keepdims=True)
        outs.append(_dot(pr.astype(BF16), v))
    return jnp.concatenate(outs, axis=1)


def _mem_attn_kernel(mq_ref, kv_ref, o_ref):
    o_ref[0] = _mem_attend(mq_ref[0], kv_ref.at[0]).astype(BF16)


def _mem_attn_call(mq3, mkv):
    b, t, _ = mq3.shape
    tq = min(TM, t)
    return pl.pallas_call(
        _mem_attn_kernel,
        grid=(b, t // tq),
        in_specs=[pl.BlockSpec((1, tq, 512), lambda bi, i: (bi, i, 0)),
                  pl.BlockSpec((1, MEM_LEN, 2 * BRANCH_W), lambda bi, i: (bi, 0, 0))],
        out_specs=pl.BlockSpec((1, tq, 512), lambda bi, i: (bi, i, 0)),
        out_shape=jax.ShapeDtypeStruct((b, t, 512), BF16),
        compiler_params=pltpu.CompilerParams(dimension_semantics=("arbitrary", "arbitrary"),
                                             vmem_limit_bytes=VMEM_LIMIT),
        name="mem_attn",
    )(mq3, mkv)


def _finish_kernel(x_ref, on_ref, z_ref, om_ref, nmix_ref, wmg_ref, wbr_ref, wout_ref, h_out):
    x = x_ref[...]
    ms = jnp.sum(x * x, axis=-1, keepdims=True) * (1.0 / D_MODEL)
    xn = (x * lax.rsqrt(ms + EPS) * nmix_ref[...]).astype(BF16)
    merged = None
    for n, br_ref in enumerate((on_ref, z_ref, om_ref)):
        gate = jax.nn.sigmoid(_dot(xn, wmg_ref[:, n * D_MODEL:(n + 1) * D_MODEL]))
        term = gate * _dot(br_ref[...], wbr_ref[n])
        merged = term if merged is None else merged + term
    h_out[...] = x + _dot(merged.astype(BF16), wout_ref[...])


def _finish_call(x, o_nsa, z, o_mem, pw):
    n = x.shape[0]
    tm = min(TM, n)
    tok = lambda w: pl.BlockSpec((tm, w), lambda i: (i, 0))
    return pl.pallas_call(
        _finish_kernel,
        grid=(n // tm,),
        in_specs=[tok(D_MODEL), tok(512), tok(512), tok(512), _const_spec((1, D_MODEL)),
                  _const_spec((D_MODEL, N_BRANCH * D_MODEL)),
                  _const_spec((N_BRANCH, BRANCH_W, D_MODEL)), _const_spec((D_MODEL, D_MODEL))],
        out_specs=tok(D_MODEL),
        out_shape=jax.ShapeDtypeStruct((n, D_MODEL), F32),
        compiler_params=pltpu.CompilerParams(dimension_semantics=("arbitrary",),
                                             vmem_limit_bytes=VMEM_LIMIT),
        name="finish",
    )(x, o_nsa, z, o_mem, pw['nmix'], pw['w_mg'], pw['w_br'], pw['w_out'])


def _ffn_kernel(h_ref, nf_ref, wgu_ref, wd_ref, y_out):
    h = h_ref[...]
    ms = jnp.sum(h * h, axis=-1, keepdims=True) * (1.0 / D_MODEL)
    hn = (h * lax.rsqrt(ms + EPS) * nf_ref[...]).astype(BF16)
    acc = h
    for c in range(D_FF // FF_CHUNK):
        g = _dot(hn, wgu_ref[:, c * FF_CHUNK:(c + 1) * FF_CHUNK])
        u = _dot(hn, wgu_ref[:, D_FF + c * FF_CHUNK:D_FF + (c + 1) * FF_CHUNK])
        a = (jax.nn.silu(g) * u).astype(BF16)
        acc = acc + _dot(a, wd_ref[c * FF_CHUNK:(c + 1) * FF_CHUNK, :])
    y_out[...] = acc


def _ffn_call(h, pw):
    n = h.shape[0]
    tm = min(TM, n)
    tok = pl.BlockSpec((tm, D_MODEL), lambda i: (i, 0))
    return pl.pallas_call(
        _ffn_kernel,
        grid=(n // tm,),
        in_specs=[tok, _const_spec((1, D_MODEL)), _const_spec((D_MODEL, 2 * D_FF)),
                  _const_spec((D_FF, D_MODEL))],
        out_specs=tok,
        out_shape=jax.ShapeDtypeStruct((n, D_MODEL), F32),
        compiler_params=pltpu.CompilerParams(dimension_semantics=("arbitrary",),
                                             vmem_limit_bytes=VMEM_LIMIT),
        name="ffn",
    )(h, pw['nffn'], pw['w_gu'], pw['w_down'])


def _nsa_sample_kernel(layer_base, pt_ref, pool_ref, q_ref, qr_ref, rows_ref, wnew_ref, wbuf_ref,
                       mq_ref, mkv_ref, ng_ref, w1_ref, pe_ref, w2_ref, kg_ref, e64_ref, ov_ref,
                       cm_ref, gx_ref,
                       o_ref, wout_ref, om_ref,
                       kraw_ref, vraw_ref, slc_ref, cmp_sem, slc_sem):
    b = pl.program_id(0)
    nb = pl.num_programs(0)
    n_pages = pt_ref.shape[1]
    n_past = n_pages * PAGE_SIZE
    n_sub = n_past // CMP_STRIDE

    def cmp_copies(bb, slot, j):
        page = layer_base + pt_ref[bb, j]
        dst = pl.ds(pl.multiple_of(j * PAGE_SIZE, PAGE_SIZE), PAGE_SIZE)
        return (pltpu.make_async_copy(pool_ref.at[page, :, 0:LANES],
                                      kraw_ref.at[slot, dst, :], cmp_sem.at[slot]),
                pltpu.make_async_copy(pool_ref.at[page, :, LANES:2 * LANES],
                                      vraw_ref.at[slot, dst, :], cmp_sem.at[slot]))

    def slc_copy(bb, j):
        page = layer_base + pt_ref[bb, j]
        dst = pl.ds(pl.multiple_of(j * PAGE_SIZE, PAGE_SIZE), PAGE_SIZE)
        return pltpu.make_async_copy(pool_ref.at[page, :, 2 * LANES:4 * LANES],
                                     slc_ref.at[dst, :], slc_sem.at[0])

    def start_cmp(bb, slot):
        def body(j, c):
            for cp in cmp_copies(bb, slot, j):
                cp.start()
            return c
        lax.fori_loop(0, n_pages, body, 0)

    def wait_cmp(bb, slot):
        def body(j, c):
            for cp in cmp_copies(bb, slot, j):
                cp.wait()
            return c
        lax.fori_loop(0, n_pages, body, 0)

    slot = b % 2

    @pl.when(b == 0)
    def _():
        start_cmp(b, slot)

    def slc_start_body(j, c):
        slc_copy(b, j).start()
        return c
    lax.fori_loop(0, n_pages, slc_start_body, 0)

    @pl.when(b + 1 < nb)
    def _():
        start_cmp(b + 1, 1 - slot)

    wait_cmp(b, slot)

    row8 = lax.broadcasted_iota(jnp.int32, (8, LANES), 0)
    lane8 = lax.broadcasted_iota(jnp.int32, (8, LANES), 1)
    low = lane8 < HEAD_DIM
    own_half = (row8 < GROUP) == low

    def stack8(qrow):
        out = jnp.zeros((8, LANES), F32)
        for j in range(GROUP):
            pj = jnp.broadcast_to(qrow[:, j * LANES:(j + 1) * LANES], (8, LANES))
            out = jnp.where((row8 & (GROUP - 1)) == j, pj, out)
        return jnp.where(own_half, out, 0.0).astype(BF16)

    q8 = stack8(q_ref[0])
    qr8 = stack8(qr_ref[0])
    e64k = e64_ref[...]

    raw_refs = (kraw_ref, vraw_ref)
    kc, vc = _compress(
        lambda kv, s: raw_refs[kv][slot, pl.ds(s, n_sub, stride=CMP_STRIDE), :],
        n_sub, w1_ref, pe_ref, w2_ref, kg_ref, e64k)
    s = _dot_nt(q8, kc.astype(BF16))
    col = lax.broadcasted_iota(jnp.int32, s.shape, 1)
    rowc = lax.broadcasted_iota(jnp.int32, s.shape, 0)
    vis = col < n_sub - CMP_RATIO + 1
    sm = jnp.where(vis, s, NEG)
    e = jnp.where(vis, jnp.exp(sm - jnp.max(sm, axis=1, keepdims=True)), 0.0)
    pr = e / jnp.sum(e, axis=1, keepdims=True)
    o_c = _dot(pr.astype(BF16), vc.astype(BF16))
    imp8 = jnp.zeros(s.shape, F32)
    for g in range(N_KV):
        imp_g = jnp.sum(jnp.where(rowc // GROUP == g, pr, 0.0), axis=0, keepdims=True)
        imp8 = jnp.where(rowc == g, jnp.broadcast_to(imp_g, s.shape), imp8)

    n_blk_pad = ov_ref.shape[1]
    q_blk = n_past // SLC_BLOCK
    sc = _dot_exact01(imp8, ov_ref[...])
    bl = lax.broadcasted_iota(jnp.int32, sc.shape, 1)
    forced = (bl == 0) | (bl == q_blk) | (bl == q_blk - 1)
    sc = jnp.where(forced, jnp.inf, jnp.where(bl <= q_blk, sc, -jnp.inf))
    sc_t = jnp.transpose(sc)
    drops = []
    for g in range(N_KV):
        colv = jnp.broadcast_to(sc_t[:, g:g + 1], (n_blk_pad, n_blk_pad))
        rowv = jnp.broadcast_to(sc[g:g + 1, :], (n_blk_pad, n_blk_pad))
        ii = lax.broadcasted_iota(jnp.int32, colv.shape, 0)
        jj = lax.broadcasted_iota(jnp.int32, colv.shape, 1)
        tie = jnp.where(ii < jj, 1.0, 0.0)
        beats = jnp.where(colv > rowv, 1.0, jnp.where(colv == rowv, tie, 0.0))
        cnt = jnp.sum(beats, axis=0, keepdims=True)
        drops.append(jnp.where((cnt < N_SELECT) & (bl[0:1] <= q_blk), 0.0, 1.0))
    drop8 = jnp.where(row8 < GROUP, jnp.broadcast_to(drops[0][:, 0:LANES], (8, LANES)),
                      jnp.broadcast_to(drops[1][:, 0:LANES], (8, LANES))).astype(BF16)

    def slc_wait_body(j, c):
        slc_copy(b, j).wait()
        return c
    lax.fori_loop(0, n_pages, slc_wait_body, 0)

    qa = jnp.concatenate([qr8, drop8], axis=1)
    k_past = slc_ref[:, 0:LANES].astype(BF16)
    v_past = slc_ref[:, LANES:2 * LANES].astype(BF16)
    ka = jnp.concatenate([k_past, cm_ref[...]], axis=1)
    s_past = _dot_nt(qa, ka)
    rows = rows_ref[0]
    k_new = rows[:, 2 * LANES:3 * LANES].astype(BF16).astype(F32)
    v_new = rows[:, 3 * LANES:4 * LANES].astype(BF16).astype(F32)
    s_new = jnp.sum(qr8.astype(F32) * k_new, axis=1, keepdims=True)
    m = jnp.maximum(jnp.max(s_past, axis=1, keepdims=True), s_new)
    e_past = jnp.exp(s_past - m)
    e_new = jnp.exp(s_new - m)
    den = jnp.sum(e_past, axis=1, keepdims=True) + e_new
    pr_past = (e_past / den).astype(BF16)
    pr_new = (e_new / den).astype(BF16).astype(F32)
    o_s = _dot(pr_past, v_past) + pr_new * v_new

    wb = wbuf_ref[0]
    wrow = lax.broadcasted_iota(jnp.int32, wb.shape, 0)
    w_upd = jnp.where(wrow == wb.shape[0] - 1, wnew_ref[0], pltpu.roll(wb, wb.shape[0] - 1, 0))
    wout_ref[0] = w_upd
    s_w = _dot_nt(qr8, w_upd[:, 0:LANES].astype(BF16))
    e_w = jnp.exp(s_w - jnp.max(s_w, axis=1, keepdims=True))
    pr_w = e_w / jnp.sum(e_w, axis=1, keepdims=True)
    o_w = _dot(pr_w.astype(BF16), w_upd[:, LANES:2 * LANES].astype(BF16))

    gh, gl = _split_bf16(jnp.broadcast_to(ng_ref[0], (8, LANES)))
    tot = jnp.zeros((8, LANES), F32)
    for br, o8 in enumerate((o_c, o_s, o_w)):
        pair = jnp.where(low, o8, pltpu.roll(o8, GROUP, 0))
        gate8 = jnp.zeros((8, LANES), F32)
        for j in range(GROUP):
            gx = gx_ref[:, (br * GROUP + j) * LANES:(br * GROUP + j + 1) * LANES]
            gate8 = jnp.where(row8 == j, _dot(gh, gx) + _dot(gl, gx), gate8)
        tot = tot + gate8 * pair
    o_ref[0] = tot

    mq8 = jnp.broadcast_to(mq_ref[0], (8, BRANCH_W)).astype(BF16)
    om_ref[0] = _mem_attend(mq8, mkv_ref.at[0])


def _nsa_sample_call(layer, page_table, pool3, q4, qr4, rows_s, win_new, win_buf, mq4, mkv,
                     ng_s, pw):
    nb, n_pages = page_table.shape
    n_past = n_pages * PAGE_SIZE
    n_phys = pool3.shape[0] // DEPTH
    w_buf = win_buf.shape[1]
    n_blk_pad = pw['ov_s'].shape[1]
    per_b = lambda *shape: pl.BlockSpec((1,) + shape, lambda b, pt: (b,) + (0,) * len(shape))
    grid_spec = pltpu.PrefetchScalarGridSpec(
        num_scalar_prefetch=1,
        grid=(nb,),
        in_specs=[pl.BlockSpec(memory_space=pl.ANY),
                  per_b(1, 512), per_b(1, 512), per_b(1, 512), per_b(1, 256),
                  per_b(w_buf, 256), per_b(1, 512), per_b(MEM_LEN, 2 * BRANCH_W),
                  per_b(1, LANES),
                  _const_spec((CMP_RATIO * CMP_STRIDE, 2 * LANES, 2 * CMP_HIDDEN)),
                  _const_spec((2 * CMP_RATIO * CMP_STRIDE, LANES)),
                  _const_spec((2, 2 * CMP_HIDDEN, LANES)),
                  _const_spec((1, LANES)), _const_spec((LANES, LANES)),
                  _const_spec((n_past // CMP_STRIDE, n_blk_pad)), _const_spec((n_past, LANES)),
                  _const_spec((LANES, 3 * GROUP * LANES))],
        out_specs=[per_b(8, LANES), per_b(w_buf, 256), per_b(8, BRANCH_W)],
        scratch_shapes=[pltpu.VMEM((2, n_past, LANES), F32), pltpu.VMEM((2, n_past, LANES), F32),
                        pltpu.VMEM((n_past, 2 * LANES), F32),
                        pltpu.SemaphoreType.DMA((2,)), pltpu.SemaphoreType.DMA((1,))],
    )
    return pl.pallas_call(
        functools.partial(_nsa_sample_kernel, layer * n_phys),
        grid_spec=grid_spec,
        out_shape=[jax.ShapeDtypeStruct((nb, 8, LANES), F32),
                   jax.ShapeDtypeStruct((nb, w_buf, 256), F32),
                   jax.ShapeDtypeStruct((nb, 8, BRANCH_W), F32)],
        compiler_params=pltpu.CompilerParams(dimension_semantics=("arbitrary",),
                                             vmem_limit_bytes=VMEM_LIMIT),
        name="nsa_sample",
    )(page_table, pool3, q4, qr4, rows_s, win_new, win_buf, mq4, mkv, ng_s,
      pw['cmp_w1'], pw['cmp_pe'], pw['cmp_w2'], pw['kc_gain'], pw['e64k'], pw['ov_s'],
      pw['cm_s'], pw['gate_x'])


def _pair_perm():
    p = np.arange(BRANCH_W)
    j, ln = p // LANES, p % LANES
    head = np.where(ln < HEAD_DIM, j, GROUP + j)
    return head * HEAD_DIM + ln % HEAD_DIM


def _block_diag_ones(width, group):
    i = np.arange(width)
    return (i[:, None] // group == i[None, :] // group).astype(np.float32)


def _overlap(n_cmp_rows, n_cmp_valid, n_blk_cols):
    i = np.arange(n_cmp_rows)[:, None]
    j = np.arange(n_blk_cols)[None, :]
    ov = (i * CMP_STRIDE < (j + 1) * SLC_BLOCK) & (i * CMP_STRIDE + CMP_LEN > j * SLC_BLOCK)
    return (ov & (i < n_cmp_valid)).astype(np.float32)


def _block_mask_rows(n_keys):
    k = np.arange(n_keys)[:, None]
    j = np.arange(LANES)[None, :]
    return np.where(k // SLC_BLOCK == j, NEG, 0.0).astype(np.float32)


def _band_bias(tq):
    r = np.arange(tq)[:, None]
    c = np.arange(tq)[None, :]
    return np.stack([np.where(c <= r, 0.0, NEG), np.where(c > r, 0.0, NEG)]).astype(np.float32)


def _gate_expand():
    x = np.zeros((LANES, 3 * GROUP * LANES), np.float32)
    for br in range(3):
        for j in range(GROUP):
            for ln in range(LANES):
                head = j if ln < HEAD_DIM else GROUP + j
                x[br * N_HEADS + head, (br * GROUP + j) * LANES + ln] = 1.0
    return x


def _rope_tables(pos):
    half = ROT_DIM // 2
    inv = ROPE_THETA ** (-jnp.arange(half, dtype=F32) / half)
    ang = pos.astype(F32)[:, None] * inv[None, :]
    cos, sin = jnp.cos(ang), jnp.sin(ang)
    n = pos.shape[0]
    one = jnp.ones((n, HEAD_DIM - ROT_DIM), F32)
    zero = jnp.zeros((n, HEAD_DIM - ROT_DIM), F32)
    zh = jnp.zeros((n, half), F32)
    c = jnp.concatenate([cos, cos, one], axis=1)
    s1 = jnp.concatenate([zh, sin, zero], axis=1)
    s2 = jnp.concatenate([-sin, zh, zero], axis=1)
    tab = jnp.stack([c, s1, s2])
    return jnp.concatenate([tab, tab], axis=2)


def _layer_params(l, seq_len, n_past, norm_mix, w_in, q_norm, k_norm, cmp_pe, cmp_w1, cmp_w2,
                  conv_w, norm_mem, w_mem_kv, mem_q_norm, mem_k_norm, w_branch, w_out, norm_ffn,
                  w_gate_up, w_down):
    perm = _pair_perm()
    wi = w_in[l]
    o_kv, o_ng = BRANCH_W, BRANCH_W + 768
    o_cx = o_ng + 3 * N_HEADS
    o_cb, o_cc, o_mq, o_mg = o_cx + 512, o_cx + 1024, o_cx + 1536, o_cx + 2048
    w_a = jnp.concatenate([
        wi[:, perm], wi[:, o_kv:o_ng], wi[:, o_cx:o_cb], wi[:, o_cb:o_cc], wi[:, o_cc:o_mq],
        wi[:, o_mq:o_mg], wi[:, o_ng:o_cx], jnp.zeros((D_MODEL, LANES - 3 * N_HEADS), F32)],
        axis=1).astype(BF16)
    w1 = cmp_w1[l].reshape(2, CMP_RATIO, CMP_STRIDE, HEAD_DIM, CMP_HIDDEN)
    zw = jnp.zeros_like(w1)
    w1bd = jnp.concatenate([jnp.concatenate([w1, zw], axis=-1),
                            jnp.concatenate([zw, w1], axis=-1)], axis=-2)
    w2 = cmp_w2[l]
    z2 = jnp.zeros_like(w2)
    w2bd = jnp.concatenate([jnp.concatenate([w2, z2], axis=-1),
                            jnp.concatenate([z2, w2], axis=-1)], axis=-2)
    pe = cmp_pe[l].reshape(2 * CMP_RATIO * CMP_STRIDE, HEAD_DIM)
    n_sub_p = seq_len // CMP_STRIDE
    n_sub_s = n_past // CMP_STRIDE
    n_blk_s = -(-(n_past // SLC_BLOCK + 1) // LANES) * LANES
    wbr = w_branch[l]
    return dict(
        nmix=norm_mix[l][None, :], w_a=w_a, w_mg=wi[:, o_mg:].astype(BF16),
        q_gain=jnp.tile(q_norm[l], N_HEADS)[None, :],
        k_gain=jnp.stack([jnp.tile(k_norm[l, 1], N_KV), jnp.tile(k_norm[l, 2], N_KV)]),
        kc_gain=jnp.tile(k_norm[l, 0], N_KV)[None, :],
        mq_gain=jnp.tile(mem_q_norm[l], MEM_HEADS)[None, :],
        mk_gain=mem_k_norm[l][None, :],
        conv_w=conv_w[l],
        cmp_w1=w1bd.reshape(CMP_RATIO * CMP_STRIDE, 2 * LANES, 2 * CMP_HIDDEN).astype(BF16),
        cmp_pe=jnp.concatenate([pe, pe], axis=1),
        cmp_w2=w2bd.astype(BF16),
        nmem=norm_mem[l][None, :], w_mem=w_mem_kv[l].astype(BF16),
        w_br=jnp.concatenate([wbr[0][perm][None], wbr[1:]], axis=0).astype(BF16),
        w_out=w_out[l].astype(BF16),
        nffn=norm_ffn[l][None, :], w_gu=w_gate_up[l].astype(BF16), w_down=w_down[l].astype(BF16),
        e64=jnp.asarray(_block_diag_ones(512, HEAD_DIM), BF16),
        e64k=jnp.asarray(_block_diag_ones(LANES, HEAD_DIM), BF16),
        e128=jnp.asarray(_block_diag_ones(512, MEM_HEAD_DIM), BF16),
        ov_p=jnp.asarray(_overlap(n_sub_p, n_sub_p - CMP_RATIO + 1, LANES).T, BF16),
        cm_p=jnp.asarray(_block_mask_rows(seq_len), BF16),
        ov_s=jnp.asarray(_overlap(n_sub_s, n_sub_s - CMP_RATIO + 1, n_blk_s), BF16),
        cm_s=jnp.asarray(_block_mask_rows(n_past), BF16),
        gate_x=jnp.asarray(_gate_expand(), BF16),
    )


def kernel(x_prompt, x_sample, cache_nsa_kv, cache_win_kv, state_conv, cache_mem_kv, page_table,
           mem_prompt, norm_mix, w_in, q_norm, k_norm, cmp_pe, cmp_w1, cmp_w2, conv_w, norm_mem,
           w_mem_kv, mem_q_norm, mem_k_norm, w_branch, w_out, norm_ffn, w_gate_up, w_down):
    bp, t, _ = x_prompt.shape
    bs, ts, _ = x_sample.shape
    assert ts == 1 and t % TM == 0 and t % TQ == 0
    n_pages = page_table.shape[1]
    n_past = n_pages * PAGE_SIZE
    w_buf = cache_win_kv.shape[2]
    assert w_buf == WINDOW and n_past >= WINDOW and (n_past // SLC_BLOCK) == LANES

    xp = x_prompt.reshape(bp * t, D_MODEL)
    xs = x_sample.reshape(bs, D_MODEL)
    rope_p = _rope_tables(jnp.arange(t))
    rope_s = _rope_tables(jnp.full((bs,), n_past, jnp.int32))
    pool3 = cache_nsa_kv.reshape(DEPTH * cache_nsa_kv.shape[1], PAGE_SIZE, 4 * N_KV * HEAD_DIM)

    rows_p, rows_s, win_p, win_s, conv_p, conv_s, mem_p = [], [], [], [], [], [], []
    for l in range(DEPTH):
        pw = _layer_params(l, t, n_past, norm_mix, w_in, q_norm, k_norm, cmp_pe, cmp_w1, cmp_w2,
                           conv_w, norm_mem, w_mem_kv, mem_q_norm, mem_k_norm, w_branch, w_out,
                           norm_ffn, w_gate_up, w_down)
        q, qr, rows, win, kvb, z, utail, mq, ng = _prep_call(xp, pw, rope_p, False, t)
        rows3 = rows.reshape(bp, t, 512)
        kc, vc = _compress_prompt_call(rows3, pw)
        o_nsa = _nsa_prompt_call(q.reshape(bp, t, 512), qr.reshape(bp, t, 512),
                                 kvb.reshape(bp, t, 512), kc, vc, ng.reshape(bp, t, LANES), pw)
        mkv = _memkv_call(mem_prompt, pw)
        o_mem = _mem_attn_call(mq.reshape(bp, t, 512), mkv)
        h = _finish_call(xp, o_nsa.reshape(bp * t, 512), z, o_mem.reshape(bp * t, 512), pw)
        xp = _ffn_call(h, pw)
        rows_p.append(rows3.reshape(bp, t, 4, N_KV, HEAD_DIM))
        win_p.append(win.reshape(bp, t, 2, N_KV, HEAD_DIM)[:, t - min(WINDOW, t):])
        conv_p.append(utail.reshape(bp, t // TM, 8, 512)[:, -1, 8 - (CONV_W - 1):])
        mem_p.append(mkv.reshape(bp, MEM_LEN, 2, MEM_HEADS, MEM_HEAD_DIM))
        q, qr, rows, win, _, z, u, mq, ng = _prep_call(xs, pw, rope_s, True, 1,
                                                       state=state_conv[l])
        o_nsa, new_win, o_mem = _nsa_sample_call(
            l, page_table, pool3, q.astype(F32).reshape(bs, 1, 512),
            qr.astype(F32).reshape(bs, 1, 512), rows.reshape(bs, 1, 512),
            win.reshape(bs, 1, 256), cache_win_kv[l].reshape(bs, w_buf, 256),
            mq.astype(F32).reshape(bs, 1, 512),
            cache_mem_kv[l].reshape(bs, MEM_LEN, 2 * BRANCH_W), ng.reshape(bs, 1, LANES), pw)
        o_nsa = o_nsa[:, 0:GROUP, :].reshape(bs, 512).astype(BF16)
        o_mem = o_mem[:, 0, :].astype(BF16)
        h = _finish_call(xs, o_nsa, z, o_mem, pw)
        xs = _ffn_call(h, pw)
        rows_s.append(rows.reshape(bs, 1, 4, N_KV, HEAD_DIM))
        win_s.append(new_win.reshape(bs, w_buf, 2, N_KV, HEAD_DIM))
        conv_s.append(jnp.stack([state_conv[l][:, 1, :], u], axis=1))

    return (xp.reshape(bp, t, D_MODEL), xs.reshape(bs, 1, D_MODEL), jnp.stack(rows_p),
            jnp.stack(rows_s), jnp.stack(win_p), jnp.stack(win_s), jnp.stack(conv_p),
            jnp.stack(conv_s), jnp.stack(mem_p))
```

```python
import functools

import numpy as np
import jax
import jax.numpy as jnp
from jax import lax
from jax.experimental import pallas as pl
from jax.experimental.pallas import tpu as pltpu

F32 = jnp.float32
BF16 = jnp.bfloat16

D_MODEL = 1024
DEPTH = 2
PAST_LEN = 8192
PAGE_SIZE = 128
HEAD_DIM = 64
BRANCH_W = D_MODEL // 2
N_HEADS = BRANCH_W // HEAD_DIM
N_KV = N_HEADS // 4
GROUP = N_HEADS // N_KV
ROT_DIM = HEAD_DIM // 4
ROPE_THETA = 500000.0
CMP_LEN = 32
CMP_STRIDE = 16
CMP_RATIO = CMP_LEN // CMP_STRIDE
CMP_HIDDEN = 2 * HEAD_DIM
SLC_BLOCK = 64
N_SELECT = 16
WINDOW = 512
CONV_W = 3
MEM_LEN = 256
MEM_HEADS = 4
MEM_HEAD_DIM = BRANCH_W // MEM_HEADS
N_BRANCH = 3
D_FF = ((8 * D_MODEL + 767) // 768) * 256
EPS = 1e-6

LANES = 128
VMEM_LIMIT = 56 * 1024 * 1024

TM = 512
TQ = 512
FF_CHUNK = 256
NEG = -1e30
M_INIT = -3e38

_C_Q = 0
_C_KV = 512
_C_CX = 1280
_C_CB = 1792
_C_CC = 2304
_C_MQ = 2816
_C_NG = 3328
_N_A = 3456


def _const_spec(shape):
    nd = len(shape)
    return pl.BlockSpec(shape, lambda *_: (0,) * nd, pipeline_mode=pl.Buffered(1))


def _dot(a, b):
    return jnp.dot(a, b, preferred_element_type=F32)


def _dot_nt(a, b):
    return lax.dot_general(a, b, (((1,), (1,)), ((), ())), preferred_element_type=F32)


def _split_bf16(v):
    hi = v.astype(BF16)
    lo = (v - hi.astype(F32)).astype(BF16)
    return hi, lo


def _dot_exact01(v, m01):
    hi, lo = _split_bf16(v)
    return _dot(hi, m01) + _dot(lo, m01)


def _group_rms(v, e01, group, gain):
    ms = _dot_exact01(v * v, e01) * (1.0 / group)
    return v * lax.rsqrt(ms + EPS) * gain


def _rope_cols(v, c, s1, s2):
    outs = []
    for j in range(v.shape[1] // LANES):
        col = v[:, j * LANES:(j + 1) * LANES]
        outs.append(col * c + pltpu.roll(col, ROT_DIM // 2, 1) * s1
                    + pltpu.roll(col, LANES - ROT_DIM // 2, 1) * s2)
    return outs[0] if len(outs) == 1 else jnp.concatenate(outs, axis=1)


def _prep_kernel(sample, tiles_per_seq, *refs):
    if sample:
        (x_ref, nmix_ref, w_ref, qg_ref, kg_ref, mqg_ref, rope_ref, e64_ref, e128_ref, cw_ref,
         s0_ref, s1_ref,
         q_out, qr_out, rows_out, win_out, kvb_out, z_out, u_out, mq_out, ng_out) = refs
    else:
        (x_ref, nmix_ref, w_ref, qg_ref, kg_ref, mqg_ref, rope_ref, e64_ref, e128_ref, cw_ref,
         q_out, qr_out, rows_out, win_out, kvb_out, z_out, u_out, mq_out, ng_out,
         carry_ref) = refs
    tm = x_ref.shape[0]
    x = x_ref[...]
    ms = jnp.sum(x * x, axis=-1, keepdims=True) * (1.0 / D_MODEL)
    xn = (x * lax.rsqrt(ms + EPS) * nmix_ref[...]).astype(BF16)

    rc, rs1, rs2 = rope_ref[0], rope_ref[1], rope_ref[2]
    e64 = e64_ref[...]

    q = _dot(xn, w_ref[:, _C_Q:_C_Q + 512])
    qn = _group_rms(q, e64, HEAD_DIM, qg_ref[...])
    q_out[...] = (qn * HEAD_DIM ** -0.5).astype(BF16)
    qr_out[...] = (_rope_cols(qn, rc, rs1, rs2) * HEAD_DIM ** -0.5).astype(BF16)

    kv = _dot(xn, w_ref[:, _C_KV:_C_KV + 768])
    e64k = e64_ref[0:LANES, 0:LANES]
    k_slc = _rope_cols(_group_rms(kv[:, 256:384], e64k, HEAD_DIM, kg_ref[0:1, :]), rc, rs1, rs2)
    k_win = _rope_cols(_group_rms(kv[:, 512:640], e64k, HEAD_DIM, kg_ref[1:2, :]), rc, rs1, rs2)
    v_slc = kv[:, 384:512]
    v_win = kv[:, 640:768]
    rows_out[:, 0:256] = kv[:, 0:256]
    rows_out[:, 256:384] = k_slc
    rows_out[:, 384:512] = v_slc
    win_out[:, 0:128] = k_win
    win_out[:, 128:256] = v_win
    kvb_out[:, 0:128] = k_slc.astype(BF16)
    kvb_out[:, 128:256] = v_slc.astype(BF16)
    kvb_out[:, 256:384] = k_win.astype(BF16)
    kvb_out[:, 384:512] = v_win.astype(BF16)

    cx = _dot(xn, w_ref[:, _C_CX:_C_CX + 512])
    cc = _dot(xn, w_ref[:, _C_CC:_C_CC + 512])
    cb = _dot(xn, w_ref[:, _C_CB:_C_CB + 512])
    u = cc * cx
    w0, w1, w2 = cw_ref[0:1, :], cw_ref[1:2, :], cw_ref[2:3, :]
    if sample:
        z = cb * (s0_ref[...] * w0 + s1_ref[...] * w1 + u * w2)
        u_out[...] = u
    else:
        @pl.when(pl.program_id(0) % tiles_per_seq == 0)
        def _():
            carry_ref[...] = jnp.zeros_like(carry_ref)
        prev = carry_ref[...]
        row = lax.broadcasted_iota(jnp.int32, u.shape, 0)
        u1 = jnp.where(row == 0, prev[7:8, :], pltpu.roll(u, 1, 0))
        u2 = jnp.where(row == 0, prev[6:7, :],
                       jnp.where(row == 1, prev[7:8, :], pltpu.roll(u, 2, 0)))
        z = cb * (u2 * w0 + u1 * w1 + u * w2)
        tail = u[tm - 8:tm, :]
        carry_ref[...] = tail
        u_out[...] = tail
    z_out[...] = z.astype(BF16)

    mq = _dot(xn, w_ref[:, _C_MQ:_C_MQ + 512])
    mq_out[...] = _group_rms(mq, e128_ref[...], MEM_HEAD_DIM, mqg_ref[...]).astype(BF16)

    ng_out[...] = jax.nn.sigmoid(_dot(xn, w_ref[:, _C_NG:_C_NG + 128]))


def _prep_call(x, pw, rope_tab, sample, seq_len, state=None):
    n = x.shape[0]
    tm = n if sample else TM
    tps = 1 if sample else seq_len // tm
    grid = (n // tm,)
    tok = lambda w: pl.BlockSpec((tm, w), lambda i: (i, 0))
    in_specs = [tok(D_MODEL), _const_spec((1, D_MODEL)), _const_spec((D_MODEL, _N_A)),
                _const_spec((1, 512)), _const_spec((2, LANES)), _const_spec((1, 512)),
                pl.BlockSpec((3, tm, LANES), lambda i: (0, i % tps, 0)),
                _const_spec((512, 512)), _const_spec((512, 512)), _const_spec((CONV_W, 512))]
    args = [x, pw['nmix'], pw['w_a'], pw['q_gain'], pw['k_gain'], pw['mq_gain'], rope_tab,
            pw['e64'], pw['e128'], pw['conv_w']]
    scratch = []
    if sample:
        in_specs += [tok(512), tok(512)]
        args += [state[:, 0, :], state[:, 1, :]]
        u_shape, u_spec = (n, 512), tok(512)
    else:
        scratch = [pltpu.VMEM((8, 512), F32)]
        u_shape, u_spec = (n // tm * 8, 512), pl.BlockSpec((8, 512), lambda i: (i, 0))
    out_shape = [jax.ShapeDtypeStruct((n, 512), BF16), jax.ShapeDtypeStruct((n, 512), BF16),
                 jax.ShapeDtypeStruct((n, 512), F32), jax.ShapeDtypeStruct((n, 256), F32),
                 jax.ShapeDtypeStruct((n, 512), BF16), jax.ShapeDtypeStruct((n, 512), BF16),
                 jax.ShapeDtypeStruct(u_shape, F32), jax.ShapeDtypeStruct((n, 512), BF16),
                 jax.ShapeDtypeStruct((n, LANES), F32)]
    out_specs = [tok(512), tok(512), tok(512), tok(256), tok(512), tok(512), u_spec, tok(512),
                 tok(LANES)]
    return pl.pallas_call(
        functools.partial(_prep_kernel, sample, tps),
        grid=grid, in_specs=in_specs, out_specs=out_specs, out_shape=out_shape,
        scratch_shapes=scratch,
        compiler_params=pltpu.CompilerParams(dimension_semantics=("arbitrary",),
                                             vmem_limit_bytes=VMEM_LIMIT),
        name="prep_sample" if sample else "prep_prompt",
    )(*args)


def _compress(load_xs, n_sub, w1_ref, pe_ref, w2_ref, kg_ref, e64k):
    outs = []
    for kv in range(2):
        acc = [None, None]
        for s in range(0, CMP_STRIDE, 2):
            xa, xb = load_xs(kv, s), load_xs(kv, s + 1)
            for r in range(CMP_RATIO):
                idx = (kv * CMP_RATIO + r) * CMP_STRIDE + s
                lhs = jnp.concatenate([(xa + pe_ref[idx:idx + 1, :]).astype(BF16),
                                       (xb + pe_ref[idx + 1:idx + 2, :]).astype(BF16)], axis=1)
                d = _dot(lhs, w1_ref[idx // 2])
                acc[r] = d if acc[r] is None else acc[r] + d
        h = acc[0] + pltpu.roll(acc[1], n_sub - 1, 0)
        o = _dot(jax.nn.silu(h).astype(BF16), w2_ref[kv])
        outs.append(o)
    kc = _group_rms(outs[0], e64k, HEAD_DIM, kg_ref[...])
    return kc, outs[1]


def _compress_prompt_kernel(xk_ref, xv_ref, w1_ref, pe_ref, w2_ref, kg_ref, e64_ref,
                            kc_out, vc_out):
    n_sub = kc_out.shape[1]
    x_refs = (xk_ref, xv_ref)
    kc, vc = _compress(lambda kv, s: x_refs[kv][0, pl.ds(s, n_sub, stride=CMP_STRIDE), :],
                       n_sub, w1_ref, pe_ref, w2_ref, kg_ref, e64_ref[...])
    kc_out[0] = kc.astype(BF16)
    vc_out[0] = vc.astype(BF16)


def _compress_prompt_call(rows3, pw):
    b, t, _ = rows3.shape
    n_sub = t // CMP_STRIDE
    out = jax.ShapeDtypeStruct((b, n_sub, LANES), BF16)
    return pl.pallas_call(
        _compress_prompt_kernel,
        grid=(b,),
        in_specs=[pl.BlockSpec((1, t, LANES), lambda i: (i, 0, 0)),
                  pl.BlockSpec((1, t, LANES), lambda i: (i, 0, 1)),
                  _const_spec((CMP_RATIO * CMP_STRIDE, 2 * LANES, 2 * CMP_HIDDEN)),
                  _const_spec((2 * CMP_RATIO * CMP_STRIDE, LANES)),
                  _const_spec((2, 2 * CMP_HIDDEN, LANES)),
                  _const_spec((1, LANES)), _const_spec((LANES, LANES))],
        out_specs=[pl.BlockSpec((1, n_sub, LANES), lambda i: (i, 0, 0))] * 2,
        out_shape=[out, out],
        compiler_params=pltpu.CompilerParams(dimension_semantics=("arbitrary",),
                                             vmem_limit_bytes=VMEM_LIMIT),
        name="compress_prompt",
    )(rows3, rows3, pw['cmp_w1'], pw['cmp_pe'], pw['cmp_w2'], pw['kc_gain'], pw['e64k'])


def _flash_init(m_ref, l_ref, acc_ref):
    m_ref[...] = jnp.full(m_ref.shape, M_INIT, F32)
    l_ref[...] = jnp.zeros(l_ref.shape, F32)
    acc_ref[...] = jnp.zeros(acc_ref.shape, F32)


def _flash_step(s, v, m_ref, l_ref, acc_ref):
    m_prev = m_ref[...]
    m_next = jnp.maximum(m_prev, jnp.max(s, axis=1, keepdims=True))
    p = jnp.exp(s - jnp.concatenate([m_next] * (s.shape[1] // LANES), axis=1))
    alpha = jnp.exp(m_prev - m_next)
    l_ref[...] = alpha * l_ref[...] + jnp.sum(p, axis=1, keepdims=True)
    acc_ref[...] = alpha * acc_ref[...] + _dot(p.astype(BF16), v)
    m_ref[...] = m_next


def _take_top(score_t, n_take):
    nb = score_t.shape[0]
    jidx = lax.broadcasted_iota(jnp.int32, score_t.shape, 0).astype(F32)
    taken = jnp.zeros(score_t.shape, F32)
    s = score_t
    for _ in range(n_take):
        mx = jnp.max(s, axis=0, keepdims=True)
        first = jnp.min(jnp.where(s == mx, jidx, float(nb)), axis=0, keepdims=True)
        pick = jidx == first
        taken = jnp.where(pick, 1.0, taken)
        s = jnp.where(pick, -jnp.inf, s)
    return taken


def _nsa_prompt_kernel(q_ref, qr_ref, kvb_ref, kc_ref, vc_ref, ng_ref, ovt_ref, cm_ref, gx_ref,
                       tri_ref, o_ref, m_ref, l_ref, acc_ref, ob_ref, qa_ref):
    tq = q_ref.shape[1]
    n_cmp = kc_ref.shape[1]
    n_blk = kvb_ref.shape[1] // SLC_BLOCK
    i = pl.program_id(1)
    q0 = i * tq
    lane = lax.broadcasted_iota(jnp.int32, (tq, LANES), 1)
    low = lane < HEAD_DIM

    def stack(qb, g):
        keep = low if g == 0 else jnp.logical_not(low)
        parts = [jnp.where(keep, qb[:, LANES * j:LANES * (j + 1)], jnp.zeros((tq, LANES), BF16))
                 for j in range(GROUP)]
        return jnp.concatenate(parts, axis=0)

    q = q_ref[0]
    qr = qr_ref[0]
    kc = kc_ref[0]
    vc = vc_ref[0]

    rowc = lax.broadcasted_iota(jnp.int32, (tq, n_cmp), 0)
    colc = lax.broadcasted_iota(jnp.int32, (tq, n_cmp), 1)
    vis1 = jnp.where(colc * CMP_STRIDE + (CMP_LEN - 1) <= q0 + rowc, 1.0, 0.0)
    vis = jnp.concatenate([vis1] * GROUP, axis=0) > 0.5

    for g in range(N_KV):
        s = _dot_nt(stack(q, g), kc)
        sm = jnp.where(vis, s, NEG)
        e = jnp.where(vis, jnp.exp(sm - jnp.max(sm, axis=1, keepdims=True)), 0.0)
        lsum = jnp.sum(e, axis=1, keepdims=True)
        pr = e / jnp.where(lsum > 0.0, lsum, 1.0)
        ob_ref[g] = _dot(pr.astype(BF16), vc)
        imp = pr[0:tq]
        for hh in range(1, GROUP):
            imp = imp + pr[hh * tq:(hh + 1) * tq]

        ih, il = _split_bf16(imp)
        sc_t = (_dot_nt(ovt_ref[...], ih) + _dot_nt(ovt_ref[...], il))[0:n_blk]
        blk_t = lax.broadcasted_iota(jnp.int32, sc_t.shape, 0)
        tpos_t = q0 + lax.broadcasted_iota(jnp.int32, sc_t.shape, 1)
        qblk_t = tpos_t // SLC_BLOCK
        forced_t = (blk_t == 0) | (blk_t == qblk_t) | (blk_t == qblk_t - 1)
        elig_t = blk_t * SLC_BLOCK <= tpos_t
        rest_t = _take_top(jnp.where(elig_t & jnp.logical_not(forced_t), sc_t, -jnp.inf),
                           N_SELECT - 3)
        keep_t = elig_t & (forced_t | (rest_t > 0.5))
        drop_t = jnp.where(keep_t, 0.0, 1.0)
        if n_blk < LANES:
            drop_t = jnp.concatenate([drop_t, jnp.zeros((LANES - n_blk, tq), F32)], axis=0)
        drop = jnp.transpose(drop_t).astype(BF16)
        qa_ref[g, :, 0:LANES] = stack(qr, g)
        qa_ref[g, :, LANES:2 * LANES] = jnp.concatenate([drop] * GROUP, axis=0)

    def flash(g):
        return m_ref.at[g], l_ref.at[g], acc_ref.at[g]

    def tile4(bias):
        return jnp.concatenate([bias] * GROUP, axis=0)

    for g in range(N_KV):
        _flash_init(*flash(g))

    def slc_tile(k0, diagonal):
        ka = jnp.concatenate([kvb_ref[0, pl.ds(k0, tq), 0:LANES], cm_ref[pl.ds(k0, tq), :]],
                             axis=1)
        v = kvb_ref[0, pl.ds(k0, tq), LANES:2 * LANES]
        for g in range(N_KV):
            st = _dot_nt(qa_ref[g], ka)
            if diagonal:
                st = st + tile4(tri_ref[0])
            _flash_step(st, v, *flash(g))

    def slc_body(kk, carry):
        slc_tile(pl.multiple_of(kk * tq, tq), False)
        return carry

    lax.fori_loop(0, i, slc_body, 0)
    slc_tile(pl.multiple_of(q0, tq), True)
    for g in range(N_KV):
        ob_ref[N_KV + g] = acc_ref[g] / l_ref[g]

    for g in range(N_KV):
        _flash_init(*flash(g))

    def win_tile(k0, bias):
        k = kvb_ref[0, pl.ds(k0, tq), 2 * LANES:3 * LANES]
        v = kvb_ref[0, pl.ds(k0, tq), 3 * LANES:4 * LANES]
        for g in range(N_KV):
            st = _dot_nt(qa_ref[g, :, 0:LANES], k)
            if bias is not None:
                st = st + tile4(tri_ref[bias])
            _flash_step(st, v, *flash(g))

    win_tile(pl.multiple_of(q0, tq), 0)
    n_back = WINDOW // tq
    for back in range(1, n_back + 1):
        @pl.when(i >= back)
        def _():
            win_tile(pl.multiple_of(q0 - back * tq, tq), 1 if back == n_back else None)
    for g in range(N_KV):
        ob_ref[2 * N_KV + g] = acc_ref[g] / l_ref[g]

    gh, gl = _split_bf16(ng_ref[0])
    for j in range(GROUP):
        tot = None
        for br in range(3):
            gx = gx_ref[:, (br * GROUP + j) * LANES:(br * GROUP + j + 1) * LANES]
            gate = _dot(gh, gx) + _dot(gl, gx)
            o_lo = ob_ref[br * N_KV + 0, j * tq:(j + 1) * tq, :]
            o_hi = ob_ref[br * N_KV + 1, j * tq:(j + 1) * tq, :]
            term = gate * jnp.where(low, o_lo, o_hi)
            tot = term if tot is None else tot + term
        o_ref[0, :, j * LANES:(j + 1) * LANES] = tot.astype(BF16)


def _nsa_prompt_call(q3, qr3, kvb3, kc, vc, ng3, pw):
    b, t, _ = q3.shape
    tq = min(TQ, t)
    n_cmp = kc.shape[1]
    blk = lambda w: pl.BlockSpec((1, tq, w), lambda bi, i: (bi, i, 0))
    per_b = lambda r, w: pl.BlockSpec((1, r, w), lambda bi, i: (bi, 0, 0))
    return pl.pallas_call(
        _nsa_prompt_kernel,
        grid=(b, t // tq),
        in_specs=[blk(512), blk(512), per_b(t, 512), per_b(n_cmp, LANES), per_b(n_cmp, LANES),
                  blk(LANES), _const_spec((LANES, n_cmp)), _const_spec((t, LANES)),
                  _const_spec((LANES, 3 * GROUP * LANES)), _const_spec((2, tq, tq))],
        out_specs=blk(512),
        out_shape=jax.ShapeDtypeStruct((b, t, 512), BF16),
        scratch_shapes=[pltpu.VMEM((N_KV, GROUP * tq, LANES), F32)] * 3
        + [pltpu.VMEM((3 * N_KV, GROUP * tq, LANES), F32),
           pltpu.VMEM((N_KV, GROUP * tq, 2 * LANES), BF16)],
        compiler_params=pltpu.CompilerParams(dimension_semantics=("arbitrary", "arbitrary"),
                                             vmem_limit_bytes=VMEM_LIMIT),
        name="nsa_prompt",
    )(q3, qr3, kvb3, kc, vc, ng3, pw['ov_p'], pw['cm_p'], pw['gate_x'],
      jnp.asarray(_band_bias(tq), F32))


def _memkv_kernel(mem_ref, nm_ref, w_ref, kg_ref, kv_out):
    x = mem_ref[0]
    ms = jnp.sum(x * x, axis=-1, keepdims=True) * (1.0 / D_MODEL)
    xn = (x * lax.rsqrt(ms + EPS) * nm_ref[...]).astype(BF16)
    kv = _dot(xn, w_ref[...])
    for h in range(MEM_HEADS):
        k = kv[:, h * LANES:(h + 1) * LANES]
        kms = jnp.sum(k * k, axis=-1, keepdims=True) * (1.0 / MEM_HEAD_DIM)
        kv_out[0, :, h * LANES:(h + 1) * LANES] = k * lax.rsqrt(kms + EPS) * kg_ref[...]
    kv_out[0, :, BRANCH_W:2 * BRANCH_W] = kv[:, BRANCH_W:2 * BRANCH_W]


def _memkv_call(mem, pw):
    b, m, _ = mem.shape
    return pl.pallas_call(
        _memkv_kernel,
        grid=(b,),
        in_specs=[pl.BlockSpec((1, m, D_MODEL), lambda i: (i, 0, 0)), _const_spec((1, D_MODEL)),
                  _const_spec((D_MODEL, 2 * BRANCH_W)), _const_spec((1, LANES))],
        out_specs=pl.BlockSpec((1, m, 2 * BRANCH_W), lambda i: (i, 0, 0)),
        out_shape=jax.ShapeDtypeStruct((b, m, 2 * BRANCH_W), F32),
        compiler_params=pltpu.CompilerParams(dimension_semantics=("arbitrary",),
                                             vmem_limit_bytes=VMEM_LIMIT),
        name="mem_kv",
    )(mem, pw['nmem'], pw['w_mem'], pw['mk_gain'])


def _mem_attend(mq, kv_ref):
    outs = []
    for h in range(MEM_HEADS):
        k = kv_ref[:, h * LANES:(h + 1) * LANES].astype(BF16)
        v = kv_ref[:, BRANCH_W + h * LANES:BRANCH_W + (h + 1) * LANES].astype(BF16)
        s = _dot_nt(mq[:, h * LANES:(h + 1) * LANES], k) * MEM_HEAD_DIM ** -0.5
        e = jnp.exp(s - jnp.max(s, axis=1, keepdims=True))
        pr = e / jnp.sum(e, axis=1, keepdims=True)
        outs.append(_dot(pr.astype(BF16), v))
    return jnp.concatenate(outs, axis=1)


def _mem_attn_kernel(mq_ref, kv_ref, o_ref):
    o_ref[0] = _mem_attend(mq_ref[0], kv_ref.at[0]).astype(BF16)


def _mem_attn_call(mq3, mkv):
    b, t, _ = mq3.shape
    tq = min(TM, t)
    return pl.pallas_call(
        _mem_attn_kernel,
        grid=(b, t // tq),
        in_specs=[pl.BlockSpec((1, tq, 512), lambda bi, i: (bi, i, 0)),
                  pl.BlockSpec((1, MEM_LEN, 2 * BRANCH_W), lambda bi, i: (bi, 0, 0))],
        out_specs=pl.BlockSpec((1, tq, 512), lambda bi, i: (bi, i, 0)),
        out_shape=jax.ShapeDtypeStruct((b, t, 512), BF16),
        compiler_params=pltpu.CompilerParams(dimension_semantics=("arbitrary", "arbitrary"),
                                             vmem_limit_bytes=VMEM_LIMIT),
        name="mem_attn",
    )(mq3, mkv)


def _finish_kernel(x_ref, on_ref, z_ref, om_ref, nmix_ref, wmg_ref, wbr_ref, wout_ref, h_out):
    x = x_ref[...]
    ms = jnp.sum(x * x, axis=-1, keepdims=True) * (1.0 / D_MODEL)
    xn = (x * lax.rsqrt(ms + EPS) * nmix_ref[...]).astype(BF16)
    merged = None
    for n, br_ref in enumerate((on_ref, z_ref, om_ref)):
        gate = jax.nn.sigmoid(_dot(xn, wmg_ref[:, n * D_MODEL:(n + 1) * D_MODEL]))
        term = gate * _dot(br_ref[...], wbr_ref[n])
        merged = term if merged is None else merged + term
    h_out[...] = x + _dot(merged.astype(BF16), wout_ref[...])


def _finish_call(x, o_nsa, z, o_mem, pw):
    n = x.shape[0]
    tm = min(TM, n)
    tok = lambda w: pl.BlockSpec((tm, w), lambda i: (i, 0))
    return pl.pallas_call(
        _finish_kernel,
        grid=(n // tm,),
        in_specs=[tok(D_MODEL), tok(512), tok(512), tok(512), _const_spec((1, D_MODEL)),
                  _const_spec((D_MODEL, N_BRANCH * D_MODEL)),
                  _const_spec((N_BRANCH, BRANCH_W, D_MODEL)), _const_spec((D_MODEL, D_MODEL))],
        out_specs=tok(D_MODEL),
        out_shape=jax.ShapeDtypeStruct((n, D_MODEL), F32),
        compiler_params=pltpu.CompilerParams(dimension_semantics=("arbitrary",),
                                             vmem_limit_bytes=VMEM_LIMIT),
        name="finish",
    )(x, o_nsa, z, o_mem, pw['nmix'], pw['w_mg'], pw['w_br'], pw['w_out'])


def _ffn_kernel(h_ref, nf_ref, wgu_ref, wd_ref, y_out):
    h = h_ref[...]
    ms = jnp.sum(h * h, axis=-1, keepdims=True) * (1.0 / D_MODEL)
    hn = (h * lax.rsqrt(ms + EPS) * nf_ref[...]).astype(BF16)
    acc = h
    for c in range(D_FF // FF_CHUNK):
        g = _dot(hn, wgu_ref[:, c * FF_CHUNK:(c + 1) * FF_CHUNK])
        u = _dot(hn, wgu_ref[:, D_FF + c * FF_CHUNK:D_FF + (c + 1) * FF_CHUNK])
        a = (jax.nn.silu(g) * u).astype(BF16)
        acc = acc + _dot(a, wd_ref[c * FF_CHUNK:(c + 1) * FF_CHUNK, :])
    y_out[...] = acc


def _ffn_call(h, pw):
    n = h.shape[0]
    tm = min(TM, n)
    tok = pl.BlockSpec((tm, D_MODEL), lambda i: (i, 0))
    return pl.pallas_call(
        _ffn_kernel,
        grid=(n // tm,),
        in_specs=[tok, _const_spec((1, D_MODEL)), _const_spec((D_MODEL, 2 * D_FF)),
                  _const_spec((D_FF, D_MODEL))],
        out_specs=tok,
        out_shape=jax.ShapeDtypeStruct((n, D_MODEL), F32),
        compiler_params=pltpu.CompilerParams(dimension_semantics=("arbitrary",),
                                             vmem_limit_bytes=VMEM_LIMIT),
        name="ffn",
    )(h, pw['nffn'], pw['w_gu'], pw['w_down'])


def _nsa_sample_kernel(layer_base, pt_ref, pool_ref, q_ref, qr_ref, rows_ref, wnew_ref, wbuf_ref,
                       mq_ref, mkv_ref, ng_ref, w1_ref, pe_ref, w2_ref, kg_ref, ov_ref, gx_ref,
                       o_ref, wout_ref, om_ref,
                       cbuf_ref, gbuf_ref, idx_vm, idx_sm, cmp_sem, gat_sem, idx_sem):
    b = pl.program_id(0)
    nb = pl.num_programs(0)
    n_pages = pt_ref.shape[1]
    n_past = n_pages * PAGE_SIZE
    n_sub = n_past // CMP_STRIDE
    subs_per_page = PAGE_SIZE // CMP_STRIDE
    n_sg = cbuf_ref.shape[1]
    s_per = CMP_STRIDE // n_sg
    n_gather = gbuf_ref.shape[1]
    q_blk = n_past // SLC_BLOCK

    def cmp_copies(bb, slot, j):
        page = layer_base + pt_ref[bb, j]
        dst = pl.ds(pl.multiple_of(j * subs_per_page, subs_per_page), subs_per_page)
        return [pltpu.make_async_copy(
            pool_ref.at[page, :, pl.ds(sg * s_per, s_per), pl.ds(0, 2 * N_KV), :],
            cbuf_ref.at[slot, sg, dst], cmp_sem.at[slot]) for sg in range(n_sg)]

    def start_cmp(bb, slot):
        def body(j, c):
            for cp in cmp_copies(bb, slot, j):
                cp.start()
            return c
        lax.fori_loop(0, n_pages, body, 0)

    def wait_cmp(bb, slot):
        def body(j, c):
            for cp in cmp_copies(bb, slot, j):
                cp.wait()
            return c
        lax.fori_loop(0, n_pages, body, 0)

    slot = b % 2

    @pl.when(b == 0)
    def _():
        start_cmp(b, slot)

    @pl.when(b + 1 < nb)
    def _():
        start_cmp(b + 1, 1 - slot)

    wait_cmp(b, slot)

    row8 = lax.broadcasted_iota(jnp.int32, (8, LANES), 0)
    lane8 = lax.broadcasted_iota(jnp.int32, (8, LANES), 1)
    low = lane8 < HEAD_DIM
    own_half = (row8 < GROUP) == low

    def stack8(qrow):
        out = jnp.zeros((8, LANES), F32)
        for j in range(GROUP):
            pj = jnp.broadcast_to(qrow[:, j * LANES:(j + 1) * LANES], (8, LANES))
            out = jnp.where((row8 & (GROUP - 1)) == j, pj, out)
        return jnp.where(own_half, out, 0.0).astype(BF16)

    q8 = stack8(q_ref[0])
    qr8 = stack8(qr_ref[0])

    rows_per_sg = n_sub * s_per * 2 * N_KV
    cflat = cbuf_ref.reshape(2 * n_sg * rows_per_sg, HEAD_DIM)

    def load_xs(kv, g, s):
        start = ((slot * n_sg + s // s_per) * rows_per_sg
                 + (s % s_per) * 2 * N_KV + kv * N_KV + g)
        return cflat[pl.ds(start, n_sub, stride=s_per * 2 * N_KV), :]

    quad = 4
    halves = [[None] * N_KV for _ in range(2)]
    for kv in range(2):
        for g in range(N_KV):
            acc = [None] * CMP_RATIO
            for qd in range(CMP_STRIDE // quad):
                xs = [load_xs(kv, g, qd * quad + t) for t in range(quad)]
                for r in range(CMP_RATIO):
                    base = (kv * CMP_RATIO + r) * CMP_STRIDE + qd * quad
                    lhs = jnp.concatenate(
                        [(xs[t] + pe_ref[base + t:base + t + 1, :]).astype(BF16)
                         for t in range(quad)], axis=1)
                    d = _dot(lhs, w1_ref[base // quad])
                    acc[r] = d if acc[r] is None else acc[r] + d
            h = acc[0] + pltpu.roll(acc[1], n_sub - 1, 0)
            halves[kv][g] = _dot(jax.nn.silu(h).astype(BF16), w2_ref[kv])
    for g in range(N_KV):
        o = halves[0][g]
        ms = jnp.sum(o * o, axis=-1, keepdims=True) * (1.0 / HEAD_DIM)
        halves[0][g] = o * lax.rsqrt(ms + EPS) * kg_ref[...]
    kc = jnp.concatenate(halves[0], axis=1)
    vc = jnp.concatenate(halves[1], axis=1)
    s = _dot_nt(q8, kc.astype(BF16))
    col = lax.broadcasted_iota(jnp.int32, s.shape, 1)
    rowc = lax.broadcasted_iota(jnp.int32, s.shape, 0)
    vis = col < n_sub - CMP_RATIO + 1
    sm = jnp.where(vis, s, NEG)
    e = jnp.where(vis, jnp.exp(sm - jnp.max(sm, axis=1, keepdims=True)), 0.0)
    pr = e / jnp.sum(e, axis=1, keepdims=True)
    o_c = _dot(pr.astype(BF16), vc.astype(BF16))
    imp8 = jnp.zeros(s.shape, F32)
    for g in range(N_KV):
        imp_g = jnp.sum(jnp.where(rowc // GROUP == g, pr, 0.0), axis=0, keepdims=True)
        imp8 = jnp.where(rowc == g, jnp.broadcast_to(imp_g, s.shape), imp8)

    n_blk_pad = ov_ref.shape[1]
    sc = _dot_exact01(imp8, ov_ref[...])
    bl = lax.broadcasted_iota(jnp.int32, sc.shape, 1)
    forced = (bl == 0) | (bl == q_blk) | (bl == q_blk - 1)
    sc = jnp.where(forced, jnp.inf, jnp.where(bl <= q_blk, sc, -jnp.inf))
    sc_t = jnp.transpose(sc)
    blf = bl[0:1].astype(F32)
    idx8 = jnp.zeros((8, LANES), F32)
    for g in range(N_KV):
        colv = jnp.broadcast_to(sc_t[:, g:g + 1], (n_blk_pad, n_blk_pad))
        rowv = jnp.broadcast_to(sc[g:g + 1, :], (n_blk_pad, n_blk_pad))
        ii = lax.broadcasted_iota(jnp.int32, colv.shape, 0)
        jj = lax.broadcasted_iota(jnp.int32, colv.shape, 1)
        tie = jnp.where(ii < jj, 1.0, 0.0)
        beats = jnp.where(colv > rowv, 1.0, jnp.where(colv == rowv, tie, 0.0))
        rank = jnp.sum(beats, axis=0, keepdims=True)
        for k in range(N_SELECT):
            blk_k = jnp.sum(jnp.where(rank == float(k), blf, 0.0), axis=1, keepdims=True)
            idx8 = jnp.where((row8 == g) & (lane8 == k), blk_k, idx8)
    idx_vm[...] = idx8.astype(jnp.int32)
    idx_copy = pltpu.make_async_copy(idx_vm, idx_sm, idx_sem.at[0])
    idx_copy.start()

    wb = wbuf_ref[0]
    wrow = lax.broadcasted_iota(jnp.int32, wb.shape, 0)
    w_upd = jnp.where(wrow == wb.shape[0] - 1, wnew_ref[0], pltpu.roll(wb, wb.shape[0] - 1, 0))
    wout_ref[0] = w_upd
    s_w = _dot_nt(qr8, w_upd[:, 0:LANES].astype(BF16))
    e_w = jnp.exp(s_w - jnp.max(s_w, axis=1, keepdims=True))
    pr_w = e_w / jnp.sum(e_w, axis=1, keepdims=True)
    o_w = _dot(pr_w.astype(BF16), w_upd[:, LANES:2 * LANES].astype(BF16))

    idx_copy.wait()
    past_ranks = [k for k in range(N_SELECT) if k != 2]
    gathers = []
    for g in range(N_KV):
        for kk, k in enumerate(past_ranks):
            blk = jnp.minimum(idx_sm[g, k], q_blk - 1)
            page = layer_base + pt_ref[b, lax.shift_right_logical(blk, 1)]
            sub0 = (blk & 1) * (SLC_BLOCK // CMP_STRIDE)
            gathers.append(pltpu.make_async_copy(
                pool_ref.at[page, pl.ds(sub0, SLC_BLOCK // CMP_STRIDE), :,
                            pl.ds(2 * N_KV, 2 * N_KV), :],
                gbuf_ref.at[g, kk], gat_sem.at[0]))
    for cp in gathers:
        cp.start()

    mq8 = jnp.broadcast_to(mq_ref[0], (8, BRANCH_W)).astype(BF16)
    om_ref[0] = _mem_attend(mq8, mkv_ref.at[0])

    for cp in gathers:
        cp.wait()
    rows = rows_ref[0]
    k_new = rows[:, 2 * LANES:3 * LANES].astype(BF16).astype(F32)
    v_new = rows[:, 3 * LANES:4 * LANES].astype(BF16).astype(F32)
    s_new = jnp.sum(qr8.astype(F32) * k_new, axis=1, keepdims=True)
    rows_per_g = n_gather * SLC_BLOCK * 2 * N_KV
    gflat = gbuf_ref.reshape(N_KV * rows_per_g, HEAD_DIM)
    n_keys = n_gather * SLC_BLOCK
    o_parts, p_new = [], []
    for g in range(N_KV):
        k_g = gflat[pl.ds(g * rows_per_g + g, n_keys, stride=2 * N_KV), :].astype(BF16)
        v_g = gflat[pl.ds(g * rows_per_g + N_KV + g, n_keys, stride=2 * N_KV), :].astype(BF16)
        s_past = _dot_nt(qr8[:, g * HEAD_DIM:(g + 1) * HEAD_DIM], k_g)
        m = jnp.maximum(jnp.max(s_past, axis=1, keepdims=True), s_new)
        e_past = jnp.exp(s_past - m)
        e_new = jnp.exp(s_new - m)
        den = jnp.sum(e_past, axis=1, keepdims=True) + e_new
        o_parts.append(_dot((e_past / den).astype(BF16), v_g))
        p_new.append((e_new / den).astype(BF16).astype(F32))
    pr_new = jnp.where(row8[:, 0:1] < GROUP, p_new[0], p_new[1])
    o_s = jnp.concatenate(o_parts, axis=1) + pr_new * v_new

    gh, gl = _split_bf16(jnp.broadcast_to(ng_ref[0], (8, LANES)))
    tot = jnp.zeros((8, LANES), F32)
    for br, o8 in enumerate((o_c, o_s, o_w)):
        pair = jnp.where(low, o8, pltpu.roll(o8, GROUP, 0))
        gate8 = jnp.zeros((8, LANES), F32)
        for j in range(GROUP):
            gx = gx_ref[:, (br * GROUP + j) * LANES:(br * GROUP + j + 1) * LANES]
            gate8 = jnp.where(row8 == j, _dot(gh, gx) + _dot(gl, gx), gate8)
        tot = tot + gate8 * pair
    o_ref[0] = tot


def _nsa_sample_call(layer, page_table, pool5, q4, qr4, rows_s, win_new, win_buf, mq4, mkv,
                     ng_s, pw):
    nb, n_pages = page_table.shape
    n_past = n_pages * PAGE_SIZE
    n_sub = n_past // CMP_STRIDE
    n_phys = pool5.shape[0] // DEPTH
    w_buf = win_buf.shape[1]
    n_blk_pad = pw['ov_s'].shape[1]
    n_sg = 8
    quad = 4
    per_b = lambda *shape: pl.BlockSpec((1,) + shape, lambda b, pt: (b,) + (0,) * len(shape))
    grid_spec = pltpu.PrefetchScalarGridSpec(
        num_scalar_prefetch=1,
        grid=(nb,),
        in_specs=[pl.BlockSpec(memory_space=pl.ANY),
                  per_b(1, 512), per_b(1, 512), per_b(1, 512), per_b(1, 256),
                  per_b(w_buf, 256), per_b(1, 512), per_b(MEM_LEN, 2 * BRANCH_W),
                  per_b(1, LANES),
                  _const_spec((2 * CMP_RATIO * CMP_STRIDE // quad, quad * HEAD_DIM, CMP_HIDDEN)),
                  _const_spec((2 * CMP_RATIO * CMP_STRIDE, HEAD_DIM)),
                  _const_spec((2, CMP_HIDDEN, HEAD_DIM)),
                  _const_spec((1, HEAD_DIM)),
                  _const_spec((n_sub, n_blk_pad)),
                  _const_spec((LANES, 3 * GROUP * LANES))],
        out_specs=[per_b(8, LANES), per_b(w_buf, 256), per_b(8, BRANCH_W)],
        scratch_shapes=[
            pltpu.VMEM((2, n_sg, n_sub, CMP_STRIDE // n_sg, 2 * N_KV, HEAD_DIM), F32),
            pltpu.VMEM((N_KV, N_SELECT - 1, SLC_BLOCK // CMP_STRIDE, CMP_STRIDE, 2 * N_KV,
                        HEAD_DIM), F32),
            pltpu.VMEM((8, LANES), jnp.int32), pltpu.SMEM((8, LANES), jnp.int32),
            pltpu.SemaphoreType.DMA((2,)), pltpu.SemaphoreType.DMA((1,)),
            pltpu.SemaphoreType.DMA((1,))],
    )
    return pl.pallas_call(
        functools.partial(_nsa_sample_kernel, layer * n_phys),
        grid_spec=grid_spec,
        out_shape=[jax.ShapeDtypeStruct((nb, 8, LANES), F32),
                   jax.ShapeDtypeStruct((nb, w_buf, 256), F32),
                   jax.ShapeDtypeStruct((nb, 8, BRANCH_W), F32)],
        compiler_params=pltpu.CompilerParams(dimension_semantics=("arbitrary",),
                                             vmem_limit_bytes=VMEM_LIMIT),
        name="nsa_sample",
    )(page_table, pool5, q4, qr4, rows_s, win_new, win_buf, mq4, mkv, ng_s,
      pw['cmp_w1q'], pw['cmp_pe64'], pw['cmp_w2n'], pw['kc_gain64'], pw['ov_s'], pw['gate_x'])


def _pair_perm():
    p = np.arange(BRANCH_W)
    j, ln = p // LANES, p % LANES
    head = np.where(ln < HEAD_DIM, j, GROUP + j)
    return head * HEAD_DIM + ln % HEAD_DIM


def _block_diag_ones(width, group):
    i = np.arange(width)
    return (i[:, None] // group == i[None, :] // group).astype(np.float32)


def _overlap(n_cmp_rows, n_cmp_valid, n_blk_cols):
    i = np.arange(n_cmp_rows)[:, None]
    j = np.arange(n_blk_cols)[None, :]
    ov = (i * CMP_STRIDE < (j + 1) * SLC_BLOCK) & (i * CMP_STRIDE + CMP_LEN > j * SLC_BLOCK)
    return (ov & (i < n_cmp_valid)).astype(np.float32)


def _block_mask_rows(n_keys):
    k = np.arange(n_keys)[:, None]
    j = np.arange(LANES)[None, :]
    return np.where(k // SLC_BLOCK == j, NEG, 0.0).astype(np.float32)


def _band_bias(tq):
    r = np.arange(tq)[:, None]
    c = np.arange(tq)[None, :]
    return np.stack([np.where(c <= r, 0.0, NEG), np.where(c > r, 0.0, NEG)]).astype(np.float32)


def _gate_expand():
    x = np.zeros((LANES, 3 * GROUP * LANES), np.float32)
    for br in range(3):
        for j in range(GROUP):
            for ln in range(LANES):
                head = j if ln < HEAD_DIM else GROUP + j
                x[br * N_HEADS + head, (br * GROUP + j) * LANES + ln] = 1.0
    return x


def _rope_tables(pos):
    half = ROT_DIM // 2
    inv = ROPE_THETA ** (-jnp.arange(half, dtype=F32) / half)
    ang = pos.astype(F32)[:, None] * inv[None, :]
    cos, sin = jnp.cos(ang), jnp.sin(ang)
    n = pos.shape[0]
    one = jnp.ones((n, HEAD_DIM - ROT_DIM), F32)
    zero = jnp.zeros((n, HEAD_DIM - ROT_DIM), F32)
    zh = jnp.zeros((n, half), F32)
    c = jnp.concatenate([cos, cos, one], axis=1)
    s1 = jnp.concatenate([zh, sin, zero], axis=1)
    s2 = jnp.concatenate([-sin, zh, zero], axis=1)
    tab = jnp.stack([c, s1, s2])
    return jnp.concatenate([tab, tab], axis=2)


def _layer_params(l, seq_len, n_past, norm_mix, w_in, q_norm, k_norm, cmp_pe, cmp_w1, cmp_w2,
                  conv_w, norm_mem, w_mem_kv, mem_q_norm, mem_k_norm, w_branch, w_out, norm_ffn,
                  w_gate_up, w_down):
    perm = _pair_perm()
    wi = w_in[l]
    o_kv, o_ng = BRANCH_W, BRANCH_W + 768
    o_cx = o_ng + 3 * N_HEADS
    o_cb, o_cc, o_mq, o_mg = o_cx + 512, o_cx + 1024, o_cx + 1536, o_cx + 2048
    w_a = jnp.concatenate([
        wi[:, perm], wi[:, o_kv:o_ng], wi[:, o_cx:o_cb], wi[:, o_cb:o_cc], wi[:, o_cc:o_mq],
        wi[:, o_mq:o_mg], wi[:, o_ng:o_cx], jnp.zeros((D_MODEL, LANES - 3 * N_HEADS), F32)],
        axis=1).astype(BF16)
    w1 = cmp_w1[l].reshape(2, CMP_RATIO, CMP_STRIDE, HEAD_DIM, CMP_HIDDEN)
    zw = jnp.zeros_like(w1)
    w1bd = jnp.concatenate([jnp.concatenate([w1, zw], axis=-1),
                            jnp.concatenate([zw, w1], axis=-1)], axis=-2)
    w2 = cmp_w2[l]
    z2 = jnp.zeros_like(w2)
    w2bd = jnp.concatenate([jnp.concatenate([w2, z2], axis=-1),
                            jnp.concatenate([z2, w2], axis=-1)], axis=-2)
    pe = cmp_pe[l].reshape(2 * CMP_RATIO * CMP_STRIDE, HEAD_DIM)
    n_sub_p = seq_len // CMP_STRIDE
    n_sub_s = n_past // CMP_STRIDE
    n_blk_s = -(-(n_past // SLC_BLOCK + 1) // LANES) * LANES
    wbr = w_branch[l]
    return dict(
        nmix=norm_mix[l][None, :], w_a=w_a, w_mg=wi[:, o_mg:].astype(BF16),
        q_gain=jnp.tile(q_norm[l], N_HEADS)[None, :],
        k_gain=jnp.stack([jnp.tile(k_norm[l, 1], N_KV), jnp.tile(k_norm[l, 2], N_KV)]),
        kc_gain=jnp.tile(k_norm[l, 0], N_KV)[None, :],
        mq_gain=jnp.tile(mem_q_norm[l], MEM_HEADS)[None, :],
        mk_gain=mem_k_norm[l][None, :],
        conv_w=conv_w[l],
        cmp_w1=w1bd.reshape(CMP_RATIO * CMP_STRIDE, 2 * LANES, 2 * CMP_HIDDEN).astype(BF16),
        cmp_pe=jnp.concatenate([pe, pe], axis=1),
        cmp_w2=w2bd.astype(BF16),
        cmp_w1q=cmp_w1[l].reshape(2 * CMP_RATIO * CMP_STRIDE // 4, 4 * HEAD_DIM,
                                  CMP_HIDDEN).astype(BF16),
        cmp_pe64=pe, cmp_w2n=w2.astype(BF16), kc_gain64=k_norm[l, 0][None, :],
        nmem=norm_mem[l][None, :], w_mem=w_mem_kv[l].astype(BF16),
        w_br=jnp.concatenate([wbr[0][perm][None], wbr[1:]], axis=0).astype(BF16),
        w_out=w_out[l].astype(BF16),
        nffn=norm_ffn[l][None, :], w_gu=w_gate_up[l].astype(BF16), w_down=w_down[l].astype(BF16),
        e64=jnp.asarray(_block_diag_ones(512, HEAD_DIM), BF16),
        e64k=jnp.asarray(_block_diag_ones(LANES, HEAD_DIM), BF16),
        e128=jnp.asarray(_block_diag_ones(512, MEM_HEAD_DIM), BF16),
        ov_p=jnp.asarray(_overlap(n_sub_p, n_sub_p - CMP_RATIO + 1, LANES).T, BF16),
        cm_p=jnp.asarray(_block_mask_rows(seq_len), BF16),
        ov_s=jnp.asarray(_overlap(n_sub_s, n_sub_s - CMP_RATIO + 1, n_blk_s), BF16),
        gate_x=jnp.asarray(_gate_expand(), BF16),
    )


def kernel(x_prompt, x_sample, cache_nsa_kv, cache_win_kv, state_conv, cache_mem_kv, page_table,
           mem_prompt, norm_mix, w_in, q_norm, k_norm, cmp_pe, cmp_w1, cmp_w2, conv_w, norm_mem,
           w_mem_kv, mem_q_norm, mem_k_norm, w_branch, w_out, norm_ffn, w_gate_up, w_down):
    bp, t, _ = x_prompt.shape
    bs, ts, _ = x_sample.shape
    assert ts == 1 and t % TM == 0 and t % TQ == 0
    n_pages = page_table.shape[1]
    n_past = n_pages * PAGE_SIZE
    w_buf = cache_win_kv.shape[2]
    assert w_buf == WINDOW and n_past >= WINDOW and (n_past // SLC_BLOCK) == LANES

    xp = x_prompt.reshape(bp * t, D_MODEL)
    xs = x_sample.reshape(bs, D_MODEL)
    rope_p = _rope_tables(jnp.arange(t))
    rope_s = _rope_tables(jnp.full((bs,), n_past, jnp.int32))
    pool5 = cache_nsa_kv.reshape(DEPTH * cache_nsa_kv.shape[1], PAGE_SIZE // CMP_STRIDE,
                                 CMP_STRIDE, 4 * N_KV, HEAD_DIM)

    rows_p, rows_s, win_p, win_s, conv_p, conv_s, mem_p = [], [], [], [], [], [], []
    for l in range(DEPTH):
        pw = _layer_params(l, t, n_past, norm_mix, w_in, q_norm, k_norm, cmp_pe, cmp_w1, cmp_w2,
                           conv_w, norm_mem, w_mem_kv, mem_q_norm, mem_k_norm, w_branch, w_out,
                           norm_ffn, w_gate_up, w_down)
        q, qr, rows, win, kvb, z, utail, mq, ng = _prep_call(xp, pw, rope_p, False, t)
        rows3 = rows.reshape(bp, t, 512)
        kc, vc = _compress_prompt_call(rows3, pw)
        o_nsa = _nsa_prompt_call(q.reshape(bp, t, 512), qr.reshape(bp, t, 512),
                                 kvb.reshape(bp, t, 512), kc, vc, ng.reshape(bp, t, LANES), pw)
        mkv = _memkv_call(mem_prompt, pw)
        o_mem = _mem_attn_call(mq.reshape(bp, t, 512), mkv)
        h = _finish_call(xp, o_nsa.reshape(bp * t, 512), z, o_mem.reshape(bp * t, 512), pw)
        xp = _ffn_call(h, pw)
        rows_p.append(rows3.reshape(bp, t, 4, N_KV, HEAD_DIM))
        win_p.append(win.reshape(bp, t, 2, N_KV, HEAD_DIM)[:, t - min(WINDOW, t):])
        conv_p.append(utail.reshape(bp, t // TM, 8, 512)[:, -1, 8 - (CONV_W - 1):])
        mem_p.append(mkv.reshape(bp, MEM_LEN, 2, MEM_HEADS, MEM_HEAD_DIM))
        q, qr, rows, win, _, z, u, mq, ng = _prep_call(xs, pw, rope_s, True, 1,
                                                       state=state_conv[l])
        o_nsa, new_win, o_mem = _nsa_sample_call(
            l, page_table, pool5, q.astype(F32).reshape(bs, 1, 512),
            qr.astype(F32).reshape(bs, 1, 512), rows.reshape(bs, 1, 512),
            win.reshape(bs, 1, 256), cache_win_kv[l].reshape(bs, w_buf, 256),
            mq.astype(F32).reshape(bs, 1, 512),
            cache_mem_kv[l].reshape(bs, MEM_LEN, 2 * BRANCH_W), ng.reshape(bs, 1, LANES), pw)
        o_nsa = o_nsa[:, 0:GROUP, :].reshape(bs, 512).astype(BF16)
        o_mem = o_mem[:, 0, :].astype(BF16)
        h = _finish_call(xs, o_nsa, z, o_mem, pw)
        xs = _ffn_call(h, pw)
        rows_s.append(rows.reshape(bs, 1, 4, N_KV, HEAD_DIM))
        win_s.append(new_win.reshape(bs, w_buf, 2, N_KV, HEAD_DIM))
        conv_s.append(jnp.stack([state_conv[l][:, 1, :], u], axis=1))

    return (xp.reshape(bp, t, D_MODEL), xs.reshape(bs, 1, D_MODEL), jnp.stack(rows_p),
            jnp.stack(rows_s), jnp.stack(win_p), jnp.stack(win_s), jnp.stack(conv_p),
            jnp.stack(conv_s), jnp.stack(mem_p))
```

```python
import functools

import numpy as np
import jax
import jax.numpy as jnp
from jax import lax
from jax.experimental import pallas as pl
from jax.experimental.pallas import tpu as pltpu

F32 = jnp.float32
BF16 = jnp.bfloat16

D_MODEL = 1024
DEPTH = 2
PAST_LEN = 8192
PAGE_SIZE = 128
HEAD_DIM = 64
BRANCH_W = D_MODEL // 2
N_HEADS = BRANCH_W // HEAD_DIM
N_KV = N_HEADS // 4
GROUP = N_HEADS // N_KV
ROT_DIM = HEAD_DIM // 4
ROPE_THETA = 500000.0
CMP_LEN = 32
CMP_STRIDE = 16
CMP_RATIO = CMP_LEN // CMP_STRIDE
CMP_HIDDEN = 2 * HEAD_DIM
SLC_BLOCK = 64
N_SELECT = 16
WINDOW = 512
CONV_W = 3
MEM_LEN = 256
MEM_HEADS = 4
MEM_HEAD_DIM = BRANCH_W // MEM_HEADS
N_BRANCH = 3
D_FF = ((8 * D_MODEL + 767) // 768) * 256
EPS = 1e-6

LANES = 128
VMEM_LIMIT = 56 * 1024 * 1024

TM = 512
TQ = 512
FF_CHUNK = 256
NEG = -1e30
M_INIT = -3e38

_C_Q = 0
_C_KV = 512
_C_CX = 1280
_C_CB = 1792
_C_CC = 2304
_C_MQ = 2816
_C_NG = 3328
_N_A = 3456


def _const_spec(shape):
    nd = len(shape)
    return pl.BlockSpec(shape, lambda *_: (0,) * nd, pipeline_mode=pl.Buffered(1))


def _dot(a, b):
    return jnp.dot(a, b, preferred_element_type=F32)


def _dot_nt(a, b):
    return lax.dot_general(a, b, (((1,), (1,)), ((), ())), preferred_element_type=F32)


def _split_bf16(v):
    hi = v.astype(BF16)
    lo = (v - hi.astype(F32)).astype(BF16)
    return hi, lo


def _dot_exact01(v, m01):
    hi, lo = _split_bf16(v)
    return _dot(hi, m01) + _dot(lo, m01)


def _group_rms(v, e01, group, gain):
    ms = _dot_exact01(v * v, e01) * (1.0 / group)
    return v * lax.rsqrt(ms + EPS) * gain


def _rope_cols(v, c, s1, s2):
    outs = []
    for j in range(v.shape[1] // LANES):
        col = v[:, j * LANES:(j + 1) * LANES]
        outs.append(col * c + pltpu.roll(col, ROT_DIM // 2, 1) * s1
                    + pltpu.roll(col, LANES - ROT_DIM // 2, 1) * s2)
    return outs[0] if len(outs) == 1 else jnp.concatenate(outs, axis=1)


def _prep_kernel(sample, tiles_per_seq, *refs):
    if sample:
        (x_ref, nmix_ref, w_ref, qg_ref, kg_ref, mqg_ref, rope_ref, e64_ref, e128_ref, cw_ref,
         s0_ref, s1_ref,
         q_out, qr_out, rows_out, win_out, kvb_out, z_out, u_out, mq_out, ng_out) = refs
    else:
        (x_ref, nmix_ref, w_ref, qg_ref, kg_ref, mqg_ref, rope_ref, e64_ref, e128_ref, cw_ref,
         q_out, qr_out, rows_out, win_out, kvb_out, z_out, u_out, mq_out, ng_out,
         carry_ref) = refs
    tm = x_ref.shape[0]
    x = x_ref[...]
    ms = jnp.sum(x * x, axis=-1, keepdims=True) * (1.0 / D_MODEL)
    xn = (x * lax.rsqrt(ms + EPS) * nmix_ref[...]).astype(BF16)

    rc, rs1, rs2 = rope_ref[0], rope_ref[1], rope_ref[2]
    e64 = e64_ref[...]

    q = _dot(xn, w_ref[:, _C_Q:_C_Q + 512])
    qn = _group_rms(q, e64, HEAD_DIM, qg_ref[...])
    q_out[...] = (qn * HEAD_DIM ** -0.5).astype(BF16)
    qr_out[...] = (_rope_cols(qn, rc, rs1, rs2) * HEAD_DIM ** -0.5).astype(BF16)

    kv = _dot(xn, w_ref[:, _C_KV:_C_KV + 768])
    e64k = e64_ref[0:LANES, 0:LANES]
    k_slc = _rope_cols(_group_rms(kv[:, 256:384], e64k, HEAD_DIM, kg_ref[0:1, :]), rc, rs1, rs2)
    k_win = _rope_cols(_group_rms(kv[:, 512:640], e64k, HEAD_DIM, kg_ref[1:2, :]), rc, rs1, rs2)
    v_slc = kv[:, 384:512]
    v_win = kv[:, 640:768]
    rows_out[:, 0:256] = kv[:, 0:256]
    rows_out[:, 256:384] = k_slc
    rows_out[:, 384:512] = v_slc
    win_out[:, 0:128] = k_win
    win_out[:, 128:256] = v_win
    kvb_out[:, 0:128] = k_slc.astype(BF16)
    kvb_out[:, 128:256] = v_slc.astype(BF16)
    kvb_out[:, 256:384] = k_win.astype(BF16)
    kvb_out[:, 384:512] = v_win.astype(BF16)

    cx = _dot(xn, w_ref[:, _C_CX:_C_CX + 512])
    cc = _dot(xn, w_ref[:, _C_CC:_C_CC + 512])
    cb = _dot(xn, w_ref[:, _C_CB:_C_CB + 512])
    u = cc * cx
    w0, w1, w2 = cw_ref[0:1, :], cw_ref[1:2, :], cw_ref[2:3, :]
    if sample:
        z = cb * (s0_ref[...] * w0 + s1_ref[...] * w1 + u * w2)
        u_out[...] = u
    else:
        @pl.when(pl.program_id(0) % tiles_per_seq == 0)
        def _():
            carry_ref[...] = jnp.zeros_like(carry_ref)
        prev = carry_ref[...]
        row = lax.broadcasted_iota(jnp.int32, u.shape, 0)
        u1 = jnp.where(row == 0, prev[7:8, :], pltpu.roll(u, 1, 0))
        u2 = jnp.where(row == 0, prev[6:7, :],
                       jnp.where(row == 1, prev[7:8, :], pltpu.roll(u, 2, 0)))
        z = cb * (u2 * w0 + u1 * w1 + u * w2)
        tail = u[tm - 8:tm, :]
        carry_ref[...] = tail
        u_out[...] = tail
    z_out[...] = z.astype(BF16)

    mq = _dot(xn, w_ref[:, _C_MQ:_C_MQ + 512])
    mq_out[...] = _group_rms(mq, e128_ref[...], MEM_HEAD_DIM, mqg_ref[...]).astype(BF16)

    ng_out[...] = jax.nn.sigmoid(_dot(xn, w_ref[:, _C_NG:_C_NG + 128]))


def _prep_call(x, pw, rope_tab, sample, seq_len, state=None):
    n = x.shape[0]
    tm = n if sample else TM
    tps = 1 if sample else seq_len // tm
    grid = (n // tm,)
    tok = lambda w: pl.BlockSpec((tm, w), lambda i: (i, 0))
    in_specs = [tok(D_MODEL), _const_spec((1, D_MODEL)), _const_spec((D_MODEL, _N_A)),
                _const_spec((1, 512)), _const_spec((2, LANES)), _const_spec((1, 512)),
                pl.BlockSpec((3, tm, LANES), lambda i: (0, i % tps, 0)),
                _const_spec((512, 512)), _const_spec((512, 512)), _const_spec((CONV_W, 512))]
    args = [x, pw['nmix'], pw['w_a'], pw['q_gain'], pw['k_gain'], pw['mq_gain'], rope_tab,
            pw['e64'], pw['e128'], pw['conv_w']]
    scratch = []
    if sample:
        in_specs += [tok(512), tok(512)]
        args += [state[:, 0, :], state[:, 1, :]]
        u_shape, u_spec = (n, 512), tok(512)
    else:
        scratch = [pltpu.VMEM((8, 512), F32)]
        u_shape, u_spec = (n // tm * 8, 512), pl.BlockSpec((8, 512), lambda i: (i, 0))
    out_shape = [jax.ShapeDtypeStruct((n, 512), BF16), jax.ShapeDtypeStruct((n, 512), BF16),
                 jax.ShapeDtypeStruct((n, 512), F32), jax.ShapeDtypeStruct((n, 256), F32),
                 jax.ShapeDtypeStruct((n, 512), BF16), jax.ShapeDtypeStruct((n, 512), BF16),
                 jax.ShapeDtypeStruct(u_shape, F32), jax.ShapeDtypeStruct((n, 512), BF16),
                 jax.ShapeDtypeStruct((n, LANES), F32)]
    out_specs = [tok(512), tok(512), tok(512), tok(256), tok(512), tok(512), u_spec, tok(512),
                 tok(LANES)]
    return pl.pallas_call(
        functools.partial(_prep_kernel, sample, tps),
        grid=grid, in_specs=in_specs, out_specs=out_specs, out_shape=out_shape,
        scratch_shapes=scratch,
        compiler_params=pltpu.CompilerParams(dimension_semantics=("arbitrary",),
                                             vmem_limit_bytes=VMEM_LIMIT),
        name="prep_sample" if sample else "prep_prompt",
    )(*args)


def _compress(load_xs, n_sub, w1_ref, pe_ref, w2_ref, kg_ref, e64k):
    outs = []
    for kv in range(2):
        acc = [None, None]
        for s in range(0, CMP_STRIDE, 2):
            xa, xb = load_xs(kv, s), load_xs(kv, s + 1)
            for r in range(CMP_RATIO):
                idx = (kv * CMP_RATIO + r) * CMP_STRIDE + s
                lhs = jnp.concatenate([(xa + pe_ref[idx:idx + 1, :]).astype(BF16),
                                       (xb + pe_ref[idx + 1:idx + 2, :]).astype(BF16)], axis=1)
                d = _dot(lhs, w1_ref[idx // 2])
                acc[r] = d if acc[r] is None else acc[r] + d
        h = acc[0] + pltpu.roll(acc[1], n_sub - 1, 0)
        o = _dot(jax.nn.silu(h).astype(BF16), w2_ref[kv])
        outs.append(o)
    kc = _group_rms(outs[0], e64k, HEAD_DIM, kg_ref[...])
    return kc, outs[1]


def _compress_prompt_kernel(xk_ref, xv_ref, w1_ref, pe_ref, w2_ref, kg_ref, e64_ref,
                            kc_out, vc_out):
    n_sub = kc_out.shape[1]
    x_refs = (xk_ref, xv_ref)
    kc, vc = _compress(lambda kv, s: x_refs[kv][0, pl.ds(s, n_sub, stride=CMP_STRIDE), :],
                       n_sub, w1_ref, pe_ref, w2_ref, kg_ref, e64_ref[...])
    kc_out[0] = kc.astype(BF16)
    vc_out[0] = vc.astype(BF16)


def _compress_prompt_call(rows3, pw):
    b, t, _ = rows3.shape
    n_sub = t // CMP_STRIDE
    out = jax.ShapeDtypeStruct((b, n_sub, LANES), BF16)
    return pl.pallas_call(
        _compress_prompt_kernel,
        grid=(b,),
        in_specs=[pl.BlockSpec((1, t, LANES), lambda i: (i, 0, 0)),
                  pl.BlockSpec((1, t, LANES), lambda i: (i, 0, 1)),
                  _const_spec((CMP_RATIO * CMP_STRIDE, 2 * LANES, 2 * CMP_HIDDEN)),
                  _const_spec((2 * CMP_RATIO * CMP_STRIDE, LANES)),
                  _const_spec((2, 2 * CMP_HIDDEN, LANES)),
                  _const_spec((1, LANES)), _const_spec((LANES, LANES))],
        out_specs=[pl.BlockSpec((1, n_sub, LANES), lambda i: (i, 0, 0))] * 2,
        out_shape=[out, out],
        compiler_params=pltpu.CompilerParams(dimension_semantics=("arbitrary",),
                                             vmem_limit_bytes=VMEM_LIMIT),
        name="compress_prompt",
    )(rows3, rows3, pw['cmp_w1'], pw['cmp_pe'], pw['cmp_w2'], pw['kc_gain'], pw['e64k'])


def _flash_init(m_ref, l_ref, acc_ref):
    m_ref[...] = jnp.full(m_ref.shape, M_INIT, F32)
    l_ref[...] = jnp.zeros(l_ref.shape, F32)
    acc_ref[...] = jnp.zeros(acc_ref.shape, F32)


def _flash_step(s, v, m_ref, l_ref, acc_ref):
    m_prev = m_ref[...]
    m_next = jnp.maximum(m_prev, jnp.max(s, axis=1, keepdims=True))
    p = jnp.exp(s - jnp.concatenate([m_next] * (s.shape[1] // LANES), axis=1))
    alpha = jnp.exp(m_prev - m_next)
    l_ref[...] = alpha * l_ref[...] + jnp.sum(p, axis=1, keepdims=True)
    acc_ref[...] = alpha * acc_ref[...] + _dot(p.astype(BF16), v)
    m_ref[...] = m_next


def _take_top(score_t, n_take):
    nb = score_t.shape[0]
    jidx = lax.broadcasted_iota(jnp.int32, score_t.shape, 0).astype(F32)
    taken = jnp.zeros(score_t.shape, F32)
    s = score_t
    for _ in range(n_take):
        mx = jnp.max(s, axis=0, keepdims=True)
        first = jnp.min(jnp.where(s == mx, jidx, float(nb)), axis=0, keepdims=True)
        pick = jidx == first
        taken = jnp.where(pick, 1.0, taken)
        s = jnp.where(pick, -jnp.inf, s)
    return taken


def _nsa_prompt_kernel(q_ref, qr_ref, kvb_ref, kc_ref, vc_ref, ng_ref, ovt_ref, cm_ref, gx_ref,
                       tri_ref, o_ref, m_ref, l_ref, acc_ref, ob_ref, qa_ref):
    tq = q_ref.shape[1]
    n_cmp = kc_ref.shape[1]
    n_blk = kvb_ref.shape[1] // SLC_BLOCK
    i = pl.program_id(1)
    q0 = i * tq
    lane = lax.broadcasted_iota(jnp.int32, (tq, LANES), 1)
    low = lane < HEAD_DIM

    def stack(qb, g):
        keep = low if g == 0 else jnp.logical_not(low)
        parts = [jnp.where(keep, qb[:, LANES * j:LANES * (j + 1)], jnp.zeros((tq, LANES), BF16))
                 for j in range(GROUP)]
        return jnp.concatenate(parts, axis=0)

    q = q_ref[0]
    qr = qr_ref[0]
    kc = kc_ref[0]
    vc = vc_ref[0]

    rowc = lax.broadcasted_iota(jnp.int32, (tq, n_cmp), 0)
    colc = lax.broadcasted_iota(jnp.int32, (tq, n_cmp), 1)
    vis1 = jnp.where(colc * CMP_STRIDE + (CMP_LEN - 1) <= q0 + rowc, 1.0, 0.0)
    vis = jnp.concatenate([vis1] * GROUP, axis=0) > 0.5

    for g in range(N_KV):
        s = _dot_nt(stack(q, g), kc)
        sm = jnp.where(vis, s, NEG)
        e = jnp.where(vis, jnp.exp(sm - jnp.max(sm, axis=1, keepdims=True)), 0.0)
        lsum = jnp.sum(e, axis=1, keepdims=True)
        pr = e / jnp.where(lsum > 0.0, lsum, 1.0)
        ob_ref[g] = _dot(pr.astype(BF16), vc)
        imp = pr[0:tq]
        for hh in range(1, GROUP):
            imp = imp + pr[hh * tq:(hh + 1) * tq]

        ih, il = _split_bf16(imp)
        sc_t = (_dot_nt(ovt_ref[...], ih) + _dot_nt(ovt_ref[...], il))[0:n_blk]
        blk_t = lax.broadcasted_iota(jnp.int32, sc_t.shape, 0)
        tpos_t = q0 + lax.broadcasted_iota(jnp.int32, sc_t.shape, 1)
        qblk_t = tpos_t // SLC_BLOCK
        forced_t = (blk_t == 0) | (blk_t == qblk_t) | (blk_t == qblk_t - 1)
        elig_t = blk_t * SLC_BLOCK <= tpos_t
        rest_t = _take_top(jnp.where(elig_t & jnp.logical_not(forced_t), sc_t, -jnp.inf),
                           N_SELECT - 3)
        keep_t = elig_t & (forced_t | (rest_t > 0.5))
        drop_t = jnp.where(keep_t, 0.0, 1.0)
        if n_blk < LANES:
            drop_t = jnp.concatenate([drop_t, jnp.zeros((LANES - n_blk, tq), F32)], axis=0)
        drop = jnp.transpose(drop_t).astype(BF16)
        qa_ref[g, :, 0:LANES] = stack(qr, g)
        qa_ref[g, :, LANES:2 * LANES] = jnp.concatenate([drop] * GROUP, axis=0)

    def flash(g):
        return m_ref.at[g], l_ref.at[g], acc_ref.at[g]

    def tile4(bias):
        return jnp.concatenate([bias] * GROUP, axis=0)

    for g in range(N_KV):
        _flash_init(*flash(g))

    def slc_tile(k0, diagonal):
        ka = jnp.concatenate([kvb_ref[0, pl.ds(k0, tq), 0:LANES], cm_ref[pl.ds(k0, tq), :]],
                             axis=1)
        v = kvb_ref[0, pl.ds(k0, tq), LANES:2 * LANES]
        for g in range(N_KV):
            st = _dot_nt(qa_ref[g], ka)
            if diagonal:
                st = st + tile4(tri_ref[0])
            _flash_step(st, v, *flash(g))

    def slc_body(kk, carry):
        slc_tile(pl.multiple_of(kk * tq, tq), False)
        return carry

    lax.fori_loop(0, i, slc_body, 0)
    slc_tile(pl.multiple_of(q0, tq), True)
    for g in range(N_KV):
        ob_ref[N_KV + g] = acc_ref[g] / l_ref[g]

    for g in range(N_KV):
        _flash_init(*flash(g))

    def win_tile(k0, bias):
        k = kvb_ref[0, pl.ds(k0, tq), 2 * LANES:3 * LANES]
        v = kvb_ref[0, pl.ds(k0, tq), 3 * LANES:4 * LANES]
        for g in range(N_KV):
            st = _dot_nt(qa_ref[g, :, 0:LANES], k)
            if bias is not None:
                st = st + tile4(tri_ref[bias])
            _flash_step(st, v, *flash(g))

    win_tile(pl.multiple_of(q0, tq), 0)
    n_back = WINDOW // tq
    for back in range(1, n_back + 1):
        @pl.when(i >= back)
        def _():
            win_tile(pl.multiple_of(q0 - back * tq, tq), 1 if back == n_back else None)
    for g in range(N_KV):
        ob_ref[2 * N_KV + g] = acc_ref[g] / l_ref[g]

    gh, gl = _split_bf16(ng_ref[0])
    for j in range(GROUP):
        tot = None
        for br in range(3):
            gx = gx_ref[:, (br * GROUP + j) * LANES:(br * GROUP + j + 1) * LANES]
            gate = _dot(gh, gx) + _dot(gl, gx)
            o_lo = ob_ref[br * N_KV + 0, j * tq:(j + 1) * tq, :]
            o_hi = ob_ref[br * N_KV + 1, j * tq:(j + 1) * tq, :]
            term = gate * jnp.where(low, o_lo, o_hi)
            tot = term if tot is None else tot + term
        o_ref[0, :, j * LANES:(j + 1) * LANES] = tot.astype(BF16)


def _nsa_prompt_call(q3, qr3, kvb3, kc, vc, ng3, pw):
    b, t, _ = q3.shape
    tq = min(TQ, t)
    n_cmp = kc.shape[1]
    blk = lambda w: pl.BlockSpec((1, tq, w), lambda bi, i: (bi, i, 0))
    per_b = lambda r, w: pl.BlockSpec((1, r, w), lambda bi, i: (bi, 0, 0))
    return pl.pallas_call(
        _nsa_prompt_kernel,
        grid=(b, t // tq),
        in_specs=[blk(512), blk(512), per_b(t, 512), per_b(n_cmp, LANES), per_b(n_cmp, LANES),
                  blk(LANES), _const_spec((LANES, n_cmp)), _const_spec((t, LANES)),
                  _const_spec((LANES, 3 * GROUP * LANES)), _const_spec((2, tq, tq))],
        out_specs=blk(512),
        out_shape=jax.ShapeDtypeStruct((b, t, 512), BF16),
        scratch_shapes=[pltpu.VMEM((N_KV, GROUP * tq, LANES), F32)] * 3
        + [pltpu.VMEM((3 * N_KV, GROUP * tq, LANES), F32),
           pltpu.VMEM((N_KV, GROUP * tq, 2 * LANES), BF16)],
        compiler_params=pltpu.CompilerParams(dimension_semantics=("arbitrary", "arbitrary"),
                                             vmem_limit_bytes=VMEM_LIMIT),
        name="nsa_prompt",
    )(q3, qr3, kvb3, kc, vc, ng3, pw['ov_p'], pw['cm_p'], pw['gate_x'],
      jnp.asarray(_band_bias(tq), F32))


def _memkv_kernel(mem_ref, nm_ref, w_ref, kg_ref, kv_out):
    x = mem_ref[0]
    ms = jnp.sum(x * x, axis=-1, keepdims=True) * (1.0 / D_MODEL)
    xn = (x * lax.rsqrt(ms + EPS) * nm_ref[...]).astype(BF16)
    kv = _dot(xn, w_ref[...])
    for h in range(MEM_HEADS):
        k = kv[:, h * LANES:(h + 1) * LANES]
        kms = jnp.sum(k * k, axis=-1, keepdims=True) * (1.0 / MEM_HEAD_DIM)
        kv_out[0, :, h * LANES:(h + 1) * LANES] = k * lax.rsqrt(kms + EPS) * kg_ref[...]
    kv_out[0, :, BRANCH_W:2 * BRANCH_W] = kv[:, BRANCH_W:2 * BRANCH_W]


def _memkv_call(mem, pw):
    b, m, _ = mem.shape
    return pl.pallas_call(
        _memkv_kernel,
        grid=(b,),
        in_specs=[pl.BlockSpec((1, m, D_MODEL), lambda i: (i, 0, 0)), _const_spec((1, D_MODEL)),
                  _const_spec((D_MODEL, 2 * BRANCH_W)), _const_spec((1, LANES))],
        out_specs=pl.BlockSpec((1, m, 2 * BRANCH_W), lambda i: (i, 0, 0)),
        out_shape=jax.ShapeDtypeStruct((b, m, 2 * BRANCH_W), F32),
        compiler_params=pltpu.CompilerParams(dimension_semantics=("arbitrary",),
                                             vmem_limit_bytes=VMEM_LIMIT),
        name="mem_kv",
    )(mem, pw['nmem'], pw['w_mem'], pw['mk_gain'])


def _mem_attend(mq, kv_ref):
    outs = []
    for h in range(MEM_HEADS):
        k = kv_ref[:, h * LANES:(h + 1) * LANES].astype(BF16)
        v = kv_ref[:, BRANCH_W + h * LANES:BRANCH_W + (h + 1) * LANES].astype(BF16)
        s = _dot_nt(mq[:, h * LANES:(h + 1) * LANES], k) * MEM_HEAD_DIM ** -0.5
        e = jnp.exp(s - jnp.max(s, axis=1, keepdims=True))
        pr = e / jnp.sum(e, axis=1, keepdims=True)
        outs.append(_dot(pr.astype(BF16), v))
    return jnp.concatenate(outs, axis=1)


def _mem_attn_kernel(mq_ref, kv_ref, o_ref):
    o_ref[0] = _mem_attend(mq_ref[0], kv_ref.at[0]).astype(BF16)


def _mem_attn_call(mq3, mkv):
    b, t, _ = mq3.shape
    tq = min(TM, t)
    return pl.pallas_call(
        _mem_attn_kernel,
        grid=(b, t // tq),
        in_specs=[pl.BlockSpec((1, tq, 512), lambda bi, i: (bi, i, 0)),
                  pl.BlockSpec((1, MEM_LEN, 2 * BRANCH_W), lambda bi, i: (bi, 0, 0))],
        out_specs=pl.BlockSpec((1, tq, 512), lambda bi, i: (bi, i, 0)),
        out_shape=jax.ShapeDtypeStruct((b, t, 512), BF16),
        compiler_params=pltpu.CompilerParams(dimension_semantics=("arbitrary", "arbitrary"),
                                             vmem_limit_bytes=VMEM_LIMIT),
        name="mem_attn",
    )(mq3, mkv)


def _finish_kernel(x_ref, on_ref, z_ref, om_ref, nmix_ref, wmg_ref, wbr_ref, wout_ref, h_out):
    x = x_ref[...]
    ms = jnp.sum(x * x, axis=-1, keepdims=True) * (1.0 / D_MODEL)
    xn = (x * lax.rsqrt(ms + EPS) * nmix_ref[...]).astype(BF16)
    merged = None
    for n, br_ref in enumerate((on_ref, z_ref, om_ref)):
        gate = jax.nn.sigmoid(_dot(xn, wmg_ref[:, n * D_MODEL:(n + 1) * D_MODEL]))
        term = gate * _dot(br_ref[...], wbr_ref[n])
        merged = term if merged is None else merged + term
    h_out[...] = x + _dot(merged.astype(BF16), wout_ref[...])


def _finish_call(x, o_nsa, z, o_mem, pw):
    n = x.shape[0]
    tm = min(TM, n)
    tok = lambda w: pl.BlockSpec((tm, w), lambda i: (i, 0))
    return pl.pallas_call(
        _finish_kernel,
        grid=(n // tm,),
        in_specs=[tok(D_MODEL), tok(512), tok(512), tok(512), _const_spec((1, D_MODEL)),
                  _const_spec((D_MODEL, N_BRANCH * D_MODEL)),
                  _const_spec((N_BRANCH, BRANCH_W, D_MODEL)), _const_spec((D_MODEL, D_MODEL))],
        out_specs=tok(D_MODEL),
        out_shape=jax.ShapeDtypeStruct((n, D_MODEL), F32),
        compiler_params=pltpu.CompilerParams(dimension_semantics=("arbitrary",),
                                             vmem_limit_bytes=VMEM_LIMIT),
        name="finish",
    )(x, o_nsa, z, o_mem, pw['nmix'], pw['w_mg'], pw['w_br'], pw['w_out'])


def _ffn_kernel(h_ref, nf_ref, wgu_ref, wd_ref, y_out):
    h = h_ref[...]
    ms = jnp.sum(h * h, axis=-1, keepdims=True) * (1.0 / D_MODEL)
    hn = (h * lax.rsqrt(ms + EPS) * nf_ref[...]).astype(BF16)
    acc = h
    for c in range(D_FF // FF_CHUNK):
        g = _dot(hn, wgu_ref[:, c * FF_CHUNK:(c + 1) * FF_CHUNK])
        u = _dot(hn, wgu_ref[:, D_FF + c * FF_CHUNK:D_FF + (c + 1) * FF_CHUNK])
        a = (jax.nn.silu(g) * u).astype(BF16)
        acc = acc + _dot(a, wd_ref[c * FF_CHUNK:(c + 1) * FF_CHUNK, :])
    y_out[...] = acc


def _ffn_call(h, pw):
    n = h.shape[0]
    tm = min(TM, n)
    tok = pl.BlockSpec((tm, D_MODEL), lambda i: (i, 0))
    return pl.pallas_call(
        _ffn_kernel,
        grid=(n // tm,),
        in_specs=[tok, _const_spec((1, D_MODEL)), _const_spec((D_MODEL, 2 * D_FF)),
                  _const_spec((D_FF, D_MODEL))],
        out_specs=tok,
        out_shape=jax.ShapeDtypeStruct((n, D_MODEL), F32),
        compiler_params=pltpu.CompilerParams(dimension_semantics=("arbitrary",),
                                             vmem_limit_bytes=VMEM_LIMIT),
        name="ffn",
    )(h, pw['nffn'], pw['w_gu'], pw['w_down'])


def _nsa_sample_kernel(layer_base, pt_ref, pool_ref, q_ref, qr_ref, rows_ref, wnew_ref,
                       wbuf_ref, mq_ref, mkv_ref, ng_ref, w1_ref, pe_ref, w2_ref, kg_ref, e64_ref,
                       ov_ref, gx_ref,
                       o_ref, wout_ref, om_ref,
                       tbuf_ref, kraw_ref, vraw_ref, gk_ref, gv_ref, idx_vm, idx_sm, cmp_sem,
                       gat_sem, idx_sem):
    b = pl.program_id(0)
    nb = pl.num_programs(0)
    n_pages = pt_ref.shape[1]
    n_past = n_pages * PAGE_SIZE
    n_sub = n_past // CMP_STRIDE
    n_gather = gk_ref.shape[1]
    q_blk = n_past // SLC_BLOCK

    def cmp_copies(bb, slot, j):
        page = layer_base + pt_ref[bb, j]
        return [pltpu.make_async_copy(pool_ref.at[page, kv], tbuf_ref.at[slot, kv, j],
                                      cmp_sem.at[slot]) for kv in range(2)]

    def start_cmp(bb, slot):
        def body(j, c):
            for cp in cmp_copies(bb, slot, j):
                cp.start()
            return c
        lax.fori_loop(0, n_pages, body, 0)

    def wait_cmp(bb, slot):
        def body(j, c):
            for cp in cmp_copies(bb, slot, j):
                cp.wait()
            return c
        lax.fori_loop(0, n_pages, body, 0)

    slot = b % 2

    @pl.when(b == 0)
    def _():
        start_cmp(b, slot)

    @pl.when(b + 1 < nb)
    def _():
        start_cmp(b + 1, 1 - slot)

    wait_cmp(b, slot)

    row8 = lax.broadcasted_iota(jnp.int32, (8, LANES), 0)
    lane8 = lax.broadcasted_iota(jnp.int32, (8, LANES), 1)
    low = lane8 < HEAD_DIM
    own_half = (row8 < GROUP) == low

    def stack8(qrow):
        out = jnp.zeros((8, LANES), F32)
        for j in range(GROUP):
            pj = jnp.broadcast_to(qrow[:, j * LANES:(j + 1) * LANES], (8, LANES))
            out = jnp.where((row8 & (GROUP - 1)) == j, pj, out)
        return jnp.where(own_half, out, 0.0).astype(BF16)

    q8 = stack8(q_ref[0])
    qr8 = stack8(qr_ref[0])

    raw_refs = (kraw_ref, vraw_ref)

    def to_rows(j, c):
        dst = pl.ds(pl.multiple_of(j * PAGE_SIZE, PAGE_SIZE), PAGE_SIZE)
        for kv in range(2):
            raw_refs[kv][dst, :] = jnp.concatenate(
                [jnp.transpose(tbuf_ref[slot, kv, j, g]) for g in range(N_KV)], axis=1)
        return c
    lax.fori_loop(0, n_pages, to_rows, 0, unroll=8)

    kc, vc = _compress(
        lambda kv, s: raw_refs[kv][pl.ds(s, n_sub, stride=CMP_STRIDE), :],
        n_sub, w1_ref, pe_ref, w2_ref, kg_ref, e64_ref[...])
    s = _dot_nt(q8, kc.astype(BF16))
    col = lax.broadcasted_iota(jnp.int32, s.shape, 1)
    rowc = lax.broadcasted_iota(jnp.int32, s.shape, 0)
    vis = col < n_sub - CMP_RATIO + 1
    sm = jnp.where(vis, s, NEG)
    e = jnp.where(vis, jnp.exp(sm - jnp.max(sm, axis=1, keepdims=True)), 0.0)
    pr = e / jnp.sum(e, axis=1, keepdims=True)
    o_c = _dot(pr.astype(BF16), vc.astype(BF16))
    imp8 = jnp.zeros(s.shape, F32)
    for g in range(N_KV):
        imp_g = jnp.sum(jnp.where(rowc // GROUP == g, pr, 0.0), axis=0, keepdims=True)
        imp8 = jnp.where(rowc == g, jnp.broadcast_to(imp_g, s.shape), imp8)

    n_blk_pad = ov_ref.shape[1]
    sc = _dot_exact01(imp8, ov_ref[...])
    bl = lax.broadcasted_iota(jnp.int32, sc.shape, 1)
    forced = (bl == 0) | (bl == q_blk) | (bl == q_blk - 1)
    sc = jnp.where(forced, jnp.inf, jnp.where(bl <= q_blk, sc, -jnp.inf))
    sc_t = jnp.transpose(sc)
    blf = bl[0:1].astype(F32)
    idx8 = jnp.zeros((8, LANES), F32)
    for g in range(N_KV):
        colv = jnp.broadcast_to(sc_t[:, g:g + 1], (n_blk_pad, n_blk_pad))
        rowv = jnp.broadcast_to(sc[g:g + 1, :], (n_blk_pad, n_blk_pad))
        ii = lax.broadcasted_iota(jnp.int32, colv.shape, 0)
        jj = lax.broadcasted_iota(jnp.int32, colv.shape, 1)
        tie = jnp.where(ii < jj, 1.0, 0.0)
        beats = jnp.where(colv > rowv, 1.0, jnp.where(colv == rowv, tie, 0.0))
        rank = jnp.sum(beats, axis=0, keepdims=True)
        for k in range(N_SELECT):
            blk_k = jnp.sum(jnp.where(rank == float(k), blf, 0.0), axis=1, keepdims=True)
            idx8 = jnp.where((row8 == g) & (lane8 == k), blk_k, idx8)
    idx_vm[...] = idx8.astype(jnp.int32)
    idx_copy = pltpu.make_async_copy(idx_vm, idx_sm, idx_sem.at[0])
    idx_copy.start()

    wb = wbuf_ref[0]
    wrow = lax.broadcasted_iota(jnp.int32, wb.shape, 0)
    w_upd = jnp.where(wrow == wb.shape[0] - 1, wnew_ref[0], pltpu.roll(wb, wb.shape[0] - 1, 0))
    wout_ref[0] = w_upd
    s_w = _dot_nt(qr8, w_upd[:, 0:LANES].astype(BF16))
    e_w = jnp.exp(s_w - jnp.max(s_w, axis=1, keepdims=True))
    pr_w = e_w / jnp.sum(e_w, axis=1, keepdims=True)
    o_w = _dot(pr_w.astype(BF16), w_upd[:, LANES:2 * LANES].astype(BF16))

    idx_copy.wait()
    past_ranks = [k for k in range(N_SELECT) if k != 2]
    gathers, halves = [], [[], []]
    for g in range(N_KV):
        for kk, k in enumerate(past_ranks):
            blk = jnp.minimum(idx_sm[g, k], q_blk - 1)
            page = layer_base + pt_ref[b, lax.shift_right_logical(blk, 1)]
            halves[g].append(blk & 1)
            gathers.append(pltpu.make_async_copy(pool_ref.at[page, 2, g], gk_ref.at[g, kk],
                                                 gat_sem.at[0]))
            gathers.append(pltpu.make_async_copy(pool_ref.at[page, 3, g], gv_ref.at[g, kk],
                                                 gat_sem.at[0]))
    for cp in gathers:
        cp.start()

    mq8 = jnp.broadcast_to(mq_ref[0], (8, BRANCH_W)).astype(BF16)
    om_ref[0] = _mem_attend(mq8, mkv_ref.at[0])

    for cp in gathers:
        cp.wait()
    rows = rows_ref[0]
    k_new = rows[:, 2 * LANES:3 * LANES].astype(BF16).astype(F32)
    v_new = rows[:, 3 * LANES:4 * LANES].astype(BF16).astype(F32)
    s_new = jnp.sum(qr8.astype(F32) * k_new, axis=1, keepdims=True)
    lane_blk = lane8 // SLC_BLOCK
    o_parts, p_new = [], []
    for g in range(N_KV):
        q_g = qr8[:, g * HEAD_DIM:(g + 1) * HEAD_DIM]
        ss = []
        for kk in range(n_gather):
            st = _dot(q_g, gk_ref[g, kk].astype(BF16))
            ss.append(jnp.where(lane_blk == halves[g][kk], st, NEG))
        s_past = jnp.concatenate(ss, axis=1)
        m = jnp.maximum(jnp.max(s_past, axis=1, keepdims=True), s_new)
        e_past = jnp.exp(s_past - m)
        e_new = jnp.exp(s_new - m)
        den = jnp.sum(e_past, axis=1, keepdims=True) + e_new
        pr_past = (e_past / den).astype(BF16)
        o_g = None
        for kk in range(n_gather):
            d = _dot_nt(pr_past[:, kk * LANES:(kk + 1) * LANES], gv_ref[g, kk].astype(BF16))
            o_g = d if o_g is None else o_g + d
        o_parts.append(o_g)
        p_new.append((e_new / den).astype(BF16).astype(F32))
    pr_new = jnp.where(row8[:, 0:1] < GROUP, p_new[0], p_new[1])
    o_s = jnp.concatenate(o_parts, axis=1) + pr_new * v_new

    gh, gl = _split_bf16(jnp.broadcast_to(ng_ref[0], (8, LANES)))
    tot = jnp.zeros((8, LANES), F32)
    for br, o8 in enumerate((o_c, o_s, o_w)):
        pair = jnp.where(low, o8, pltpu.roll(o8, GROUP, 0))
        gate8 = jnp.zeros((8, LANES), F32)
        for j in range(GROUP):
            gx = gx_ref[:, (br * GROUP + j) * LANES:(br * GROUP + j + 1) * LANES]
            gate8 = jnp.where(row8 == j, _dot(gh, gx) + _dot(gl, gx), gate8)
        tot = tot + gate8 * pair
    o_ref[0] = tot


def _nsa_sample_call(layer, page_table, pool5, q4, qr4, rows_s, win_new, win_buf, mq4, mkv,
                     ng_s, pw):
    nb, n_pages = page_table.shape
    n_past = n_pages * PAGE_SIZE
    n_sub = n_past // CMP_STRIDE
    n_phys = pool5.shape[0] // DEPTH
    w_buf = win_buf.shape[1]
    n_blk_pad = pw['ov_s'].shape[1]
    per_b = lambda *shape: pl.BlockSpec((1,) + shape, lambda b, pt: (b,) + (0,) * len(shape))
    grid_spec = pltpu.PrefetchScalarGridSpec(
        num_scalar_prefetch=1,
        grid=(nb,),
        in_specs=[pl.BlockSpec(memory_space=pl.ANY),
                  per_b(1, 512), per_b(1, 512), per_b(1, 512), per_b(1, 256),
                  per_b(w_buf, 256), per_b(1, 512), per_b(MEM_LEN, 2 * BRANCH_W),
                  per_b(1, LANES),
                  _const_spec((CMP_RATIO * CMP_STRIDE, 2 * LANES, 2 * CMP_HIDDEN)),
                  _const_spec((2 * CMP_RATIO * CMP_STRIDE, LANES)),
                  _const_spec((2, 2 * CMP_HIDDEN, LANES)),
                  _const_spec((1, LANES)), _const_spec((LANES, LANES)),
                  _const_spec((n_sub, n_blk_pad)),
                  _const_spec((LANES, 3 * GROUP * LANES))],
        out_specs=[per_b(8, LANES), per_b(w_buf, 256), per_b(8, BRANCH_W)],
        scratch_shapes=[
            pltpu.VMEM((2, 2, n_pages, N_KV, HEAD_DIM, PAGE_SIZE), F32),
            pltpu.VMEM((n_past, LANES), F32), pltpu.VMEM((n_past, LANES), F32),
            pltpu.VMEM((N_KV, N_SELECT - 1, HEAD_DIM, PAGE_SIZE), F32),
            pltpu.VMEM((N_KV, N_SELECT - 1, HEAD_DIM, PAGE_SIZE), F32),
            pltpu.VMEM((8, LANES), jnp.int32), pltpu.SMEM((8, LANES), jnp.int32),
            pltpu.SemaphoreType.DMA((2,)), pltpu.SemaphoreType.DMA((1,)),
            pltpu.SemaphoreType.DMA((1,))],
    )
    return pl.pallas_call(
        functools.partial(_nsa_sample_kernel, layer * n_phys),
        grid_spec=grid_spec,
        out_shape=[jax.ShapeDtypeStruct((nb, 8, LANES), F32),
                   jax.ShapeDtypeStruct((nb, w_buf, 256), F32),
                   jax.ShapeDtypeStruct((nb, 8, BRANCH_W), F32)],
        compiler_params=pltpu.CompilerParams(dimension_semantics=("arbitrary",),
                                             vmem_limit_bytes=VMEM_LIMIT),
        name="nsa_sample",
    )(page_table, pool5, q4, qr4, rows_s, win_new, win_buf, mq4, mkv, ng_s,
      pw['cmp_w1'], pw['cmp_pe'], pw['cmp_w2'], pw['kc_gain'], pw['e64k'], pw['ov_s'],
      pw['gate_x'])


def _pair_perm():
    p = np.arange(BRANCH_W)
    j, ln = p // LANES, p % LANES
    head = np.where(ln < HEAD_DIM, j, GROUP + j)
    return head * HEAD_DIM + ln % HEAD_DIM


def _block_diag_ones(width, group):
    i = np.arange(width)
    return (i[:, None] // group == i[None, :] // group).astype(np.float32)


def _overlap(n_cmp_rows, n_cmp_valid, n_blk_cols):
    i = np.arange(n_cmp_rows)[:, None]
    j = np.arange(n_blk_cols)[None, :]
    ov = (i * CMP_STRIDE < (j + 1) * SLC_BLOCK) & (i * CMP_STRIDE + CMP_LEN > j * SLC_BLOCK)
    return (ov & (i < n_cmp_valid)).astype(np.float32)


def _block_mask_rows(n_keys):
    k = np.arange(n_keys)[:, None]
    j = np.arange(LANES)[None, :]
    return np.where(k // SLC_BLOCK == j, NEG, 0.0).astype(np.float32)


def _band_bias(tq):
    r = np.arange(tq)[:, None]
    c = np.arange(tq)[None, :]
    return np.stack([np.where(c <= r, 0.0, NEG), np.where(c > r, 0.0, NEG)]).astype(np.float32)


def _gate_expand():
    x = np.zeros((LANES, 3 * GROUP * LANES), np.float32)
    for br in range(3):
        for j in range(GROUP):
            for ln in range(LANES):
                head = j if ln < HEAD_DIM else GROUP + j
                x[br * N_HEADS + head, (br * GROUP + j) * LANES + ln] = 1.0
    return x


def _rope_tables(pos):
    half = ROT_DIM // 2
    inv = ROPE_THETA ** (-jnp.arange(half, dtype=F32) / half)
    ang = pos.astype(F32)[:, None] * inv[None, :]
    cos, sin = jnp.cos(ang), jnp.sin(ang)
    n = pos.shape[0]
    one = jnp.ones((n, HEAD_DIM - ROT_DIM), F32)
    zero = jnp.zeros((n, HEAD_DIM - ROT_DIM), F32)
    zh = jnp.zeros((n, half), F32)
    c = jnp.concatenate([cos, cos, one], axis=1)
    s1 = jnp.concatenate([zh, sin, zero], axis=1)
    s2 = jnp.concatenate([-sin, zh, zero], axis=1)
    tab = jnp.stack([c, s1, s2])
    return jnp.concatenate([tab, tab], axis=2)


def _layer_params(l, seq_len, n_past, norm_mix, w_in, q_norm, k_norm, cmp_pe, cmp_w1, cmp_w2,
                  conv_w, norm_mem, w_mem_kv, mem_q_norm, mem_k_norm, w_branch, w_out, norm_ffn,
                  w_gate_up, w_down):
    perm = _pair_perm()
    wi = w_in[l]
    o_kv, o_ng = BRANCH_W, BRANCH_W + 768
    o_cx = o_ng + 3 * N_HEADS
    o_cb, o_cc, o_mq, o_mg = o_cx + 512, o_cx + 1024, o_cx + 1536, o_cx + 2048
    w_a = jnp.concatenate([
        wi[:, perm], wi[:, o_kv:o_ng], wi[:, o_cx:o_cb], wi[:, o_cb:o_cc], wi[:, o_cc:o_mq],
        wi[:, o_mq:o_mg], wi[:, o_ng:o_cx], jnp.zeros((D_MODEL, LANES - 3 * N_HEADS), F32)],
        axis=1).astype(BF16)
    w1 = cmp_w1[l].reshape(2, CMP_RATIO, CMP_STRIDE, HEAD_DIM, CMP_HIDDEN)
    zw = jnp.zeros_like(w1)
    w1bd = jnp.concatenate([jnp.concatenate([w1, zw], axis=-1),
                            jnp.concatenate([zw, w1], axis=-1)], axis=-2)
    w2 = cmp_w2[l]
    z2 = jnp.zeros_like(w2)
    w2bd = jnp.concatenate([jnp.concatenate([w2, z2], axis=-1),
                            jnp.concatenate([z2, w2], axis=-1)], axis=-2)
    pe = cmp_pe[l].reshape(2 * CMP_RATIO * CMP_STRIDE, HEAD_DIM)
    n_sub_p = seq_len // CMP_STRIDE
    n_sub_s = n_past // CMP_STRIDE
    n_blk_s = -(-(n_past // SLC_BLOCK + 1) // LANES) * LANES
    wbr = w_branch[l]
    return dict(
        nmix=norm_mix[l][None, :], w_a=w_a, w_mg=wi[:, o_mg:].astype(BF16),
        q_gain=jnp.tile(q_norm[l], N_HEADS)[None, :],
        k_gain=jnp.stack([jnp.tile(k_norm[l, 1], N_KV), jnp.tile(k_norm[l, 2], N_KV)]),
        kc_gain=jnp.tile(k_norm[l, 0], N_KV)[None, :],
        mq_gain=jnp.tile(mem_q_norm[l], MEM_HEADS)[None, :],
        mk_gain=mem_k_norm[l][None, :],
        conv_w=conv_w[l],
        cmp_w1=w1bd.reshape(CMP_RATIO * CMP_STRIDE, 2 * LANES, 2 * CMP_HIDDEN).astype(BF16),
        cmp_pe=jnp.concatenate([pe, pe], axis=1),
        cmp_w2=w2bd.astype(BF16),
        nmem=norm_mem[l][None, :], w_mem=w_mem_kv[l].astype(BF16),
        w_br=jnp.concatenate([wbr[0][perm][None], wbr[1:]], axis=0).astype(BF16),
        w_out=w_out[l].astype(BF16),
        nffn=norm_ffn[l][None, :], w_gu=w_gate_up[l].astype(BF16), w_down=w_down[l].astype(BF16),
        e64=jnp.asarray(_block_diag_ones(512, HEAD_DIM), BF16),
        e64k=jnp.asarray(_block_diag_ones(LANES, HEAD_DIM), BF16),
        e128=jnp.asarray(_block_diag_ones(512, MEM_HEAD_DIM), BF16),
        ov_p=jnp.asarray(_overlap(n_sub_p, n_sub_p - CMP_RATIO + 1, LANES).T, BF16),
        cm_p=jnp.asarray(_block_mask_rows(seq_len), BF16),
        ov_s=jnp.asarray(_overlap(n_sub_s, n_sub_s - CMP_RATIO + 1, n_blk_s), BF16),
        gate_x=jnp.asarray(_gate_expand(), BF16),
    )


def kernel(x_prompt, x_sample, cache_nsa_kv, cache_win_kv, state_conv, cache_mem_kv, page_table,
           mem_prompt, norm_mix, w_in, q_norm, k_norm, cmp_pe, cmp_w1, cmp_w2, conv_w, norm_mem,
           w_mem_kv, mem_q_norm, mem_k_norm, w_branch, w_out, norm_ffn, w_gate_up, w_down):
    bp, t, _ = x_prompt.shape
    bs, ts, _ = x_sample.shape
    assert ts == 1 and t % TM == 0 and t % TQ == 0
    n_pages = page_table.shape[1]
    n_past = n_pages * PAGE_SIZE
    w_buf = cache_win_kv.shape[2]
    assert w_buf == WINDOW and n_past >= WINDOW and (n_past // SLC_BLOCK) == LANES

    xp = x_prompt.reshape(bp * t, D_MODEL)
    xs = x_sample.reshape(bs, D_MODEL)
    rope_p = _rope_tables(jnp.arange(t))
    rope_s = _rope_tables(jnp.full((bs,), n_past, jnp.int32))
    n_phys = cache_nsa_kv.shape[1]
    pool5 = jnp.transpose(cache_nsa_kv, (0, 1, 3, 4, 5, 2)).reshape(
        DEPTH * n_phys, 4, N_KV, HEAD_DIM, PAGE_SIZE)

    rows_p, rows_s, win_p, win_s, conv_p, conv_s, mem_p = [], [], [], [], [], [], []
    for l in range(DEPTH):
        pw = _layer_params(l, t, n_past, norm_mix, w_in, q_norm, k_norm, cmp_pe, cmp_w1, cmp_w2,
                           conv_w, norm_mem, w_mem_kv, mem_q_norm, mem_k_norm, w_branch, w_out,
                           norm_ffn, w_gate_up, w_down)
        q, qr, rows, win, kvb, z, utail, mq, ng = _prep_call(xp, pw, rope_p, False, t)
        rows3 = rows.reshape(bp, t, 512)
        kc, vc = _compress_prompt_call(rows3, pw)
        o_nsa = _nsa_prompt_call(q.reshape(bp, t, 512), qr.reshape(bp, t, 512),
                                 kvb.reshape(bp, t, 512), kc, vc, ng.reshape(bp, t, LANES), pw)
        mkv = _memkv_call(mem_prompt, pw)
        o_mem = _mem_attn_call(mq.reshape(bp, t, 512), mkv)
        h = _finish_call(xp, o_nsa.reshape(bp * t, 512), z, o_mem.reshape(bp * t, 512), pw)
        xp = _ffn_call(h, pw)
        rows_p.append(rows3.reshape(bp, t, 4, N_KV, HEAD_DIM))
        win_p.append(win.reshape(bp, t, 2, N_KV, HEAD_DIM)[:, t - min(WINDOW, t):])
        conv_p.append(utail.reshape(bp, t // TM, 8, 512)[:, -1, 8 - (CONV_W - 1):])
        mem_p.append(mkv.reshape(bp, MEM_LEN, 2, MEM_HEADS, MEM_HEAD_DIM))
        q, qr, rows, win, _, z, u, mq, ng = _prep_call(xs, pw, rope_s, True, 1,
                                                       state=state_conv[l])
        o_nsa, new_win, o_mem = _nsa_sample_call(
            l, page_table, pool5, q.astype(F32).reshape(bs, 1, 512),
            qr.astype(F32).reshape(bs, 1, 512), rows.reshape(bs, 1, 512),
            win.reshape(bs, 1, 256), cache_win_kv[l].reshape(bs, w_buf, 256),
            mq.astype(F32).reshape(bs, 1, 512),
            cache_mem_kv[l].reshape(bs, MEM_LEN, 2 * BRANCH_W), ng.reshape(bs, 1, LANES), pw)
        o_nsa = o_nsa[:, 0:GROUP, :].reshape(bs, 512).astype(BF16)
        o_mem = o_mem[:, 0, :].astype(BF16)
        h = _finish_call(xs, o_nsa, z, o_mem, pw)
        xs = _ffn_call(h, pw)
        rows_s.append(rows.reshape(bs, 1, 4, N_KV, HEAD_DIM))
        win_s.append(new_win.reshape(bs, w_buf, 2, N_KV, HEAD_DIM))
        conv_s.append(jnp.stack([state_conv[l][:, 1, :], u], axis=1))

    return (xp.reshape(bp, t, D_MODEL), xs.reshape(bs, 1, D_MODEL), jnp.stack(rows_p),
            jnp.stack(rows_s), jnp.stack(win_p), jnp.stack(win_s), jnp.stack(conv_p),
            jnp.stack(conv_s), jnp.stack(mem_p))
```

```python
import functools

import numpy as np
import jax
import jax.numpy as jnp
from jax import lax
from jax.experimental import pallas as pl
from jax.experimental.pallas import tpu as pltpu

F32 = jnp.float32
BF16 = jnp.bfloat16

D_MODEL = 1024
DEPTH = 2
PAST_LEN = 8192
PAGE_SIZE = 128
HEAD_DIM = 64
BRANCH_W = D_MODEL // 2
N_HEADS = BRANCH_W // HEAD_DIM
N_KV = N_HEADS // 4
GROUP = N_HEADS // N_KV
ROT_DIM = HEAD_DIM // 4
ROPE_THETA = 500000.0
CMP_LEN = 32
CMP_STRIDE = 16
CMP_RATIO = CMP_LEN // CMP_STRIDE
CMP_HIDDEN = 2 * HEAD_DIM
SLC_BLOCK = 64
N_SELECT = 16
WINDOW = 512
CONV_W = 3
MEM_LEN = 256
MEM_HEADS = 4
MEM_HEAD_DIM = BRANCH_W // MEM_HEADS
N_BRANCH = 3
D_FF = ((8 * D_MODEL + 767) // 768) * 256
EPS = 1e-6

LANES = 128
VMEM_LIMIT = 56 * 1024 * 1024

TM = 512
TQ = 512
FF_CHUNK = 256
NEG = -1e30
M_INIT = -3e38

_C_Q = 0
_C_KV = 512
_C_CX = 1280
_C_CB = 1792
_C_CC = 2304
_C_MQ = 2816
_C_NG = 3328
_N_A = 3456


def _const_spec(shape):
    nd = len(shape)
    return pl.BlockSpec(shape, lambda *_: (0,) * nd, pipeline_mode=pl.Buffered(1))


def _dot(a, b):
    return jnp.dot(a, b, preferred_element_type=F32)


def _dot_nt(a, b):
    return lax.dot_general(a, b, (((1,), (1,)), ((), ())), preferred_element_type=F32)


def _split_bf16(v):
    hi = v.astype(BF16)
    lo = (v - hi.astype(F32)).astype(BF16)
    return hi, lo


def _dot_exact01(v, m01):
    hi, lo = _split_bf16(v)
    return _dot(hi, m01) + _dot(lo, m01)


def _group_rms(v, e01, group, gain):
    ms = _dot_exact01(v * v, e01) * (1.0 / group)
    return v * lax.rsqrt(ms + EPS) * gain


def _rope_cols(v, c, s1, s2):
    outs = []
    for j in range(v.shape[1] // LANES):
        col = v[:, j * LANES:(j + 1) * LANES]
        outs.append(col * c + pltpu.roll(col, ROT_DIM // 2, 1) * s1
                    + pltpu.roll(col, LANES - ROT_DIM // 2, 1) * s2)
    return outs[0] if len(outs) == 1 else jnp.concatenate(outs, axis=1)


def _prep_kernel(sample, tiles_per_seq, *refs):
    if sample:
        (x_ref, nmix_ref, w_ref, qg_ref, kg_ref, mqg_ref, rope_ref, e64_ref, e128_ref, cw_ref,
         s0_ref, s1_ref,
         q_out, qr_out, rows_out, win_out, kvb_out, z_out, u_out, mq_out, ng_out) = refs
    else:
        (x_ref, nmix_ref, w_ref, qg_ref, kg_ref, mqg_ref, rope_ref, e64_ref, e128_ref, cw_ref,
         q_out, qr_out, rows_out, win_out, kvb_out, z_out, u_out, mq_out, ng_out,
         carry_ref) = refs
    tm = x_ref.shape[0]
    x = x_ref[...]
    ms = jnp.sum(x * x, axis=-1, keepdims=True) * (1.0 / D_MODEL)
    xn = (x * lax.rsqrt(ms + EPS) * nmix_ref[...]).astype(BF16)

    rc, rs1, rs2 = rope_ref[0], rope_ref[1], rope_ref[2]
    e64 = e64_ref[...]

    q = _dot(xn, w_ref[:, _C_Q:_C_Q + 512])
    qn = _group_rms(q, e64, HEAD_DIM, qg_ref[...])
    q_out[...] = (qn * HEAD_DIM ** -0.5).astype(BF16)
    qr_out[...] = (_rope_cols(qn, rc, rs1, rs2) * HEAD_DIM ** -0.5).astype(BF16)

    kv = _dot(xn, w_ref[:, _C_KV:_C_KV + 768])
    e64k = e64_ref[0:LANES, 0:LANES]
    k_slc = _rope_cols(_group_rms(kv[:, 256:384], e64k, HEAD_DIM, kg_ref[0:1, :]), rc, rs1, rs2)
    k_win = _rope_cols(_group_rms(kv[:, 512:640], e64k, HEAD_DIM, kg_ref[1:2, :]), rc, rs1, rs2)
    v_slc = kv[:, 384:512]
    v_win = kv[:, 640:768]
    rows_out[:, 0:256] = kv[:, 0:256]
    rows_out[:, 256:384] = k_slc
    rows_out[:, 384:512] = v_slc
    win_out[:, 0:128] = k_win
    win_out[:, 128:256] = v_win
    kvb_out[:, 0:128] = k_slc.astype(BF16)
    kvb_out[:, 128:256] = v_slc.astype(BF16)
    kvb_out[:, 256:384] = k_win.astype(BF16)
    kvb_out[:, 384:512] = v_win.astype(BF16)

    cx = _dot(xn, w_ref[:, _C_CX:_C_CX + 512])
    cc = _dot(xn, w_ref[:, _C_CC:_C_CC + 512])
    cb = _dot(xn, w_ref[:, _C_CB:_C_CB + 512])
    u = cc * cx
    w0, w1, w2 = cw_ref[0:1, :], cw_ref[1:2, :], cw_ref[2:3, :]
    if sample:
        z = cb * (s0_ref[...] * w0 + s1_ref[...] * w1 + u * w2)
        u_out[...] = u
    else:
        @pl.when(pl.program_id(0) % tiles_per_seq == 0)
        def _():
            carry_ref[...] = jnp.zeros_like(carry_ref)
        prev = carry_ref[...]
        row = lax.broadcasted_iota(jnp.int32, u.shape, 0)
        u1 = jnp.where(row == 0, prev[7:8, :], pltpu.roll(u, 1, 0))
        u2 = jnp.where(row == 0, prev[6:7, :],
                       jnp.where(row == 1, prev[7:8, :], pltpu.roll(u, 2, 0)))
        z = cb * (u2 * w0 + u1 * w1 + u * w2)
        tail = u[tm - 8:tm, :]
        carry_ref[...] = tail
        u_out[...] = tail
    z_out[...] = z.astype(BF16)

    mq = _dot(xn, w_ref[:, _C_MQ:_C_MQ + 512])
    mq_out[...] = _group_rms(mq, e128_ref[...], MEM_HEAD_DIM, mqg_ref[...]).astype(BF16)

    ng_out[...] = jax.nn.sigmoid(_dot(xn, w_ref[:, _C_NG:_C_NG + 128]))


def _prep_call(x, pw, rope_tab, sample, seq_len, state=None):
    n = x.shape[0]
    tm = n if sample else TM
    tps = 1 if sample else seq_len // tm
    grid = (n // tm,)
    tok = lambda w: pl.BlockSpec((tm, w), lambda i: (i, 0))
    in_specs = [tok(D_MODEL), _const_spec((1, D_MODEL)), _const_spec((D_MODEL, _N_A)),
                _const_spec((1, 512)), _const_spec((2, LANES)), _const_spec((1, 512)),
                pl.BlockSpec((3, tm, LANES), lambda i: (0, i % tps, 0)),
                _const_spec((512, 512)), _const_spec((512, 512)), _const_spec((CONV_W, 512))]
    args = [x, pw['nmix'], pw['w_a'], pw['q_gain'], pw['k_gain'], pw['mq_gain'], rope_tab,
            pw['e64'], pw['e128'], pw['conv_w']]
    scratch = []
    if sample:
        in_specs += [tok(512), tok(512)]
        args += [state[:, 0, :], state[:, 1, :]]
        u_shape, u_spec = (n, 512), tok(512)
    else:
        scratch = [pltpu.VMEM((8, 512), F32)]
        u_shape, u_spec = (n // tm * 8, 512), pl.BlockSpec((8, 512), lambda i: (i, 0))
    out_shape = [jax.ShapeDtypeStruct((n, 512), BF16), jax.ShapeDtypeStruct((n, 512), BF16),
                 jax.ShapeDtypeStruct((n, 512), F32), jax.ShapeDtypeStruct((n, 256), F32),
                 jax.ShapeDtypeStruct((n, 512), BF16), jax.ShapeDtypeStruct((n, 512), BF16),
                 jax.ShapeDtypeStruct(u_shape, F32), jax.ShapeDtypeStruct((n, 512), BF16),
                 jax.ShapeDtypeStruct((n, LANES), F32)]
    out_specs = [tok(512), tok(512), tok(512), tok(256), tok(512), tok(512), u_spec, tok(512),
                 tok(LANES)]
    return pl.pallas_call(
        functools.partial(_prep_kernel, sample, tps),
        grid=grid, in_specs=in_specs, out_specs=out_specs, out_shape=out_shape,
        scratch_shapes=scratch,
        compiler_params=pltpu.CompilerParams(dimension_semantics=("arbitrary",),
                                             vmem_limit_bytes=VMEM_LIMIT),
        name="prep_sample" if sample else "prep_prompt",
    )(*args)


def _compress(load_xs, n_sub, w1_ref, pe_ref, w2_ref, kg_ref, e64k):
    outs = []
    for kv in range(2):
        acc = [None, None]
        for s in range(0, CMP_STRIDE, 2):
            xa, xb = load_xs(kv, s), load_xs(kv, s + 1)
            for r in range(CMP_RATIO):
                idx = (kv * CMP_RATIO + r) * CMP_STRIDE + s
                lhs = jnp.concatenate([(xa + pe_ref[idx:idx + 1, :]).astype(BF16),
                                       (xb + pe_ref[idx + 1:idx + 2, :]).astype(BF16)], axis=1)
                d = _dot(lhs, w1_ref[idx // 2])
                acc[r] = d if acc[r] is None else acc[r] + d
        h = acc[0] + pltpu.roll(acc[1], n_sub - 1, 0)
        o = _dot(jax.nn.silu(h).astype(BF16), w2_ref[kv])
        outs.append(o)
    kc = _group_rms(outs[0], e64k, HEAD_DIM, kg_ref[...])
    return kc, outs[1]


def _compress_prompt_kernel(xk_ref, xv_ref, w1_ref, pe_ref, w2_ref, kg_ref, e64_ref,
                            kc_out, vc_out):
    n_sub = kc_out.shape[1]
    x_refs = (xk_ref, xv_ref)
    kc, vc = _compress(lambda kv, s: x_refs[kv][0, pl.ds(s, n_sub, stride=CMP_STRIDE), :],
                       n_sub, w1_ref, pe_ref, w2_ref, kg_ref, e64_ref[...])
    kc_out[0] = kc.astype(BF16)
    vc_out[0] = vc.astype(BF16)


def _compress_prompt_call(rows3, pw):
    b, t, _ = rows3.shape
    n_sub = t // CMP_STRIDE
    out = jax.ShapeDtypeStruct((b, n_sub, LANES), BF16)
    return pl.pallas_call(
        _compress_prompt_kernel,
        grid=(b,),
        in_specs=[pl.BlockSpec((1, t, LANES), lambda i: (i, 0, 0)),
                  pl.BlockSpec((1, t, LANES), lambda i: (i, 0, 1)),
                  _const_spec((CMP_RATIO * CMP_STRIDE, 2 * LANES, 2 * CMP_HIDDEN)),
                  _const_spec((2 * CMP_RATIO * CMP_STRIDE, LANES)),
                  _const_spec((2, 2 * CMP_HIDDEN, LANES)),
                  _const_spec((1, LANES)), _const_spec((LANES, LANES))],
        out_specs=[pl.BlockSpec((1, n_sub, LANES), lambda i: (i, 0, 0))] * 2,
        out_shape=[out, out],
        compiler_params=pltpu.CompilerParams(dimension_semantics=("arbitrary",),
                                             vmem_limit_bytes=VMEM_LIMIT),
        name="compress_prompt",
    )(rows3, rows3, pw['cmp_w1'], pw['cmp_pe'], pw['cmp_w2'], pw['kc_gain'], pw['e64k'])


def _flash_init(m_ref, l_ref, acc_ref):
    m_ref[...] = jnp.full(m_ref.shape, M_INIT, F32)
    l_ref[...] = jnp.zeros(l_ref.shape, F32)
    acc_ref[...] = jnp.zeros(acc_ref.shape, F32)


def _flash_step(s, v, m_ref, l_ref, acc_ref):
    m_prev = m_ref[...]
    m_next = jnp.maximum(m_prev, jnp.max(s, axis=1, keepdims=True))
    p = jnp.exp(s - jnp.concatenate([m_next] * (s.shape[1] // LANES), axis=1))
    alpha = jnp.exp(m_prev - m_next)
    l_ref[...] = alpha * l_ref[...] + jnp.sum(p, axis=1, keepdims=True)
    acc_ref[...] = alpha * acc_ref[...] + _dot(p.astype(BF16), v)
    m_ref[...] = m_next


def _take_top(score_t, n_take):
    nb = score_t.shape[0]
    jidx = lax.broadcasted_iota(jnp.int32, score_t.shape, 0).astype(F32)
    taken = jnp.zeros(score_t.shape, F32)
    s = score_t
    for _ in range(n_take):
        mx = jnp.max(s, axis=0, keepdims=True)
        first = jnp.min(jnp.where(s == mx, jidx, float(nb)), axis=0, keepdims=True)
        pick = jidx == first
        taken = jnp.where(pick, 1.0, taken)
        s = jnp.where(pick, -jnp.inf, s)
    return taken


def _nsa_prompt_kernel(q_ref, qr_ref, kvb_ref, kc_ref, vc_ref, ng_ref, ovt_ref, cm_ref, gx_ref,
                       tri_ref, o_ref, m_ref, l_ref, acc_ref, ob_ref, qa_ref):
    tq = q_ref.shape[1]
    n_cmp = kc_ref.shape[1]
    n_blk = kvb_ref.shape[1] // SLC_BLOCK
    i = pl.program_id(1)
    q0 = i * tq
    lane = lax.broadcasted_iota(jnp.int32, (tq, LANES), 1)
    low = lane < HEAD_DIM

    def stack(qb, g):
        keep = low if g == 0 else jnp.logical_not(low)
        parts = [jnp.where(keep, qb[:, LANES * j:LANES * (j + 1)], jnp.zeros((tq, LANES), BF16))
                 for j in range(GROUP)]
        return jnp.concatenate(parts, axis=0)

    q = q_ref[0]
    qr = qr_ref[0]
    kc = kc_ref[0]
    vc = vc_ref[0]

    rowc = lax.broadcasted_iota(jnp.int32, (tq, n_cmp), 0)
    colc = lax.broadcasted_iota(jnp.int32, (tq, n_cmp), 1)
    vis1 = jnp.where(colc * CMP_STRIDE + (CMP_LEN - 1) <= q0 + rowc, 1.0, 0.0)
    vis = jnp.concatenate([vis1] * GROUP, axis=0) > 0.5

    for g in range(N_KV):
        s = _dot_nt(stack(q, g), kc)
        sm = jnp.where(vis, s, NEG)
        e = jnp.where(vis, jnp.exp(sm - jnp.max(sm, axis=1, keepdims=True)), 0.0)
        lsum = jnp.sum(e, axis=1, keepdims=True)
        pr = e / jnp.where(lsum > 0.0, lsum, 1.0)
        ob_ref[g] = _dot(pr.astype(BF16), vc)
        imp = pr[0:tq]
        for hh in range(1, GROUP):
            imp = imp + pr[hh * tq:(hh + 1) * tq]

        ih, il = _split_bf16(imp)
        sc_t = (_dot_nt(ovt_ref[...], ih) + _dot_nt(ovt_ref[...], il))[0:n_blk]
        blk_t = lax.broadcasted_iota(jnp.int32, sc_t.shape, 0)
        tpos_t = q0 + lax.broadcasted_iota(jnp.int32, sc_t.shape, 1)
        qblk_t = tpos_t // SLC_BLOCK
        forced_t = (blk_t == 0) | (blk_t == qblk_t) | (blk_t == qblk_t - 1)
        elig_t = blk_t * SLC_BLOCK <= tpos_t
        rest_t = _take_top(jnp.where(elig_t & jnp.logical_not(forced_t), sc_t, -jnp.inf),
                           N_SELECT - 3)
        keep_t = elig_t & (forced_t | (rest_t > 0.5))
        drop_t = jnp.where(keep_t, 0.0, 1.0)
        if n_blk < LANES:
            drop_t = jnp.concatenate([drop_t, jnp.zeros((LANES - n_blk, tq), F32)], axis=0)
        drop = jnp.transpose(drop_t).astype(BF16)
        qa_ref[g, :, 0:LANES] = stack(qr, g)
        qa_ref[g, :, LANES:2 * LANES] = jnp.concatenate([drop] * GROUP, axis=0)

    def flash(g):
        return m_ref.at[g], l_ref.at[g], acc_ref.at[g]

    def tile4(bias):
        return jnp.concatenate([bias] * GROUP, axis=0)

    for g in range(N_KV):
        _flash_init(*flash(g))

    def slc_tile(k0, diagonal):
        ka = jnp.concatenate([kvb_ref[0, pl.ds(k0, tq), 0:LANES], cm_ref[pl.ds(k0, tq), :]],
                             axis=1)
        v = kvb_ref[0, pl.ds(k0, tq), LANES:2 * LANES]
        for g in range(N_KV):
            st = _dot_nt(qa_ref[g], ka)
            if diagonal:
                st = st + tile4(tri_ref[0])
            _flash_step(st, v, *flash(g))

    def slc_body(kk, carry):
        slc_tile(pl.multiple_of(kk * tq, tq), False)
        return carry

    lax.fori_loop(0, i, slc_body, 0)
    slc_tile(pl.multiple_of(q0, tq), True)
    for g in range(N_KV):
        ob_ref[N_KV + g] = acc_ref[g] / l_ref[g]

    for g in range(N_KV):
        _flash_init(*flash(g))

    def win_tile(k0, bias):
        k = kvb_ref[0, pl.ds(k0, tq), 2 * LANES:3 * LANES]
        v = kvb_ref[0, pl.ds(k0, tq), 3 * LANES:4 * LANES]
        for g in range(N_KV):
            st = _dot_nt(qa_ref[g, :, 0:LANES], k)
            if bias is not None:
                st = st + tile4(tri_ref[bias])
            _flash_step(st, v, *flash(g))

    win_tile(pl.multiple_of(q0, tq), 0)
    n_back = WINDOW // tq
    for back in range(1, n_back + 1):
        @pl.when(i >= back)
        def _():
            win_tile(pl.multiple_of(q0 - back * tq, tq), 1 if back == n_back else None)
    for g in range(N_KV):
        ob_ref[2 * N_KV + g] = acc_ref[g] / l_ref[g]

    gh, gl = _split_bf16(ng_ref[0])
    for j in range(GROUP):
        tot = None
        for br in range(3):
            gx = gx_ref[:, (br * GROUP + j) * LANES:(br * GROUP + j + 1) * LANES]
            gate = _dot(gh, gx) + _dot(gl, gx)
            o_lo = ob_ref[br * N_KV + 0, j * tq:(j + 1) * tq, :]
            o_hi = ob_ref[br * N_KV + 1, j * tq:(j + 1) * tq, :]
            term = gate * jnp.where(low, o_lo, o_hi)
            tot = term if tot is None else tot + term
        o_ref[0, :, j * LANES:(j + 1) * LANES] = tot.astype(BF16)


def _nsa_prompt_call(q3, qr3, kvb3, kc, vc, ng3, pw):
    b, t, _ = q3.shape
    tq = min(TQ, t)
    n_cmp = kc.shape[1]
    blk = lambda w: pl.BlockSpec((1, tq, w), lambda bi, i: (bi, i, 0))
    per_b = lambda r, w: pl.BlockSpec((1, r, w), lambda bi, i: (bi, 0, 0))
    return pl.pallas_call(
        _nsa_prompt_kernel,
        grid=(b, t // tq),
        in_specs=[blk(512), blk(512), per_b(t, 512), per_b(n_cmp, LANES), per_b(n_cmp, LANES),
                  blk(LANES), _const_spec((LANES, n_cmp)), _const_spec((t, LANES)),
                  _const_spec((LANES, 3 * GROUP * LANES)), _const_spec((2, tq, tq))],
        out_specs=blk(512),
        out_shape=jax.ShapeDtypeStruct((b, t, 512), BF16),
        scratch_shapes=[pltpu.VMEM((N_KV, GROUP * tq, LANES), F32)] * 3
        + [pltpu.VMEM((3 * N_KV, GROUP * tq, LANES), F32),
           pltpu.VMEM((N_KV, GROUP * tq, 2 * LANES), BF16)],
        compiler_params=pltpu.CompilerParams(dimension_semantics=("arbitrary", "arbitrary"),
                                             vmem_limit_bytes=VMEM_LIMIT),
        name="nsa_prompt",
    )(q3, qr3, kvb3, kc, vc, ng3, pw['ov_p'], pw['cm_p'], pw['gate_x'],
      jnp.asarray(_band_bias(tq), F32))


def _memkv_kernel(mem_ref, nm_ref, w_ref, kg_ref, kv_out):
    x = mem_ref[0]
    ms = jnp.sum(x * x, axis=-1, keepdims=True) * (1.0 / D_MODEL)
    xn = (x * lax.rsqrt(ms + EPS) * nm_ref[...]).astype(BF16)
    kv = _dot(xn, w_ref[...])
    for h in range(MEM_HEADS):
        k = kv[:, h * LANES:(h + 1) * LANES]
        kms = jnp.sum(k * k, axis=-1, keepdims=True) * (1.0 / MEM_HEAD_DIM)
        kv_out[0, :, h * LANES:(h + 1) * LANES] = k * lax.rsqrt(kms + EPS) * kg_ref[...]
    kv_out[0, :, BRANCH_W:2 * BRANCH_W] = kv[:, BRANCH_W:2 * BRANCH_W]


def _memkv_call(mem, pw):
    b, m, _ = mem.shape
    return pl.pallas_call(
        _memkv_kernel,
        grid=(b,),
        in_specs=[pl.BlockSpec((1, m, D_MODEL), lambda i: (i, 0, 0)), _const_spec((1, D_MODEL)),
                  _const_spec((D_MODEL, 2 * BRANCH_W)), _const_spec((1, LANES))],
        out_specs=pl.BlockSpec((1, m, 2 * BRANCH_W), lambda i: (i, 0, 0)),
        out_shape=jax.ShapeDtypeStruct((b, m, 2 * BRANCH_W), F32),
        compiler_params=pltpu.CompilerParams(dimension_semantics=("arbitrary",),
                                             vmem_limit_bytes=VMEM_LIMIT),
        name="mem_kv",
    )(mem, pw['nmem'], pw['w_mem'], pw['mk_gain'])


def _mem_attend(mq, kv_ref):
    outs = []
    for h in range(MEM_HEADS):
        k = kv_ref[:, h * LANES:(h + 1) * LANES].astype(BF16)
        v = kv_ref[:, BRANCH_W + h * LANES:BRANCH_W + (h + 1) * LANES].astype(BF16)
        s = _dot_nt(mq[:, h * LANES:(h + 1) * LANES], k) * MEM_HEAD_DIM ** -0.5
        e = jnp.exp(s - jnp.max(s, axis=1, keepdims=True))
        pr = e / jnp.sum(e, axis=1, keepdims=True)
        outs.append(_dot(pr.astype(BF16), v))
    return jnp.concatenate(outs, axis=1)


def _mem_attn_kernel(mq_ref, kv_ref, o_ref):
    o_ref[0] = _mem_attend(mq_ref[0], kv_ref.at[0]).astype(BF16)


def _mem_attn_call(mq3, mkv):
    b, t, _ = mq3.shape
    tq = min(TM, t)
    return pl.pallas_call(
        _mem_attn_kernel,
        grid=(b, t // tq),
        in_specs=[pl.BlockSpec((1, tq, 512), lambda bi, i: (bi, i, 0)),
                  pl.BlockSpec((1, MEM_LEN, 2 * BRANCH_W), lambda bi, i: (bi, 0, 0))],
        out_specs=pl.BlockSpec((1, tq, 512), lambda bi, i: (bi, i, 0)),
        out_shape=jax.ShapeDtypeStruct((b, t, 512), BF16),
        compiler_params=pltpu.CompilerParams(dimension_semantics=("arbitrary", "arbitrary"),
                                             vmem_limit_bytes=VMEM_LIMIT),
        name="mem_attn",
    )(mq3, mkv)


def _finish_kernel(x_ref, on_ref, z_ref, om_ref, nmix_ref, wmg_ref, wbr_ref, wout_ref, h_out):
    x = x_ref[...]
    ms = jnp.sum(x * x, axis=-1, keepdims=True) * (1.0 / D_MODEL)
    xn = (x * lax.rsqrt(ms + EPS) * nmix_ref[...]).astype(BF16)
    merged = None
    for n, br_ref in enumerate((on_ref, z_ref, om_ref)):
        gate = jax.nn.sigmoid(_dot(xn, wmg_ref[:, n * D_MODEL:(n + 1) * D_MODEL]))
        term = gate * _dot(br_ref[...], wbr_ref[n])
        merged = term if merged is None else merged + term
    h_out[...] = x + _dot(merged.astype(BF16), wout_ref[...])


def _finish_call(x, o_nsa, z, o_mem, pw):
    n = x.shape[0]
    tm = min(TM, n)
    tok = lambda w: pl.BlockSpec((tm, w), lambda i: (i, 0))
    return pl.pallas_call(
        _finish_kernel,
        grid=(n // tm,),
        in_specs=[tok(D_MODEL), tok(512), tok(512), tok(512), _const_spec((1, D_MODEL)),
                  _const_spec((D_MODEL, N_BRANCH * D_MODEL)),
                  _const_spec((N_BRANCH, BRANCH_W, D_MODEL)), _const_spec((D_MODEL, D_MODEL))],
        out_specs=tok(D_MODEL),
        out_shape=jax.ShapeDtypeStruct((n, D_MODEL), F32),
        compiler_params=pltpu.CompilerParams(dimension_semantics=("arbitrary",),
                                             vmem_limit_bytes=VMEM_LIMIT),
        name="finish",
    )(x, o_nsa, z, o_mem, pw['nmix'], pw['w_mg'], pw['w_br'], pw['w_out'])


def _ffn_kernel(h_ref, nf_ref, wgu_ref, wd_ref, y_out):
    h = h_ref[...]
    ms = jnp.sum(h * h, axis=-1, keepdims=True) * (1.0 / D_MODEL)
    hn = (h * lax.rsqrt(ms + EPS) * nf_ref[...]).astype(BF16)
    acc = h
    for c in range(D_FF // FF_CHUNK):
        g = _dot(hn, wgu_ref[:, c * FF_CHUNK:(c + 1) * FF_CHUNK])
        u = _dot(hn, wgu_ref[:, D_FF + c * FF_CHUNK:D_FF + (c + 1) * FF_CHUNK])
        a = (jax.nn.silu(g) * u).astype(BF16)
        acc = acc + _dot(a, wd_ref[c * FF_CHUNK:(c + 1) * FF_CHUNK, :])
    y_out[...] = acc


def _ffn_call(h, pw):
    n = h.shape[0]
    tm = min(TM, n)
    tok = pl.BlockSpec((tm, D_MODEL), lambda i: (i, 0))
    return pl.pallas_call(
        _ffn_kernel,
        grid=(n // tm,),
        in_specs=[tok, _const_spec((1, D_MODEL)), _const_spec((D_MODEL, 2 * D_FF)),
                  _const_spec((D_FF, D_MODEL))],
        out_specs=tok,
        out_shape=jax.ShapeDtypeStruct((n, D_MODEL), F32),
        compiler_params=pltpu.CompilerParams(dimension_semantics=("arbitrary",),
                                             vmem_limit_bytes=VMEM_LIMIT),
        name="ffn",
    )(h, pw['nffn'], pw['w_gu'], pw['w_down'])


def _nsa_sample_kernel(layer_base, pt_ref, pool_ref, q_ref, qr_ref, rows_ref, wnew_ref,
                       wbuf_ref, mq_ref, mkv_ref, ng_ref, w1_ref, pe_ref, w2_ref, kg_ref, e64_ref,
                       ov_ref, gx_ref,
                       o_ref, wout_ref, om_ref,
                       tbuf_ref, kraw_ref, vraw_ref, gk_ref, gv_ref, idx_vm, idx_sm, cmp_sem,
                       gat_sem, idx_sem):
    b = pl.program_id(0)
    nb = pl.num_programs(0)
    n_pages = pt_ref.shape[1]
    n_past = n_pages * PAGE_SIZE
    n_sub = n_past // CMP_STRIDE
    n_gather = gk_ref.shape[1]
    q_blk = n_past // SLC_BLOCK

    def cmp_copies(bb, slot, j):
        page = layer_base + pt_ref[bb, j]
        return [pltpu.make_async_copy(pool_ref.at[page, kv], tbuf_ref.at[slot, kv, j],
                                      cmp_sem.at[slot]) for kv in range(2)]

    def start_cmp(bb, slot):
        def body(j, c):
            for cp in cmp_copies(bb, slot, j):
                cp.start()
            return c
        lax.fori_loop(0, n_pages, body, 0)

    def wait_cmp(bb, slot):
        def body(j, c):
            for cp in cmp_copies(bb, slot, j):
                cp.wait()
            return c
        lax.fori_loop(0, n_pages, body, 0)

    slot = b % 2

    @pl.when(b == 0)
    def _():
        start_cmp(b, slot)

    @pl.when(b + 1 < nb)
    def _():
        start_cmp(b + 1, 1 - slot)

    wait_cmp(b, slot)

    row8 = lax.broadcasted_iota(jnp.int32, (8, LANES), 0)
    lane8 = lax.broadcasted_iota(jnp.int32, (8, LANES), 1)
    low = lane8 < HEAD_DIM
    own_half = (row8 < GROUP) == low

    def stack8(qrow):
        out = jnp.zeros((8, LANES), F32)
        for j in range(GROUP):
            pj = jnp.broadcast_to(qrow[:, j * LANES:(j + 1) * LANES], (8, LANES))
            out = jnp.where((row8 & (GROUP - 1)) == j, pj, out)
        return jnp.where(own_half, out, 0.0).astype(BF16)

    q8 = stack8(q_ref[0])
    qr8 = stack8(qr_ref[0])

    raw_refs = (kraw_ref, vraw_ref)

    def to_rows(j, c):
        dst = pl.ds(pl.multiple_of(j * PAGE_SIZE, PAGE_SIZE), PAGE_SIZE)
        for kv in range(2):
            tile = tbuf_ref[slot, kv, j].reshape(N_KV * HEAD_DIM, PAGE_SIZE)
            raw_refs[kv][dst, :] = jnp.transpose(tile)
        return c
    lax.fori_loop(0, n_pages, to_rows, 0, unroll=8)

    kc, vc = _compress(
        lambda kv, s: raw_refs[kv][pl.ds(s, n_sub, stride=CMP_STRIDE), :],
        n_sub, w1_ref, pe_ref, w2_ref, kg_ref, e64_ref[...])
    s = _dot_nt(q8, kc.astype(BF16))
    col = lax.broadcasted_iota(jnp.int32, s.shape, 1)
    rowc = lax.broadcasted_iota(jnp.int32, s.shape, 0)
    vis = col < n_sub - CMP_RATIO + 1
    sm = jnp.where(vis, s, NEG)
    e = jnp.where(vis, jnp.exp(sm - jnp.max(sm, axis=1, keepdims=True)), 0.0)
    pr = e / jnp.sum(e, axis=1, keepdims=True)
    o_c = _dot(pr.astype(BF16), vc.astype(BF16))
    imp8 = jnp.zeros(s.shape, F32)
    for g in range(N_KV):
        imp_g = jnp.sum(jnp.where(rowc // GROUP == g, pr, 0.0), axis=0, keepdims=True)
        imp8 = jnp.where(rowc == g, jnp.broadcast_to(imp_g, s.shape), imp8)

    n_blk_pad = ov_ref.shape[1]
    sc = _dot_exact01(imp8, ov_ref[...])
    bl = lax.broadcasted_iota(jnp.int32, sc.shape, 1)
    forced = (bl == 0) | (bl == q_blk) | (bl == q_blk - 1)
    sc = jnp.where(forced, jnp.inf, jnp.where(bl <= q_blk, sc, -jnp.inf))
    sc_t = jnp.transpose(sc)
    blf = bl[0:1].astype(F32)
    idx8 = jnp.zeros((8, LANES), F32)
    for g in range(N_KV):
        colv = jnp.broadcast_to(sc_t[:, g:g + 1], (n_blk_pad, n_blk_pad))
        rowv = jnp.broadcast_to(sc[g:g + 1, :], (n_blk_pad, n_blk_pad))
        ii = lax.broadcasted_iota(jnp.int32, colv.shape, 0)
        jj = lax.broadcasted_iota(jnp.int32, colv.shape, 1)
        tie = jnp.where(ii < jj, 1.0, 0.0)
        beats = jnp.where(colv > rowv, 1.0, jnp.where(colv == rowv, tie, 0.0))
        rank = jnp.sum(beats, axis=0, keepdims=True)
        for k in range(N_SELECT):
            blk_k = jnp.sum(jnp.where(rank == float(k), blf, 0.0), axis=1, keepdims=True)
            idx8 = jnp.where((row8 == g) & (lane8 == k), blk_k, idx8)
    idx_vm[...] = idx8.astype(jnp.int32)
    idx_copy = pltpu.make_async_copy(idx_vm, idx_sm, idx_sem.at[0])
    idx_copy.start()

    w_len = wbuf_ref.shape[-1]
    wlane = lax.broadcasted_iota(jnp.int32, (HEAD_DIM, w_len), 1)
    wnew = wnew_ref[0]
    w_tiles = [[None] * N_KV for _ in range(2)]
    for kv in range(2):
        cols = jnp.transpose(jnp.broadcast_to(wnew[:, kv * LANES:(kv + 1) * LANES], (8, LANES)))
        for g in range(N_KV):
            col = cols[g * HEAD_DIM:(g + 1) * HEAD_DIM, 0:1]
            upd = jnp.where(wlane == w_len - 1, col, pltpu.roll(wbuf_ref[0, kv, g], w_len - 1, 1))
            wout_ref[0, kv, g] = upd
            w_tiles[kv][g] = upd.astype(BF16)
    o_w_parts = []
    for g in range(N_KV):
        s_w = _dot(qr8[:, g * HEAD_DIM:(g + 1) * HEAD_DIM], w_tiles[0][g])
        e_w = jnp.exp(s_w - jnp.max(s_w, axis=1, keepdims=True))
        pr_w = e_w / jnp.sum(e_w, axis=1, keepdims=True)
        o_w_parts.append(_dot_nt(pr_w.astype(BF16), w_tiles[1][g]))
    o_w = jnp.concatenate(o_w_parts, axis=1)

    idx_copy.wait()
    past_ranks = [k for k in range(N_SELECT) if k != 2]
    gathers, halves = [], [[], []]
    for g in range(N_KV):
        for kk, k in enumerate(past_ranks):
            blk = jnp.minimum(idx_sm[g, k], q_blk - 1)
            page = layer_base + pt_ref[b, lax.shift_right_logical(blk, 1)]
            halves[g].append(blk & 1)
            gathers.append(pltpu.make_async_copy(pool_ref.at[page, 2, g], gk_ref.at[g, kk],
                                                 gat_sem.at[0]))
            gathers.append(pltpu.make_async_copy(pool_ref.at[page, 3, g], gv_ref.at[g, kk],
                                                 gat_sem.at[0]))
    for cp in gathers:
        cp.start()

    mq8 = jnp.broadcast_to(mq_ref[0], (8, BRANCH_W)).astype(BF16)
    om_ref[0] = _mem_attend(mq8, mkv_ref.at[0])

    for cp in gathers:
        cp.wait()
    rows = rows_ref[0]
    k_new = rows[:, 2 * LANES:3 * LANES].astype(BF16).astype(F32)
    v_new = rows[:, 3 * LANES:4 * LANES].astype(BF16).astype(F32)
    s_new = jnp.sum(qr8.astype(F32) * k_new, axis=1, keepdims=True)
    lane_blk = lane8 // SLC_BLOCK
    o_parts, p_new = [], []
    for g in range(N_KV):
        q_g = qr8[:, g * HEAD_DIM:(g + 1) * HEAD_DIM]
        ss = []
        for kk in range(n_gather):
            st = _dot(q_g, gk_ref[g, kk].astype(BF16))
            ss.append(jnp.where(lane_blk == halves[g][kk], st, NEG))
        s_past = jnp.concatenate(ss, axis=1)
        m = jnp.maximum(jnp.max(s_past, axis=1, keepdims=True), s_new)
        e_past = jnp.exp(s_past - m)
        e_new = jnp.exp(s_new - m)
        den = jnp.sum(e_past, axis=1, keepdims=True) + e_new
        pr_past = (e_past / den).astype(BF16)
        o_g = None
        for kk in range(n_gather):
            d = _dot_nt(pr_past[:, kk * LANES:(kk + 1) * LANES], gv_ref[g, kk].astype(BF16))
            o_g = d if o_g is None else o_g + d
        o_parts.append(o_g)
        p_new.append((e_new / den).astype(BF16).astype(F32))
    pr_new = jnp.where(row8[:, 0:1] < GROUP, p_new[0], p_new[1])
    o_s = jnp.concatenate(o_parts, axis=1) + pr_new * v_new

    gh, gl = _split_bf16(jnp.broadcast_to(ng_ref[0], (8, LANES)))
    tot = jnp.zeros((8, LANES), F32)
    for br, o8 in enumerate((o_c, o_s, o_w)):
        pair = jnp.where(low, o8, pltpu.roll(o8, GROUP, 0))
        gate8 = jnp.zeros((8, LANES), F32)
        for j in range(GROUP):
            gx = gx_ref[:, (br * GROUP + j) * LANES:(br * GROUP + j + 1) * LANES]
            gate8 = jnp.where(row8 == j, _dot(gh, gx) + _dot(gl, gx), gate8)
        tot = tot + gate8 * pair
    o_ref[0] = tot


def _nsa_sample_call(layer, page_table, pool5, q4, qr4, rows_s, win_new, win_buf, mq4, mkv,
                     ng_s, pw):
    nb, n_pages = page_table.shape
    n_past = n_pages * PAGE_SIZE
    n_sub = n_past // CMP_STRIDE
    n_phys = pool5.shape[0] // DEPTH
    w_buf = win_buf.shape[-1]
    n_blk_pad = pw['ov_s'].shape[1]
    per_b = lambda *shape: pl.BlockSpec((1,) + shape, lambda b, pt: (b,) + (0,) * len(shape))
    grid_spec = pltpu.PrefetchScalarGridSpec(
        num_scalar_prefetch=1,
        grid=(nb,),
        in_specs=[pl.BlockSpec(memory_space=pl.ANY),
                  per_b(1, 512), per_b(1, 512), per_b(1, 512), per_b(1, 256),
                  per_b(2, N_KV, HEAD_DIM, w_buf), per_b(1, 512), per_b(MEM_LEN, 2 * BRANCH_W),
                  per_b(1, LANES),
                  _const_spec((CMP_RATIO * CMP_STRIDE, 2 * LANES, 2 * CMP_HIDDEN)),
                  _const_spec((2 * CMP_RATIO * CMP_STRIDE, LANES)),
                  _const_spec((2, 2 * CMP_HIDDEN, LANES)),
                  _const_spec((1, LANES)), _const_spec((LANES, LANES)),
                  _const_spec((n_sub, n_blk_pad)),
                  _const_spec((LANES, 3 * GROUP * LANES))],
        out_specs=[per_b(8, LANES), per_b(2, N_KV, HEAD_DIM, w_buf), per_b(8, BRANCH_W)],
        scratch_shapes=[
            pltpu.VMEM((2, 2, n_pages, N_KV, HEAD_DIM, PAGE_SIZE), F32),
            pltpu.VMEM((n_past, LANES), F32), pltpu.VMEM((n_past, LANES), F32),
            pltpu.VMEM((N_KV, N_SELECT - 1, HEAD_DIM, PAGE_SIZE), F32),
            pltpu.VMEM((N_KV, N_SELECT - 1, HEAD_DIM, PAGE_SIZE), F32),
            pltpu.VMEM((8, LANES), jnp.int32), pltpu.SMEM((8, LANES), jnp.int32),
            pltpu.SemaphoreType.DMA((2,)), pltpu.SemaphoreType.DMA((1,)),
            pltpu.SemaphoreType.DMA((1,))],
    )
    return pl.pallas_call(
        functools.partial(_nsa_sample_kernel, layer * n_phys),
        grid_spec=grid_spec,
        out_shape=[jax.ShapeDtypeStruct((nb, 8, LANES), F32),
                   jax.ShapeDtypeStruct((nb, 2, N_KV, HEAD_DIM, w_buf), F32),
                   jax.ShapeDtypeStruct((nb, 8, BRANCH_W), F32)],
        compiler_params=pltpu.CompilerParams(dimension_semantics=("arbitrary",),
                                             vmem_limit_bytes=VMEM_LIMIT),
        name="nsa_sample",
    )(page_table, pool5, q4, qr4, rows_s, win_new, win_buf, mq4, mkv, ng_s,
      pw['cmp_w1'], pw['cmp_pe'], pw['cmp_w2'], pw['kc_gain'], pw['e64k'], pw['ov_s'],
      pw['gate_x'])


def _pair_perm():
    p = np.arange(BRANCH_W)
    j, ln = p // LANES, p % LANES
    head = np.where(ln < HEAD_DIM, j, GROUP + j)
    return head * HEAD_DIM + ln % HEAD_DIM


def _block_diag_ones(width, group):
    i = np.arange(width)
    return (i[:, None] // group == i[None, :] // group).astype(np.float32)


def _overlap(n_cmp_rows, n_cmp_valid, n_blk_cols):
    i = np.arange(n_cmp_rows)[:, None]
    j = np.arange(n_blk_cols)[None, :]
    ov = (i * CMP_STRIDE < (j + 1) * SLC_BLOCK) & (i * CMP_STRIDE + CMP_LEN > j * SLC_BLOCK)
    return (ov & (i < n_cmp_valid)).astype(np.float32)


def _block_mask_rows(n_keys):
    k = np.arange(n_keys)[:, None]
    j = np.arange(LANES)[None, :]
    return np.where(k // SLC_BLOCK == j, NEG, 0.0).astype(np.float32)


def _band_bias(tq):
    r = np.arange(tq)[:, None]
    c = np.arange(tq)[None, :]
    return np.stack([np.where(c <= r, 0.0, NEG), np.where(c > r, 0.0, NEG)]).astype(np.float32)


def _gate_expand():
    x = np.zeros((LANES, 3 * GROUP * LANES), np.float32)
    for br in range(3):
        for j in range(GROUP):
            for ln in range(LANES):
                head = j if ln < HEAD_DIM else GROUP + j
                x[br * N_HEADS + head, (br * GROUP + j) * LANES + ln] = 1.0
    return x


def _rope_tables(pos):
    half = ROT_DIM // 2
    inv = ROPE_THETA ** (-jnp.arange(half, dtype=F32) / half)
    ang = pos.astype(F32)[:, None] * inv[None, :]
    cos, sin = jnp.cos(ang), jnp.sin(ang)
    n = pos.shape[0]
    one = jnp.ones((n, HEAD_DIM - ROT_DIM), F32)
    zero = jnp.zeros((n, HEAD_DIM - ROT_DIM), F32)
    zh = jnp.zeros((n, half), F32)
    c = jnp.concatenate([cos, cos, one], axis=1)
    s1 = jnp.concatenate([zh, sin, zero], axis=1)
    s2 = jnp.concatenate([-sin, zh, zero], axis=1)
    tab = jnp.stack([c, s1, s2])
    return jnp.concatenate([tab, tab], axis=2)


def _layer_params(l, seq_len, n_past, norm_mix, w_in, q_norm, k_norm, cmp_pe, cmp_w1, cmp_w2,
                  conv_w, norm_mem, w_mem_kv, mem_q_norm, mem_k_norm, w_branch, w_out, norm_ffn,
                  w_gate_up, w_down):
    perm = _pair_perm()
    wi = w_in[l]
    o_kv, o_ng = BRANCH_W, BRANCH_W + 768
    o_cx = o_ng + 3 * N_HEADS
    o_cb, o_cc, o_mq, o_mg = o_cx + 512, o_cx + 1024, o_cx + 1536, o_cx + 2048
    w_a = jnp.concatenate([
        wi[:, perm], wi[:, o_kv:o_ng], wi[:, o_cx:o_cb], wi[:, o_cb:o_cc], wi[:, o_cc:o_mq],
        wi[:, o_mq:o_mg], wi[:, o_ng:o_cx], jnp.zeros((D_MODEL, LANES - 3 * N_HEADS), F32)],
        axis=1).astype(BF16)
    w1 = cmp_w1[l].reshape(2, CMP_RATIO, CMP_STRIDE, HEAD_DIM, CMP_HIDDEN)
    zw = jnp.zeros_like(w1)
    w1bd = jnp.concatenate([jnp.concatenate([w1, zw], axis=-1),
                            jnp.concatenate([zw, w1], axis=-1)], axis=-2)
    w2 = cmp_w2[l]
    z2 = jnp.zeros_like(w2)
    w2bd = jnp.concatenate([jnp.concatenate([w2, z2], axis=-1),
                            jnp.concatenate([z2, w2], axis=-1)], axis=-2)
    pe = cmp_pe[l].reshape(2 * CMP_RATIO * CMP_STRIDE, HEAD_DIM)
    n_sub_p = seq_len // CMP_STRIDE
    n_sub_s = n_past // CMP_STRIDE
    n_blk_s = -(-(n_past // SLC_BLOCK + 1) // LANES) * LANES
    wbr = w_branch[l]
    return dict(
        nmix=norm_mix[l][None, :], w_a=w_a, w_mg=wi[:, o_mg:].astype(BF16),
        q_gain=jnp.tile(q_norm[l], N_HEADS)[None, :],
        k_gain=jnp.stack([jnp.tile(k_norm[l, 1], N_KV), jnp.tile(k_norm[l, 2], N_KV)]),
        kc_gain=jnp.tile(k_norm[l, 0], N_KV)[None, :],
        mq_gain=jnp.tile(mem_q_norm[l], MEM_HEADS)[None, :],
        mk_gain=mem_k_norm[l][None, :],
        conv_w=conv_w[l],
        cmp_w1=w1bd.reshape(CMP_RATIO * CMP_STRIDE, 2 * LANES, 2 * CMP_HIDDEN).astype(BF16),
        cmp_pe=jnp.concatenate([pe, pe], axis=1),
        cmp_w2=w2bd.astype(BF16),
        nmem=norm_mem[l][None, :], w_mem=w_mem_kv[l].astype(BF16),
        w_br=jnp.concatenate([wbr[0][perm][None], wbr[1:]], axis=0).astype(BF16),
        w_out=w_out[l].astype(BF16),
        nffn=norm_ffn[l][None, :], w_gu=w_gate_up[l].astype(BF16), w_down=w_down[l].astype(BF16),
        e64=jnp.asarray(_block_diag_ones(512, HEAD_DIM), BF16),
        e64k=jnp.asarray(_block_diag_ones(LANES, HEAD_DIM), BF16),
        e128=jnp.asarray(_block_diag_ones(512, MEM_HEAD_DIM), BF16),
        ov_p=jnp.asarray(_overlap(n_sub_p, n_sub_p - CMP_RATIO + 1, LANES).T, BF16),
        cm_p=jnp.asarray(_block_mask_rows(seq_len), BF16),
        ov_s=jnp.asarray(_overlap(n_sub_s, n_sub_s - CMP_RATIO + 1, n_blk_s), BF16),
        gate_x=jnp.asarray(_gate_expand(), BF16),
    )


def kernel(x_prompt, x_sample, cache_nsa_kv, cache_win_kv, state_conv, cache_mem_kv, page_table,
           mem_prompt, norm_mix, w_in, q_norm, k_norm, cmp_pe, cmp_w1, cmp_w2, conv_w, norm_mem,
           w_mem_kv, mem_q_norm, mem_k_norm, w_branch, w_out, norm_ffn, w_gate_up, w_down):
    bp, t, _ = x_prompt.shape
    bs, ts, _ = x_sample.shape
    assert ts == 1 and t % TM == 0 and t % TQ == 0
    n_pages = page_table.shape[1]
    n_past = n_pages * PAGE_SIZE
    w_buf = cache_win_kv.shape[2]
    assert w_buf == WINDOW and n_past >= WINDOW and (n_past // SLC_BLOCK) == LANES

    xp = x_prompt.reshape(bp * t, D_MODEL)
    xs = x_sample.reshape(bs, D_MODEL)
    rope_p = _rope_tables(jnp.arange(t))
    rope_s = _rope_tables(jnp.full((bs,), n_past, jnp.int32))
    n_phys = cache_nsa_kv.shape[1]
    win_t = jnp.transpose(cache_win_kv, (0, 1, 3, 4, 5, 2))
    pool5 = jnp.transpose(cache_nsa_kv, (0, 1, 3, 4, 5, 2)).reshape(
        DEPTH * n_phys, 4, N_KV, HEAD_DIM, PAGE_SIZE)

    rows_p, rows_s, win_p, win_s, conv_p, conv_s, mem_p = [], [], [], [], [], [], []
    for l in range(DEPTH):
        pw = _layer_params(l, t, n_past, norm_mix, w_in, q_norm, k_norm, cmp_pe, cmp_w1, cmp_w2,
                           conv_w, norm_mem, w_mem_kv, mem_q_norm, mem_k_norm, w_branch, w_out,
                           norm_ffn, w_gate_up, w_down)
        q, qr, rows, win, kvb, z, utail, mq, ng = _prep_call(xp, pw, rope_p, False, t)
        rows3 = rows.reshape(bp, t, 512)
        kc, vc = _compress_prompt_call(rows3, pw)
        o_nsa = _nsa_prompt_call(q.reshape(bp, t, 512), qr.reshape(bp, t, 512),
                                 kvb.reshape(bp, t, 512), kc, vc, ng.reshape(bp, t, LANES), pw)
        mkv = _memkv_call(mem_prompt, pw)
        o_mem = _mem_attn_call(mq.reshape(bp, t, 512), mkv)
        h = _finish_call(xp, o_nsa.reshape(bp * t, 512), z, o_mem.reshape(bp * t, 512), pw)
        xp = _ffn_call(h, pw)
        rows_p.append(rows3.reshape(bp, t, 4, N_KV, HEAD_DIM))
        win_p.append(win.reshape(bp, t, 2, N_KV, HEAD_DIM)[:, t - min(WINDOW, t):])
        conv_p.append(utail.reshape(bp, t // TM, 8, 512)[:, -1, 8 - (CONV_W - 1):])
        mem_p.append(mkv.reshape(bp, MEM_LEN, 2, MEM_HEADS, MEM_HEAD_DIM))
        q, qr, rows, win, _, z, u, mq, ng = _prep_call(xs, pw, rope_s, True, 1,
                                                       state=state_conv[l])
        o_nsa, new_win, o_mem = _nsa_sample_call(
            l, page_table, pool5, q.astype(F32).reshape(bs, 1, 512),
            qr.astype(F32).reshape(bs, 1, 512), rows.reshape(bs, 1, 512),
            win.reshape(bs, 1, 256), win_t[l],
            mq.astype(F32).reshape(bs, 1, 512),
            cache_mem_kv[l].reshape(bs, MEM_LEN, 2 * BRANCH_W), ng.reshape(bs, 1, LANES), pw)
        o_nsa = o_nsa[:, 0:GROUP, :].reshape(bs, 512).astype(BF16)
        o_mem = o_mem[:, 0, :].astype(BF16)
        h = _finish_call(xs, o_nsa, z, o_mem, pw)
        xs = _ffn_call(h, pw)
        rows_s.append(rows.reshape(bs, 1, 4, N_KV, HEAD_DIM))
        win_s.append(new_win)
        conv_s.append(jnp.stack([state_conv[l][:, 1, :], u], axis=1))

    return (xp.reshape(bp, t, D_MODEL), xs.reshape(bs, 1, D_MODEL), jnp.stack(rows_p),
            jnp.stack(rows_s), jnp.stack(win_p),
            jnp.transpose(jnp.stack(win_s), (0, 1, 5, 2, 3, 4)), jnp.stack(conv_p),
            jnp.stack(conv_s), jnp.stack(mem_p))
```

```python
import functools

import numpy as np
import jax
import jax.numpy as jnp
from jax import lax
from jax.experimental import pallas as pl
from jax.experimental.pallas import tpu as pltpu

F32 = jnp.float32
BF16 = jnp.bfloat16

D_MODEL = 1024
DEPTH = 2
PAST_LEN = 8192
PAGE_SIZE = 128
HEAD_DIM = 64
BRANCH_W = D_MODEL // 2
N_HEADS = BRANCH_W // HEAD_DIM
N_KV = N_HEADS // 4
GROUP = N_HEADS // N_KV
ROT_DIM = HEAD_DIM // 4
ROPE_THETA = 500000.0
CMP_LEN = 32
CMP_STRIDE = 16
CMP_RATIO = CMP_LEN // CMP_STRIDE
CMP_HIDDEN = 2 * HEAD_DIM
SLC_BLOCK = 64
N_SELECT = 16
WINDOW = 512
CONV_W = 3
MEM_LEN = 256
MEM_HEADS = 4
MEM_HEAD_DIM = BRANCH_W // MEM_HEADS
N_BRANCH = 3
D_FF = ((8 * D_MODEL + 767) // 768) * 256
EPS = 1e-6

LANES = 128
VMEM_LIMIT = 56 * 1024 * 1024

TM = 512
TQ = 512
FF_CHUNK = 256
NEG = -1e30
M_INIT = -3e38

_C_Q = 0
_C_KV = 512
_C_CX = 1280
_C_CB = 1792
_C_CC = 2304
_C_MQ = 2816
_C_NG = 3328
_N_A = 3456


def _const_spec(shape):
    nd = len(shape)
    return pl.BlockSpec(shape, lambda *_: (0,) * nd, pipeline_mode=pl.Buffered(1))


def _dot(a, b):
    return jnp.dot(a, b, preferred_element_type=F32)


def _dot_nt(a, b):
    return lax.dot_general(a, b, (((1,), (1,)), ((), ())), preferred_element_type=F32)


def _split_bf16(v):
    hi = v.astype(BF16)
    lo = (v - hi.astype(F32)).astype(BF16)
    return hi, lo


def _dot_exact01(v, m01):
    hi, lo = _split_bf16(v)
    return _dot(hi, m01) + _dot(lo, m01)


def _group_rms(v, e01, group, gain):
    ms = _dot_exact01(v * v, e01) * (1.0 / group)
    return v * lax.rsqrt(ms + EPS) * gain


def _rope_cols(v, c, s1, s2):
    outs = []
    for j in range(v.shape[1] // LANES):
        col = v[:, j * LANES:(j + 1) * LANES]
        outs.append(col * c + pltpu.roll(col, ROT_DIM // 2, 1) * s1
                    + pltpu.roll(col, LANES - ROT_DIM // 2, 1) * s2)
    return outs[0] if len(outs) == 1 else jnp.concatenate(outs, axis=1)


def _prep_kernel(sample, tiles_per_seq, *refs):
    if sample:
        (x_ref, nmix_ref, w_ref, qg_ref, kg_ref, mqg_ref, rope_ref, e64_ref, e128_ref, cw_ref,
         s0_ref, s1_ref,
         q_out, qr_out, rows_out, win_out, kvb_out, z_out, u_out, mq_out, ng_out) = refs
    else:
        (x_ref, nmix_ref, w_ref, qg_ref, kg_ref, mqg_ref, rope_ref, e64_ref, e128_ref, cw_ref,
         q_out, qr_out, rows_out, win_out, kvb_out, z_out, u_out, mq_out, ng_out,
         carry_ref) = refs
    tm = x_ref.shape[0]
    x = x_ref[...]
    ms = jnp.sum(x * x, axis=-1, keepdims=True) * (1.0 / D_MODEL)
    xn = (x * lax.rsqrt(ms + EPS) * nmix_ref[...]).astype(BF16)

    rc, rs1, rs2 = rope_ref[0], rope_ref[1], rope_ref[2]
    e64 = e64_ref[...]

    q = _dot(xn, w_ref[:, _C_Q:_C_Q + 512])
    qn = _group_rms(q, e64, HEAD_DIM, qg_ref[...])
    q_out[...] = (qn * HEAD_DIM ** -0.5).astype(BF16)
    qr_out[...] = (_rope_cols(qn, rc, rs1, rs2) * HEAD_DIM ** -0.5).astype(BF16)

    kv = _dot(xn, w_ref[:, _C_KV:_C_KV + 768])
    e64k = e64_ref[0:LANES, 0:LANES]
    k_slc = _rope_cols(_group_rms(kv[:, 256:384], e64k, HEAD_DIM, kg_ref[0:1, :]), rc, rs1, rs2)
    k_win = _rope_cols(_group_rms(kv[:, 512:640], e64k, HEAD_DIM, kg_ref[1:2, :]), rc, rs1, rs2)
    v_slc = kv[:, 384:512]
    v_win = kv[:, 640:768]
    rows_out[:, 0:256] = kv[:, 0:256]
    rows_out[:, 256:384] = k_slc
    rows_out[:, 384:512] = v_slc
    win_out[:, 0:128] = k_win
    win_out[:, 128:256] = v_win
    kvb_out[:, 0:128] = k_slc.astype(BF16)
    kvb_out[:, 128:256] = v_slc.astype(BF16)
    kvb_out[:, 256:384] = k_win.astype(BF16)
    kvb_out[:, 384:512] = v_win.astype(BF16)

    cx = _dot(xn, w_ref[:, _C_CX:_C_CX + 512])
    cc = _dot(xn, w_ref[:, _C_CC:_C_CC + 512])
    cb = _dot(xn, w_ref[:, _C_CB:_C_CB + 512])
    u = cc * cx
    w0, w1, w2 = cw_ref[0:1, :], cw_ref[1:2, :], cw_ref[2:3, :]
    if sample:
        z = cb * (s0_ref[...] * w0 + s1_ref[...] * w1 + u * w2)
        u_out[...] = u
    else:
        @pl.when(pl.program_id(0) % tiles_per_seq == 0)
        def _():
            carry_ref[...] = jnp.zeros_like(carry_ref)
        prev = carry_ref[...]
        row = lax.broadcasted_iota(jnp.int32, u.shape, 0)
        u1 = jnp.where(row == 0, prev[7:8, :], pltpu.roll(u, 1, 0))
        u2 = jnp.where(row == 0, prev[6:7, :],
                       jnp.where(row == 1, prev[7:8, :], pltpu.roll(u, 2, 0)))
        z = cb * (u2 * w0 + u1 * w1 + u * w2)
        tail = u[tm - 8:tm, :]
        carry_ref[...] = tail
        u_out[...] = tail
    z_out[...] = z.astype(BF16)

    mq = _dot(xn, w_ref[:, _C_MQ:_C_MQ + 512])
    mq_out[...] = _group_rms(mq, e128_ref[...], MEM_HEAD_DIM, mqg_ref[...]).astype(BF16)

    ng_out[...] = jax.nn.sigmoid(_dot(xn, w_ref[:, _C_NG:_C_NG + 128]))


def _prep_call(x, pw, rope_tab, sample, seq_len, state=None):
    n = x.shape[0]
    tm = n if sample else TM
    tps = 1 if sample else seq_len // tm
    grid = (n // tm,)
    tok = lambda w: pl.BlockSpec((tm, w), lambda i: (i, 0))
    in_specs = [tok(D_MODEL), _const_spec((1, D_MODEL)), _const_spec((D_MODEL, _N_A)),
                _const_spec((1, 512)), _const_spec((2, LANES)), _const_spec((1, 512)),
                pl.BlockSpec((3, tm, LANES), lambda i: (0, i % tps, 0)),
                _const_spec((512, 512)), _const_spec((512, 512)), _const_spec((CONV_W, 512))]
    args = [x, pw['nmix'], pw['w_a'], pw['q_gain'], pw['k_gain'], pw['mq_gain'], rope_tab,
            pw['e64'], pw['e128'], pw['conv_w']]
    scratch = []
    if sample:
        in_specs += [tok(512), tok(512)]
        args += [state[:, 0, :], state[:, 1, :]]
        u_shape, u_spec = (n, 512), tok(512)
    else:
        scratch = [pltpu.VMEM((8, 512), F32)]
        u_shape, u_spec = (n // tm * 8, 512), pl.BlockSpec((8, 512), lambda i: (i, 0))
    out_shape = [jax.ShapeDtypeStruct((n, 512), BF16), jax.ShapeDtypeStruct((n, 512), BF16),
                 jax.ShapeDtypeStruct((n, 512), F32), jax.ShapeDtypeStruct((n, 256), F32),
                 jax.ShapeDtypeStruct((n, 512), BF16), jax.ShapeDtypeStruct((n, 512), BF16),
                 jax.ShapeDtypeStruct(u_shape, F32), jax.ShapeDtypeStruct((n, 512), BF16),
                 jax.ShapeDtypeStruct((n, LANES), F32)]
    out_specs = [tok(512), tok(512), tok(512), tok(256), tok(512), tok(512), u_spec, tok(512),
                 tok(LANES)]
    return pl.pallas_call(
        functools.partial(_prep_kernel, sample, tps),
        grid=grid, in_specs=in_specs, out_specs=out_specs, out_shape=out_shape,
        scratch_shapes=scratch,
        compiler_params=pltpu.CompilerParams(dimension_semantics=("arbitrary",),
                                             vmem_limit_bytes=VMEM_LIMIT),
        name="prep_sample" if sample else "prep_prompt",
    )(*args)


def _compress(load_xs, n_sub, w1_ref, pe_ref, w2_ref, kg_ref, e64k):
    outs = []
    for kv in range(2):
        acc = [None, None]
        for s in range(0, CMP_STRIDE, 2):
            xa, xb = load_xs(kv, s), load_xs(kv, s + 1)
            for r in range(CMP_RATIO):
                idx = (kv * CMP_RATIO + r) * CMP_STRIDE + s
                lhs = jnp.concatenate([(xa + pe_ref[idx:idx + 1, :]).astype(BF16),
                                       (xb + pe_ref[idx + 1:idx + 2, :]).astype(BF16)], axis=1)
                d = _dot(lhs, w1_ref[idx // 2])
                acc[r] = d if acc[r] is None else acc[r] + d
        h = acc[0] + pltpu.roll(acc[1], n_sub - 1, 0)
        o = _dot(jax.nn.silu(h).astype(BF16), w2_ref[kv])
        outs.append(o)
    kc = _group_rms(outs[0], e64k, HEAD_DIM, kg_ref[...])
    return kc, outs[1]


def _compress_prompt_kernel(xk_ref, xv_ref, w1_ref, pe_ref, w2_ref, kg_ref, e64_ref,
                            kc_out, vc_out):
    n_sub = kc_out.shape[1]
    x_refs = (xk_ref, xv_ref)
    kc, vc = _compress(lambda kv, s: x_refs[kv][0, pl.ds(s, n_sub, stride=CMP_STRIDE), :],
                       n_sub, w1_ref, pe_ref, w2_ref, kg_ref, e64_ref[...])
    kc_out[0] = kc.astype(BF16)
    vc_out[0] = vc.astype(BF16)


def _compress_prompt_call(rows3, pw):
    b, t, _ = rows3.shape
    n_sub = t // CMP_STRIDE
    out = jax.ShapeDtypeStruct((b, n_sub, LANES), BF16)
    return pl.pallas_call(
        _compress_prompt_kernel,
        grid=(b,),
        in_specs=[pl.BlockSpec((1, t, LANES), lambda i: (i, 0, 0)),
                  pl.BlockSpec((1, t, LANES), lambda i: (i, 0, 1)),
                  _const_spec((CMP_RATIO * CMP_STRIDE, 2 * LANES, 2 * CMP_HIDDEN)),
                  _const_spec((2 * CMP_RATIO * CMP_STRIDE, LANES)),
                  _const_spec((2, 2 * CMP_HIDDEN, LANES)),
                  _const_spec((1, LANES)), _const_spec((LANES, LANES))],
        out_specs=[pl.BlockSpec((1, n_sub, LANES), lambda i: (i, 0, 0))] * 2,
        out_shape=[out, out],
        compiler_params=pltpu.CompilerParams(dimension_semantics=("arbitrary",),
                                             vmem_limit_bytes=VMEM_LIMIT),
        name="compress_prompt",
    )(rows3, rows3, pw['cmp_w1'], pw['cmp_pe'], pw['cmp_w2'], pw['kc_gain'], pw['e64k'])


def _flash_init(m_ref, l_ref, acc_ref):
    m_ref[...] = jnp.full(m_ref.shape, M_INIT, F32)
    l_ref[...] = jnp.zeros(l_ref.shape, F32)
    acc_ref[...] = jnp.zeros(acc_ref.shape, F32)


def _flash_step(s, v, m_ref, l_ref, acc_ref):
    m_prev = m_ref[...]
    m_next = jnp.maximum(m_prev, jnp.max(s, axis=1, keepdims=True))
    p = jnp.exp(s - jnp.concatenate([m_next] * (s.shape[1] // LANES), axis=1))
    alpha = jnp.exp(m_prev - m_next)
    l_ref[...] = alpha * l_ref[...] + jnp.sum(p, axis=1, keepdims=True)
    acc_ref[...] = alpha * acc_ref[...] + _dot(p.astype(BF16), v)
    m_ref[...] = m_next


def _take_top(score_t, n_take):
    nb = score_t.shape[0]
    jidx = lax.broadcasted_iota(jnp.int32, score_t.shape, 0).astype(F32)
    taken = jnp.zeros(score_t.shape, F32)
    s = score_t
    for _ in range(n_take):
        mx = jnp.max(s, axis=0, keepdims=True)
        first = jnp.min(jnp.where(s == mx, jidx, float(nb)), axis=0, keepdims=True)
        pick = jidx == first
        taken = jnp.where(pick, 1.0, taken)
        s = jnp.where(pick, -jnp.inf, s)
    return taken


def _nsa_prompt_kernel(q_ref, qr_ref, kvb_ref, kc_ref, vc_ref, ng_ref, ovt_ref, cm_ref, gx_ref,
                       tri_ref, o_ref, m_ref, l_ref, acc_ref, ob_ref, qa_ref):
    tq = q_ref.shape[1]
    n_cmp = kc_ref.shape[1]
    n_blk = kvb_ref.shape[1] // SLC_BLOCK
    i = pl.program_id(1)
    q0 = i * tq
    lane = lax.broadcasted_iota(jnp.int32, (tq, LANES), 1)
    low = lane < HEAD_DIM

    def stack(qb, g):
        keep = low if g == 0 else jnp.logical_not(low)
        parts = [jnp.where(keep, qb[:, LANES * j:LANES * (j + 1)], jnp.zeros((tq, LANES), BF16))
                 for j in range(GROUP)]
        return jnp.concatenate(parts, axis=0)

    q = q_ref[0]
    qr = qr_ref[0]
    kc = kc_ref[0]
    vc = vc_ref[0]

    rowc = lax.broadcasted_iota(jnp.int32, (tq, n_cmp), 0)
    colc = lax.broadcasted_iota(jnp.int32, (tq, n_cmp), 1)
    vis1 = jnp.where(colc * CMP_STRIDE + (CMP_LEN - 1) <= q0 + rowc, 1.0, 0.0)
    vis = jnp.concatenate([vis1] * GROUP, axis=0) > 0.5

    for g in range(N_KV):
        s = _dot_nt(stack(q, g), kc)
        sm = jnp.where(vis, s, NEG)
        e = jnp.where(vis, jnp.exp(sm - jnp.max(sm, axis=1, keepdims=True)), 0.0)
        lsum = jnp.sum(e, axis=1, keepdims=True)
        pr = e / jnp.where(lsum > 0.0, lsum, 1.0)
        ob_ref[g] = _dot(pr.astype(BF16), vc)
        imp = pr[0:tq]
        for hh in range(1, GROUP):
            imp = imp + pr[hh * tq:(hh + 1) * tq]

        ih, il = _split_bf16(imp)
        sc_t = (_dot_nt(ovt_ref[...], ih) + _dot_nt(ovt_ref[...], il))[0:n_blk]
        blk_t = lax.broadcasted_iota(jnp.int32, sc_t.shape, 0)
        tpos_t = q0 + lax.broadcasted_iota(jnp.int32, sc_t.shape, 1)
        qblk_t = tpos_t // SLC_BLOCK
        forced_t = (blk_t == 0) | (blk_t == qblk_t) | (blk_t == qblk_t - 1)
        elig_t = blk_t * SLC_BLOCK <= tpos_t
        rest_t = _take_top(jnp.where(elig_t & jnp.logical_not(forced_t), sc_t, -jnp.inf),
                           N_SELECT - 3)
        keep_t = elig_t & (forced_t | (rest_t > 0.5))
        drop_t = jnp.where(keep_t, 0.0, 1.0)
        if n_blk < LANES:
            drop_t = jnp.concatenate([drop_t, jnp.zeros((LANES - n_blk, tq), F32)], axis=0)
        drop = jnp.transpose(drop_t).astype(BF16)
        qa_ref[g, :, 0:LANES] = stack(qr, g)
        qa_ref[g, :, LANES:2 * LANES] = jnp.concatenate([drop] * GROUP, axis=0)

    def flash(g):
        return m_ref.at[g], l_ref.at[g], acc_ref.at[g]

    def tile4(bias):
        return jnp.concatenate([bias] * GROUP, axis=0)

    for g in range(N_KV):
        _flash_init(*flash(g))

    def slc_tile(k0, diagonal):
        ka = jnp.concatenate([kvb_ref[0, pl.ds(k0, tq), 0:LANES], cm_ref[pl.ds(k0, tq), :]],
                             axis=1)
        v = kvb_ref[0, pl.ds(k0, tq), LANES:2 * LANES]
        for g in range(N_KV):
            st = _dot_nt(qa_ref[g], ka)
            if diagonal:
                st = st + tile4(tri_ref[0])
            _flash_step(st, v, *flash(g))

    def slc_body(kk, carry):
        slc_tile(pl.multiple_of(kk * tq, tq), False)
        return carry

    lax.fori_loop(0, i, slc_body, 0)
    slc_tile(pl.multiple_of(q0, tq), True)
    for g in range(N_KV):
        ob_ref[N_KV + g] = acc_ref[g] / l_ref[g]

    for g in range(N_KV):
        _flash_init(*flash(g))

    def win_tile(k0, bias):
        k = kvb_ref[0, pl.ds(k0, tq), 2 * LANES:3 * LANES]
        v = kvb_ref[0, pl.ds(k0, tq), 3 * LANES:4 * LANES]
        for g in range(N_KV):
            st = _dot_nt(qa_ref[g, :, 0:LANES], k)
            if bias is not None:
                st = st + tile4(tri_ref[bias])
            _flash_step(st, v, *flash(g))

    win_tile(pl.multiple_of(q0, tq), 0)
    n_back = WINDOW // tq
    for back in range(1, n_back + 1):
        @pl.when(i >= back)
        def _():
            win_tile(pl.multiple_of(q0 - back * tq, tq), 1 if back == n_back else None)
    for g in range(N_KV):
        ob_ref[2 * N_KV + g] = acc_ref[g] / l_ref[g]

    gh, gl = _split_bf16(ng_ref[0])
    for j in range(GROUP):
        tot = None
        for br in range(3):
            gx = gx_ref[:, (br * GROUP + j) * LANES:(br * GROUP + j + 1) * LANES]
            gate = _dot(gh, gx) + _dot(gl, gx)
            o_lo = ob_ref[br * N_KV + 0, j * tq:(j + 1) * tq, :]
            o_hi = ob_ref[br * N_KV + 1, j * tq:(j + 1) * tq, :]
            term = gate * jnp.where(low, o_lo, o_hi)
            tot = term if tot is None else tot + term
        o_ref[0, :, j * LANES:(j + 1) * LANES] = tot.astype(BF16)


def _nsa_prompt_call(q3, qr3, kvb3, kc, vc, ng3, pw):
    b, t, _ = q3.shape
    tq = min(TQ, t)
    n_cmp = kc.shape[1]
    blk = lambda w: pl.BlockSpec((1, tq, w), lambda bi, i: (bi, i, 0))
    per_b = lambda r, w: pl.BlockSpec((1, r, w), lambda bi, i: (bi, 0, 0))
    return pl.pallas_call(
        _nsa_prompt_kernel,
        grid=(b, t // tq),
        in_specs=[blk(512), blk(512), per_b(t, 512), per_b(n_cmp, LANES), per_b(n_cmp, LANES),
                  blk(LANES), _const_spec((LANES, n_cmp)), _const_spec((t, LANES)),
                  _const_spec((LANES, 3 * GROUP * LANES)), _const_spec((2, tq, tq))],
        out_specs=blk(512),
        out_shape=jax.ShapeDtypeStruct((b, t, 512), BF16),
        scratch_shapes=[pltpu.VMEM((N_KV, GROUP * tq, LANES), F32)] * 3
        + [pltpu.VMEM((3 * N_KV, GROUP * tq, LANES), F32),
           pltpu.VMEM((N_KV, GROUP * tq, 2 * LANES), BF16)],
        compiler_params=pltpu.CompilerParams(dimension_semantics=("arbitrary", "arbitrary"),
                                             vmem_limit_bytes=VMEM_LIMIT),
        name="nsa_prompt",
    )(q3, qr3, kvb3, kc, vc, ng3, pw['ov_p'], pw['cm_p'], pw['gate_x'],
      jnp.asarray(_band_bias(tq), F32))


def _memkv_kernel(mem_ref, nm_ref, w_ref, kg_ref, kv_out):
    x = mem_ref[0]
    ms = jnp.sum(x * x, axis=-1, keepdims=True) * (1.0 / D_MODEL)
    xn = (x * lax.rsqrt(ms + EPS) * nm_ref[...]).astype(BF16)
    kv = _dot(xn, w_ref[...])
    for h in range(MEM_HEADS):
        k = kv[:, h * LANES:(h + 1) * LANES]
        kms = jnp.sum(k * k, axis=-1, keepdims=True) * (1.0 / MEM_HEAD_DIM)
        kv_out[0, :, h * LANES:(h + 1) * LANES] = k * lax.rsqrt(kms + EPS) * kg_ref[...]
    kv_out[0, :, BRANCH_W:2 * BRANCH_W] = kv[:, BRANCH_W:2 * BRANCH_W]


def _memkv_call(mem, pw):
    b, m, _ = mem.shape
    return pl.pallas_call(
        _memkv_kernel,
        grid=(b,),
        in_specs=[pl.BlockSpec((1, m, D_MODEL), lambda i: (i, 0, 0)), _const_spec((1, D_MODEL)),
                  _const_spec((D_MODEL, 2 * BRANCH_W)), _const_spec((1, LANES))],
        out_specs=pl.BlockSpec((1, m, 2 * BRANCH_W), lambda i: (i, 0, 0)),
        out_shape=jax.ShapeDtypeStruct((b, m, 2 * BRANCH_W), F32),
        compiler_params=pltpu.CompilerParams(dimension_semantics=("arbitrary",),
                                             vmem_limit_bytes=VMEM_LIMIT),
        name="mem_kv",
    )(mem, pw['nmem'], pw['w_mem'], pw['mk_gain'])


def _mem_attend(mq, kv_ref):
    outs = []
    for h in range(MEM_HEADS):
        k = kv_ref[:, h * LANES:(h + 1) * LANES].astype(BF16)
        v = kv_ref[:, BRANCH_W + h * LANES:BRANCH_W + (h + 1) * LANES].astype(BF16)
        s = _dot_nt(mq[:, h * LANES:(h + 1) * LANES], k) * MEM_HEAD_DIM ** -0.5
        e = jnp.exp(s - jnp.max(s, axis=1, keepdims=True))
        pr = e / jnp.sum(e, axis=1, keepdims=True)
        outs.append(_dot(pr.astype(BF16), v))
    return jnp.concatenate(outs, axis=1)


def _mem_attn_kernel(mq_ref, kv_ref, o_ref):
    o_ref[0] = _mem_attend(mq_ref[0], kv_ref.at[0]).astype(BF16)


def _mem_attn_call(mq3, mkv):
    b, t, _ = mq3.shape
    tq = min(TM, t)
    return pl.pallas_call(
        _mem_attn_kernel,
        grid=(b, t // tq),
        in_specs=[pl.BlockSpec((1, tq, 512), lambda bi, i: (bi, i, 0)),
                  pl.BlockSpec((1, MEM_LEN, 2 * BRANCH_W), lambda bi, i: (bi, 0, 0))],
        out_specs=pl.BlockSpec((1, tq, 512), lambda bi, i: (bi, i, 0)),
        out_shape=jax.ShapeDtypeStruct((b, t, 512), BF16),
        compiler_params=pltpu.CompilerParams(dimension_semantics=("arbitrary", "arbitrary"),
                                             vmem_limit_bytes=VMEM_LIMIT),
        name="mem_attn",
    )(mq3, mkv)


def _finish_kernel(x_ref, on_ref, z_ref, om_ref, nmix_ref, wmg_ref, wbr_ref, wout_ref, h_out):
    x = x_ref[...]
    ms = jnp.sum(x * x, axis=-1, keepdims=True) * (1.0 / D_MODEL)
    xn = (x * lax.rsqrt(ms + EPS) * nmix_ref[...]).astype(BF16)
    merged = None
    for n, br_ref in enumerate((on_ref, z_ref, om_ref)):
        gate = jax.nn.sigmoid(_dot(xn, wmg_ref[:, n * D_MODEL:(n + 1) * D_MODEL]))
        term = gate * _dot(br_ref[...], wbr_ref[n])
        merged = term if merged is None else merged + term
    h_out[...] = x + _dot(merged.astype(BF16), wout_ref[...])


def _finish_call(x, o_nsa, z, o_mem, pw):
    n = x.shape[0]
    tm = min(TM, n)
    tok = lambda w: pl.BlockSpec((tm, w), lambda i: (i, 0))
    return pl.pallas_call(
        _finish_kernel,
        grid=(n // tm,),
        in_specs=[tok(D_MODEL), tok(512), tok(512), tok(512), _const_spec((1, D_MODEL)),
                  _const_spec((D_MODEL, N_BRANCH * D_MODEL)),
                  _const_spec((N_BRANCH, BRANCH_W, D_MODEL)), _const_spec((D_MODEL, D_MODEL))],
        out_specs=tok(D_MODEL),
        out_shape=jax.ShapeDtypeStruct((n, D_MODEL), F32),
        compiler_params=pltpu.CompilerParams(dimension_semantics=("arbitrary",),
                                             vmem_limit_bytes=VMEM_LIMIT),
        name="finish",
    )(x, o_nsa, z, o_mem, pw['nmix'], pw['w_mg'], pw['w_br'], pw['w_out'])


def _ffn_kernel(h_ref, nf_ref, wgu_ref, wd_ref, y_out):
    h = h_ref[...]
    ms = jnp.sum(h * h, axis=-1, keepdims=True) * (1.0 / D_MODEL)
    hn = (h * lax.rsqrt(ms + EPS) * nf_ref[...]).astype(BF16)
    acc = h
    for c in range(D_FF // FF_CHUNK):
        g = _dot(hn, wgu_ref[:, c * FF_CHUNK:(c + 1) * FF_CHUNK])
        u = _dot(hn, wgu_ref[:, D_FF + c * FF_CHUNK:D_FF + (c + 1) * FF_CHUNK])
        a = (jax.nn.silu(g) * u).astype(BF16)
        acc = acc + _dot(a, wd_ref[c * FF_CHUNK:(c + 1) * FF_CHUNK, :])
    y_out[...] = acc


def _ffn_call(h, pw):
    n = h.shape[0]
    tm = min(TM, n)
    tok = pl.BlockSpec((tm, D_MODEL), lambda i: (i, 0))
    return pl.pallas_call(
        _ffn_kernel,
        grid=(n // tm,),
        in_specs=[tok, _const_spec((1, D_MODEL)), _const_spec((D_MODEL, 2 * D_FF)),
                  _const_spec((D_FF, D_MODEL))],
        out_specs=tok,
        out_shape=jax.ShapeDtypeStruct((n, D_MODEL), F32),
        compiler_params=pltpu.CompilerParams(dimension_semantics=("arbitrary",),
                                             vmem_limit_bytes=VMEM_LIMIT),
        name="ffn",
    )(h, pw['nffn'], pw['w_gu'], pw['w_down'])


def _nsa_sample_kernel(layer_base, pt_ref, pool_ref, q_ref, qr_ref, rows_ref, wnew_ref,
                       wbuf_ref, mq_ref, mkv_ref, ng_ref, w1_ref, pe_ref, w2_ref, kg_ref, e64_ref,
                       ov_ref, gx_ref,
                       o_ref, wout_ref, om_ref,
                       tbuf_ref, kraw_ref, vraw_ref, gk_ref, gv_ref, idx_vm, idx_sm, cmp_sem,
                       gat_sem, idx_sem):
    b = pl.program_id(0)
    nb = pl.num_programs(0)
    n_pages = pt_ref.shape[1]
    n_past = n_pages * PAGE_SIZE
    n_sub = n_past // CMP_STRIDE
    n_gather = gk_ref.shape[1]
    q_blk = n_past // SLC_BLOCK

    def cmp_copies(bb, slot, j):
        page = layer_base + pt_ref[bb, j]
        return [pltpu.make_async_copy(pool_ref.at[page, kv], tbuf_ref.at[slot, kv, j],
                                      cmp_sem.at[slot]) for kv in range(2)]

    def start_cmp(bb, slot):
        def body(j, c):
            for cp in cmp_copies(bb, slot, j):
                cp.start()
            return c
        lax.fori_loop(0, n_pages, body, 0)

    def wait_cmp(bb, slot):
        def body(j, c):
            for cp in cmp_copies(bb, slot, j):
                cp.wait()
            return c
        lax.fori_loop(0, n_pages, body, 0)

    slot = b % 2

    @pl.when(b == 0)
    def _():
        start_cmp(b, slot)

    @pl.when(b + 1 < nb)
    def _():
        start_cmp(b + 1, 1 - slot)

    wait_cmp(b, slot)

    row8 = lax.broadcasted_iota(jnp.int32, (8, LANES), 0)
    lane8 = lax.broadcasted_iota(jnp.int32, (8, LANES), 1)
    low = lane8 < HEAD_DIM
    own_half = (row8 < GROUP) == low

    def stack8(qrow):
        out = jnp.zeros((8, LANES), F32)
        for j in range(GROUP):
            pj = jnp.broadcast_to(qrow[:, j * LANES:(j + 1) * LANES], (8, LANES))
            out = jnp.where((row8 & (GROUP - 1)) == j, pj, out)
        return jnp.where(own_half, out, 0.0).astype(BF16)

    q8 = stack8(q_ref[0])
    qr8 = stack8(qr_ref[0])

    raw_refs = (kraw_ref, vraw_ref)

    def to_rows(j, c):
        dst = pl.ds(pl.multiple_of(j * PAGE_SIZE, PAGE_SIZE), PAGE_SIZE)
        for kv in range(2):
            tile = tbuf_ref[slot, kv, j].reshape(N_KV * HEAD_DIM, PAGE_SIZE)
            raw_refs[kv][dst, :] = jnp.transpose(tile)
        return c
    lax.fori_loop(0, n_pages, to_rows, 0, unroll=8)

    kc, vc = _compress(
        lambda kv, s: raw_refs[kv][pl.ds(s, n_sub, stride=CMP_STRIDE), :],
        n_sub, w1_ref, pe_ref, w2_ref, kg_ref, e64_ref[...])
    s = _dot_nt(q8, kc.astype(BF16))
    col = lax.broadcasted_iota(jnp.int32, s.shape, 1)
    rowc = lax.broadcasted_iota(jnp.int32, s.shape, 0)
    vis = col < n_sub - CMP_RATIO + 1
    sm = jnp.where(vis, s, NEG)
    e = jnp.where(vis, jnp.exp(sm - jnp.max(sm, axis=1, keepdims=True)), 0.0)
    pr = e / jnp.sum(e, axis=1, keepdims=True)
    o_c = _dot(pr.astype(BF16), vc.astype(BF16))
    imp8 = jnp.zeros(s.shape, F32)
    for g in range(N_KV):
        imp_g = jnp.sum(jnp.where(rowc // GROUP == g, pr, 0.0), axis=0, keepdims=True)
        imp8 = jnp.where(rowc == g, jnp.broadcast_to(imp_g, s.shape), imp8)

    n_blk_pad = ov_ref.shape[1]
    sc = _dot_exact01(imp8, ov_ref[...])
    bl = lax.broadcasted_iota(jnp.int32, sc.shape, 1)
    forced = (bl == 0) | (bl == q_blk) | (bl == q_blk - 1)
    sc = jnp.where(forced, jnp.inf, jnp.where(bl <= q_blk, sc, -jnp.inf))
    sc_t = jnp.transpose(sc)
    blf = bl[0:1].astype(F32)
    idx8 = jnp.zeros((8, LANES), F32)
    for g in range(N_KV):
        colv = jnp.broadcast_to(sc_t[:, g:g + 1], (n_blk_pad, n_blk_pad))
        rowv = jnp.broadcast_to(sc[g:g + 1, :], (n_blk_pad, n_blk_pad))
        ii = lax.broadcasted_iota(jnp.int32, colv.shape, 0)
        jj = lax.broadcasted_iota(jnp.int32, colv.shape, 1)
        tie = jnp.where(ii < jj, 1.0, 0.0)
        beats = jnp.where(colv > rowv, 1.0, jnp.where(colv == rowv, tie, 0.0))
        rank = jnp.sum(beats, axis=0, keepdims=True)
        for k in range(N_SELECT):
            blk_k = jnp.sum(jnp.where(rank == float(k), blf, 0.0), axis=1, keepdims=True)
            idx8 = jnp.where((row8 == g) & (lane8 == k), blk_k, idx8)
    idx_vm[...] = idx8.astype(jnp.int32)
    idx_copy = pltpu.make_async_copy(idx_vm, idx_sm, idx_sem.at[0])
    idx_copy.start()

    w_len = wbuf_ref.shape[-1]
    wlane = lax.broadcasted_iota(jnp.int32, (HEAD_DIM, w_len), 1)
    wnew = wnew_ref[0]
    w_tiles = [[None] * N_KV for _ in range(2)]
    for kv in range(2):
        cols = jnp.transpose(jnp.broadcast_to(wnew[:, kv * LANES:(kv + 1) * LANES], (8, LANES)))
        for g in range(N_KV):
            col = cols[g * HEAD_DIM:(g + 1) * HEAD_DIM, 0:1]
            upd = jnp.where(wlane == w_len - 1, col,
                            pltpu.roll(wbuf_ref[0, 0, kv, g], w_len - 1, 1))
            wout_ref[0, kv, g] = upd
            w_tiles[kv][g] = upd.astype(BF16)
    o_w_parts = []
    for g in range(N_KV):
        s_w = _dot(qr8[:, g * HEAD_DIM:(g + 1) * HEAD_DIM], w_tiles[0][g])
        e_w = jnp.exp(s_w - jnp.max(s_w, axis=1, keepdims=True))
        pr_w = e_w / jnp.sum(e_w, axis=1, keepdims=True)
        o_w_parts.append(_dot_nt(pr_w.astype(BF16), w_tiles[1][g]))
    o_w = jnp.concatenate(o_w_parts, axis=1)

    idx_copy.wait()
    past_ranks = [k for k in range(N_SELECT) if k != 2]
    gathers, halves = [], [[], []]
    for g in range(N_KV):
        for kk, k in enumerate(past_ranks):
            blk = jnp.minimum(idx_sm[g, k], q_blk - 1)
            page = layer_base + pt_ref[b, lax.shift_right_logical(blk, 1)]
            halves[g].append(blk & 1)
            gathers.append(pltpu.make_async_copy(pool_ref.at[page, 2, g], gk_ref.at[g, kk],
                                                 gat_sem.at[0]))
            gathers.append(pltpu.make_async_copy(pool_ref.at[page, 3, g], gv_ref.at[g, kk],
                                                 gat_sem.at[0]))
    for cp in gathers:
        cp.start()

    mq = mq_ref[0]
    hrow = lax.broadcasted_iota(jnp.int32, (MEM_HEADS, LANES), 0)
    q4 = jnp.zeros((MEM_HEADS, LANES), F32)
    for h in range(MEM_HEADS):
        q4 = jnp.where(hrow == h, jnp.broadcast_to(mq[:, h * LANES:(h + 1) * LANES],
                                                   (MEM_HEADS, LANES)), q4)
    s_m = jnp.sum(mkv_ref[0, 0, :, 0] * q4[None], axis=-1, keepdims=True) * MEM_HEAD_DIM ** -0.5
    e_m = jnp.exp(s_m - jnp.max(s_m, axis=0, keepdims=True))
    p_m = e_m / jnp.sum(e_m, axis=0, keepdims=True)
    om_ref[0] = jnp.sum(p_m * mkv_ref[0, 0, :, 1], axis=0)

    for cp in gathers:
        cp.wait()
    rows = rows_ref[0]
    k_new = rows[:, 2 * LANES:3 * LANES].astype(BF16).astype(F32)
    v_new = rows[:, 3 * LANES:4 * LANES].astype(BF16).astype(F32)
    s_new = jnp.sum(qr8.astype(F32) * k_new, axis=1, keepdims=True)
    lane_blk = lane8 // SLC_BLOCK
    o_parts, p_new = [], []
    for g in range(N_KV):
        q_g = qr8[:, g * HEAD_DIM:(g + 1) * HEAD_DIM]
        ss = []
        for kk in range(n_gather):
            st = _dot(q_g, gk_ref[g, kk].astype(BF16))
            ss.append(jnp.where(lane_blk == halves[g][kk], st, NEG))
        s_past = jnp.concatenate(ss, axis=1)
        m = jnp.maximum(jnp.max(s_past, axis=1, keepdims=True), s_new)
        e_past = jnp.exp(s_past - m)
        e_new = jnp.exp(s_new - m)
        den = jnp.sum(e_past, axis=1, keepdims=True) + e_new
        pr_past = (e_past / den).astype(BF16)
        o_g = None
        for kk in range(n_gather):
            d = _dot_nt(pr_past[:, kk * LANES:(kk + 1) * LANES], gv_ref[g, kk].astype(BF16))
            o_g = d if o_g is None else o_g + d
        o_parts.append(o_g)
        p_new.append((e_new / den).astype(BF16).astype(F32))
    pr_new = jnp.where(row8[:, 0:1] < GROUP, p_new[0], p_new[1])
    o_s = jnp.concatenate(o_parts, axis=1) + pr_new * v_new

    gh, gl = _split_bf16(jnp.broadcast_to(ng_ref[0], (8, LANES)))
    tot = jnp.zeros((8, LANES), F32)
    for br, o8 in enumerate((o_c, o_s, o_w)):
        pair = jnp.where(low, o8, pltpu.roll(o8, GROUP, 0))
        gate8 = jnp.zeros((8, LANES), F32)
        for j in range(GROUP):
            gx = gx_ref[:, (br * GROUP + j) * LANES:(br * GROUP + j + 1) * LANES]
            gate8 = jnp.where(row8 == j, _dot(gh, gx) + _dot(gl, gx), gate8)
        tot = tot + gate8 * pair
    o_ref[0] = tot


def _nsa_sample_call(layer, page_table, pool5, q4, qr4, rows_s, win_new, win_buf, mq4, mkv,
                     ng_s, pw):
    nb, n_pages = page_table.shape
    n_past = n_pages * PAGE_SIZE
    n_sub = n_past // CMP_STRIDE
    n_phys = pool5.shape[0] // DEPTH
    w_buf = win_buf.shape[-1]
    n_blk_pad = pw['ov_s'].shape[1]
    per_b = lambda *shape: pl.BlockSpec((1,) + shape, lambda b, pt: (b,) + (0,) * len(shape))
    per_lb = lambda *shape: pl.BlockSpec((1, 1) + shape,
                                         lambda b, pt: (layer, b) + (0,) * len(shape))
    grid_spec = pltpu.PrefetchScalarGridSpec(
        num_scalar_prefetch=1,
        grid=(nb,),
        in_specs=[pl.BlockSpec(memory_space=pl.ANY),
                  per_b(1, 512), per_b(1, 512), per_b(1, 512), per_b(1, 256),
                  per_lb(2, N_KV, HEAD_DIM, w_buf), per_b(1, 512),
                  per_lb(MEM_LEN, 2, MEM_HEADS, MEM_HEAD_DIM),
                  per_b(1, LANES),
                  _const_spec((CMP_RATIO * CMP_STRIDE, 2 * LANES, 2 * CMP_HIDDEN)),
                  _const_spec((2 * CMP_RATIO * CMP_STRIDE, LANES)),
                  _const_spec((2, 2 * CMP_HIDDEN, LANES)),
                  _const_spec((1, LANES)), _const_spec((LANES, LANES)),
                  _const_spec((n_sub, n_blk_pad)),
                  _const_spec((LANES, 3 * GROUP * LANES))],
        out_specs=[per_b(8, LANES), per_b(2, N_KV, HEAD_DIM, w_buf),
                   per_b(MEM_HEADS, MEM_HEAD_DIM)],
        scratch_shapes=[
            pltpu.VMEM((2, 2, n_pages, N_KV, HEAD_DIM, PAGE_SIZE), F32),
            pltpu.VMEM((n_past, LANES), F32), pltpu.VMEM((n_past, LANES), F32),
            pltpu.VMEM((N_KV, N_SELECT - 1, HEAD_DIM, PAGE_SIZE), F32),
            pltpu.VMEM((N_KV, N_SELECT - 1, HEAD_DIM, PAGE_SIZE), F32),
            pltpu.VMEM((8, LANES), jnp.int32), pltpu.SMEM((8, LANES), jnp.int32),
            pltpu.SemaphoreType.DMA((2,)), pltpu.SemaphoreType.DMA((1,)),
            pltpu.SemaphoreType.DMA((1,))],
    )
    return pl.pallas_call(
        functools.partial(_nsa_sample_kernel, layer * n_phys),
        grid_spec=grid_spec,
        out_shape=[jax.ShapeDtypeStruct((nb, 8, LANES), F32),
                   jax.ShapeDtypeStruct((nb, 2, N_KV, HEAD_DIM, w_buf), F32),
                   jax.ShapeDtypeStruct((nb, MEM_HEADS, MEM_HEAD_DIM), F32)],
        compiler_params=pltpu.CompilerParams(dimension_semantics=("arbitrary",),
                                             vmem_limit_bytes=VMEM_LIMIT),
        name="nsa_sample",
    )(page_table, pool5, q4, qr4, rows_s, win_new, win_buf, mq4, mkv, ng_s,
      pw['cmp_w1'], pw['cmp_pe'], pw['cmp_w2'], pw['kc_gain'], pw['e64k'], pw['ov_s'],
      pw['gate_x'])


def _pair_perm():
    p = np.arange(BRANCH_W)
    j, ln = p // LANES, p % LANES
    head = np.where(ln < HEAD_DIM, j, GROUP + j)
    return head * HEAD_DIM + ln % HEAD_DIM


def _block_diag_ones(width, group):
    i = np.arange(width)
    return (i[:, None] // group == i[None, :] // group).astype(np.float32)


def _overlap(n_cmp_rows, n_cmp_valid, n_blk_cols):
    i = np.arange(n_cmp_rows)[:, None]
    j = np.arange(n_blk_cols)[None, :]
    ov = (i * CMP_STRIDE < (j + 1) * SLC_BLOCK) & (i * CMP_STRIDE + CMP_LEN > j * SLC_BLOCK)
    return (ov & (i < n_cmp_valid)).astype(np.float32)


def _block_mask_rows(n_keys):
    k = np.arange(n_keys)[:, None]
    j = np.arange(LANES)[None, :]
    return np.where(k // SLC_BLOCK == j, NEG, 0.0).astype(np.float32)


def _band_bias(tq):
    r = np.arange(tq)[:, None]
    c = np.arange(tq)[None, :]
    return np.stack([np.where(c <= r, 0.0, NEG), np.where(c > r, 0.0, NEG)]).astype(np.float32)


def _gate_expand():
    x = np.zeros((LANES, 3 * GROUP * LANES), np.float32)
    for br in range(3):
        for j in range(GROUP):
            for ln in range(LANES):
                head = j if ln < HEAD_DIM else GROUP + j
                x[br * N_HEADS + head, (br * GROUP + j) * LANES + ln] = 1.0
    return x


def _rope_tables(pos):
    half = ROT_DIM // 2
    inv = ROPE_THETA ** (-jnp.arange(half, dtype=F32) / half)
    ang = pos.astype(F32)[:, None] * inv[None, :]
    cos, sin = jnp.cos(ang), jnp.sin(ang)
    n = pos.shape[0]
    one = jnp.ones((n, HEAD_DIM - ROT_DIM), F32)
    zero = jnp.zeros((n, HEAD_DIM - ROT_DIM), F32)
    zh = jnp.zeros((n, half), F32)
    c = jnp.concatenate([cos, cos, one], axis=1)
    s1 = jnp.concatenate([zh, sin, zero], axis=1)
    s2 = jnp.concatenate([-sin, zh, zero], axis=1)
    tab = jnp.stack([c, s1, s2])
    return jnp.concatenate([tab, tab], axis=2)


def _layer_params(l, seq_len, n_past, norm_mix, w_in, q_norm, k_norm, cmp_pe, cmp_w1, cmp_w2,
                  conv_w, norm_mem, w_mem_kv, mem_q_norm, mem_k_norm, w_branch, w_out, norm_ffn,
                  w_gate_up, w_down):
    perm = _pair_perm()
    wi = w_in[l]
    o_kv, o_ng = BRANCH_W, BRANCH_W + 768
    o_cx = o_ng + 3 * N_HEADS
    o_cb, o_cc, o_mq, o_mg = o_cx + 512, o_cx + 1024, o_cx + 1536, o_cx + 2048
    w_a = jnp.concatenate([
        wi[:, perm], wi[:, o_kv:o_ng], wi[:, o_cx:o_cb], wi[:, o_cb:o_cc], wi[:, o_cc:o_mq],
        wi[:, o_mq:o_mg], wi[:, o_ng:o_cx], jnp.zeros((D_MODEL, LANES - 3 * N_HEADS), F32)],
        axis=1).astype(BF16)
    w1 = cmp_w1[l].reshape(2, CMP_RATIO, CMP_STRIDE, HEAD_DIM, CMP_HIDDEN)
    zw = jnp.zeros_like(w1)
    w1bd = jnp.concatenate([jnp.concatenate([w1, zw], axis=-1),
                            jnp.concatenate([zw, w1], axis=-1)], axis=-2)
    w2 = cmp_w2[l]
    z2 = jnp.zeros_like(w2)
    w2bd = jnp.concatenate([jnp.concatenate([w2, z2], axis=-1),
                            jnp.concatenate([z2, w2], axis=-1)], axis=-2)
    pe = cmp_pe[l].reshape(2 * CMP_RATIO * CMP_STRIDE, HEAD_DIM)
    n_sub_p = seq_len // CMP_STRIDE
    n_sub_s = n_past // CMP_STRIDE
    n_blk_s = -(-(n_past // SLC_BLOCK + 1) // LANES) * LANES
    wbr = w_branch[l]
    return dict(
        nmix=norm_mix[l][None, :], w_a=w_a, w_mg=wi[:, o_mg:].astype(BF16),
        q_gain=jnp.tile(q_norm[l], N_HEADS)[None, :],
        k_gain=jnp.stack([jnp.tile(k_norm[l, 1], N_KV), jnp.tile(k_norm[l, 2], N_KV)]),
        kc_gain=jnp.tile(k_norm[l, 0], N_KV)[None, :],
        mq_gain=jnp.tile(mem_q_norm[l], MEM_HEADS)[None, :],
        mk_gain=mem_k_norm[l][None, :],
        conv_w=conv_w[l],
        cmp_w1=w1bd.reshape(CMP_RATIO * CMP_STRIDE, 2 * LANES, 2 * CMP_HIDDEN).astype(BF16),
        cmp_pe=jnp.concatenate([pe, pe], axis=1),
        cmp_w2=w2bd.astype(BF16),
        nmem=norm_mem[l][None, :], w_mem=w_mem_kv[l].astype(BF16),
        w_br=jnp.concatenate([wbr[0][perm][None], wbr[1:]], axis=0).astype(BF16),
        w_out=w_out[l].astype(BF16),
        nffn=norm_ffn[l][None, :], w_gu=w_gate_up[l].astype(BF16), w_down=w_down[l].astype(BF16),
        e64=jnp.asarray(_block_diag_ones(512, HEAD_DIM), BF16),
        e64k=jnp.asarray(_block_diag_ones(LANES, HEAD_DIM), BF16),
        e128=jnp.asarray(_block_diag_ones(512, MEM_HEAD_DIM), BF16),
        ov_p=jnp.asarray(_overlap(n_sub_p, n_sub_p - CMP_RATIO + 1, LANES).T, BF16),
        cm_p=jnp.asarray(_block_mask_rows(seq_len), BF16),
        ov_s=jnp.asarray(_overlap(n_sub_s, n_sub_s - CMP_RATIO + 1, n_blk_s), BF16),
        gate_x=jnp.asarray(_gate_expand(), BF16),
    )


def kernel(x_prompt, x_sample, cache_nsa_kv, cache_win_kv, state_conv, cache_mem_kv, page_table,
           mem_prompt, norm_mix, w_in, q_norm, k_norm, cmp_pe, cmp_w1, cmp_w2, conv_w, norm_mem,
           w_mem_kv, mem_q_norm, mem_k_norm, w_branch, w_out, norm_ffn, w_gate_up, w_down):
    bp, t, _ = x_prompt.shape
    bs, ts, _ = x_sample.shape
    assert ts == 1 and t % TM == 0 and t % TQ == 0
    n_pages = page_table.shape[1]
    n_past = n_pages * PAGE_SIZE
    w_buf = cache_win_kv.shape[2]
    assert w_buf == WINDOW and n_past >= WINDOW and (n_past // SLC_BLOCK) == LANES

    xp = x_prompt.reshape(bp * t, D_MODEL)
    xs = x_sample.reshape(bs, D_MODEL)
    rope_p = _rope_tables(jnp.arange(t))
    rope_s = _rope_tables(jnp.full((bs,), n_past, jnp.int32))
    n_phys = cache_nsa_kv.shape[1]
    win_t = jnp.transpose(cache_win_kv, (0, 1, 3, 4, 5, 2))
    pool5 = jnp.transpose(cache_nsa_kv, (0, 1, 3, 4, 5, 2)).reshape(
        DEPTH * n_phys, 4, N_KV, HEAD_DIM, PAGE_SIZE)

    rows_p, rows_s, win_p, win_s, conv_p, conv_s, mem_p = [], [], [], [], [], [], []
    for l in range(DEPTH):
        pw = _layer_params(l, t, n_past, norm_mix, w_in, q_norm, k_norm, cmp_pe, cmp_w1, cmp_w2,
                           conv_w, norm_mem, w_mem_kv, mem_q_norm, mem_k_norm, w_branch, w_out,
                           norm_ffn, w_gate_up, w_down)
        q, qr, rows, win, kvb, z, utail, mq, ng = _prep_call(xp, pw, rope_p, False, t)
        rows3 = rows.reshape(bp, t, 512)
        kc, vc = _compress_prompt_call(rows3, pw)
        o_nsa = _nsa_prompt_call(q.reshape(bp, t, 512), qr.reshape(bp, t, 512),
                                 kvb.reshape(bp, t, 512), kc, vc, ng.reshape(bp, t, LANES), pw)
        mkv = _memkv_call(mem_prompt, pw)
        o_mem = _mem_attn_call(mq.reshape(bp, t, 512), mkv)
        h = _finish_call(xp, o_nsa.reshape(bp * t, 512), z, o_mem.reshape(bp * t, 512), pw)
        xp = _ffn_call(h, pw)
        rows_p.append(rows3.reshape(bp, t, 4, N_KV, HEAD_DIM))
        win_p.append(win.reshape(bp, t, 2, N_KV, HEAD_DIM)[:, t - min(WINDOW, t):])
        conv_p.append(utail.reshape(bp, t // TM, 8, 512)[:, -1, 8 - (CONV_W - 1):])
        mem_p.append(mkv.reshape(bp, MEM_LEN, 2, MEM_HEADS, MEM_HEAD_DIM))
        q, qr, rows, win, _, z, u, mq, ng = _prep_call(xs, pw, rope_s, True, 1,
                                                       state=state_conv[l])
        o_nsa, new_win, o_mem = _nsa_sample_call(
            l, page_table, pool5, q.astype(F32).reshape(bs, 1, 512),
            qr.astype(F32).reshape(bs, 1, 512), rows.reshape(bs, 1, 512),
            win.reshape(bs, 1, 256), win_t,
            mq.astype(F32).reshape(bs, 1, 512),
            cache_mem_kv, ng.reshape(bs, 1, LANES), pw)
        o_nsa = o_nsa[:, 0:GROUP, :].reshape(bs, 512).astype(BF16)
        o_mem = o_mem.reshape(bs, BRANCH_W).astype(BF16)
        h = _finish_call(xs, o_nsa, z, o_mem, pw)
        xs = _ffn_call(h, pw)
        rows_s.append(rows.reshape(bs, 1, 4, N_KV, HEAD_DIM))
        win_s.append(new_win)
        conv_s.append(jnp.stack([state_conv[l][:, 1, :], u], axis=1))

    return (xp.reshape(bp, t, D_MODEL), xs.reshape(bs, 1, D_MODEL), jnp.stack(rows_p),
            jnp.stack(rows_s), jnp.stack(win_p),
            jnp.transpose(jnp.stack(win_s), (0, 1, 5, 2, 3, 4)), jnp.stack(conv_p),
            jnp.stack(conv_s), jnp.stack(mem_p))
```

```python
import functools

import numpy as np
import jax
import jax.numpy as jnp
from jax import lax
from jax.experimental import pallas as pl
from jax.experimental.pallas import tpu as pltpu

F32 = jnp.float32
BF16 = jnp.bfloat16

D_MODEL = 1024
DEPTH = 2
PAST_LEN = 8192
PAGE_SIZE = 128
HEAD_DIM = 64
BRANCH_W = D_MODEL // 2
N_HEADS = BRANCH_W // HEAD_DIM
N_KV = N_HEADS // 4
GROUP = N_HEADS // N_KV
ROT_DIM = HEAD_DIM // 4
ROPE_THETA = 500000.0
CMP_LEN = 32
CMP_STRIDE = 16
CMP_RATIO = CMP_LEN // CMP_STRIDE
CMP_HIDDEN = 2 * HEAD_DIM
SLC_BLOCK = 64
N_SELECT = 16
WINDOW = 512
CONV_W = 3
MEM_LEN = 256
MEM_HEADS = 4
MEM_HEAD_DIM = BRANCH_W // MEM_HEADS
N_BRANCH = 3
D_FF = ((8 * D_MODEL + 767) // 768) * 256
EPS = 1e-6

LANES = 128
VMEM_LIMIT = 56 * 1024 * 1024

TM = 512
TQ = 512
FF_CHUNK = 256
NEG = -1e30
M_INIT = -3e38

_C_Q = 0
_C_KV = 512
_C_CX = 1280
_C_CB = 1792
_C_CC = 2304
_C_MQ = 2816
_C_NG = 3328
_N_A = 3456


def _const_spec(shape):
    nd = len(shape)
    return pl.BlockSpec(shape, lambda *_: (0,) * nd, pipeline_mode=pl.Buffered(1))


def _dot(a, b):
    return jnp.dot(a, b, preferred_element_type=F32)


def _dot_nt(a, b):
    return lax.dot_general(a, b, (((1,), (1,)), ((), ())), preferred_element_type=F32)


def _split_bf16(v):
    hi = v.astype(BF16)
    lo = (v - hi.astype(F32)).astype(BF16)
    return hi, lo


def _dot_exact01(v, m01):
    hi, lo = _split_bf16(v)
    return _dot(hi, m01) + _dot(lo, m01)


def _group_rms(v, e01, group, gain):
    ms = _dot_exact01(v * v, e01) * (1.0 / group)
    return v * lax.rsqrt(ms + EPS) * gain


def _rope_cols(v, c, s1, s2):
    outs = []
    for j in range(v.shape[1] // LANES):
        col = v[:, j * LANES:(j + 1) * LANES]
        outs.append(col * c + pltpu.roll(col, ROT_DIM // 2, 1) * s1
                    + pltpu.roll(col, LANES - ROT_DIM // 2, 1) * s2)
    return outs[0] if len(outs) == 1 else jnp.concatenate(outs, axis=1)


def _prep_kernel(sample, tiles_per_seq, *refs):
    if sample:
        (x_ref, nmix_ref, w_ref, qg_ref, kg_ref, mqg_ref, rope_ref, e64_ref, e128_ref, cw_ref,
         s0_ref, s1_ref,
         q_out, qr_out, rows_out, win_out, kvb_out, z_out, u_out, mq_out, ng_out) = refs
    else:
        (x_ref, nmix_ref, w_ref, qg_ref, kg_ref, mqg_ref, rope_ref, e64_ref, e128_ref, cw_ref,
         q_out, qr_out, rows_out, win_out, kvb_out, z_out, u_out, mq_out, ng_out,
         carry_ref) = refs
    tm = x_ref.shape[0]
    x = x_ref[...]
    ms = jnp.sum(x * x, axis=-1, keepdims=True) * (1.0 / D_MODEL)
    xn = (x * lax.rsqrt(ms + EPS) * nmix_ref[...]).astype(BF16)

    rc, rs1, rs2 = rope_ref[0], rope_ref[1], rope_ref[2]
    e64 = e64_ref[...]

    q = _dot(xn, w_ref[:, _C_Q:_C_Q + 512])
    qn = _group_rms(q, e64, HEAD_DIM, qg_ref[...])
    q_out[...] = (qn * HEAD_DIM ** -0.5).astype(BF16)
    qr_out[...] = (_rope_cols(qn, rc, rs1, rs2) * HEAD_DIM ** -0.5).astype(BF16)

    kv = _dot(xn, w_ref[:, _C_KV:_C_KV + 768])
    e64k = e64_ref[0:LANES, 0:LANES]
    k_slc = _rope_cols(_group_rms(kv[:, 256:384], e64k, HEAD_DIM, kg_ref[0:1, :]), rc, rs1, rs2)
    k_win = _rope_cols(_group_rms(kv[:, 512:640], e64k, HEAD_DIM, kg_ref[1:2, :]), rc, rs1, rs2)
    v_slc = kv[:, 384:512]
    v_win = kv[:, 640:768]
    rows_out[:, 0:256] = kv[:, 0:256]
    rows_out[:, 256:384] = k_slc
    rows_out[:, 384:512] = v_slc
    win_out[:, 0:128] = k_win
    win_out[:, 128:256] = v_win
    kvb_out[:, 0:128] = k_slc.astype(BF16)
    kvb_out[:, 128:256] = v_slc.astype(BF16)
    kvb_out[:, 256:384] = k_win.astype(BF16)
    kvb_out[:, 384:512] = v_win.astype(BF16)

    cx = _dot(xn, w_ref[:, _C_CX:_C_CX + 512])
    cc = _dot(xn, w_ref[:, _C_CC:_C_CC + 512])
    cb = _dot(xn, w_ref[:, _C_CB:_C_CB + 512])
    u = cc * cx
    w0, w1, w2 = cw_ref[0:1, :], cw_ref[1:2, :], cw_ref[2:3, :]
    if sample:
        z = cb * (s0_ref[...] * w0 + s1_ref[...] * w1 + u * w2)
        u_out[...] = u
    else:
        @pl.when(pl.program_id(0) % tiles_per_seq == 0)
        def _():
            carry_ref[...] = jnp.zeros_like(carry_ref)
        prev = carry_ref[...]
        row = lax.broadcasted_iota(jnp.int32, u.shape, 0)
        u1 = jnp.where(row == 0, prev[7:8, :], pltpu.roll(u, 1, 0))
        u2 = jnp.where(row == 0, prev[6:7, :],
                       jnp.where(row == 1, prev[7:8, :], pltpu.roll(u, 2, 0)))
        z = cb * (u2 * w0 + u1 * w1 + u * w2)
        tail = u[tm - 8:tm, :]
        carry_ref[...] = tail
        u_out[...] = tail
    z_out[...] = z.astype(BF16)

    mq = _dot(xn, w_ref[:, _C_MQ:_C_MQ + 512])
    mq_out[...] = _group_rms(mq, e128_ref[...], MEM_HEAD_DIM, mqg_ref[...]).astype(BF16)

    ng_out[...] = jax.nn.sigmoid(_dot(xn, w_ref[:, _C_NG:_C_NG + 128]))


def _prep_call(x, pw, rope_tab, sample, seq_len, state=None):
    n = x.shape[0]
    tm = n if sample else TM
    tps = 1 if sample else seq_len // tm
    grid = (n // tm,)
    tok = lambda w: pl.BlockSpec((tm, w), lambda i: (i, 0))
    in_specs = [tok(D_MODEL), _const_spec((1, D_MODEL)), _const_spec((D_MODEL, _N_A)),
                _const_spec((1, 512)), _const_spec((2, LANES)), _const_spec((1, 512)),
                pl.BlockSpec((3, tm, LANES), lambda i: (0, i % tps, 0)),
                _const_spec((512, 512)), _const_spec((512, 512)), _const_spec((CONV_W, 512))]
    args = [x, pw['nmix'], pw['w_a'], pw['q_gain'], pw['k_gain'], pw['mq_gain'], rope_tab,
            pw['e64'], pw['e128'], pw['conv_w']]
    scratch = []
    if sample:
        in_specs += [tok(512), tok(512)]
        args += [state[:, 0, :], state[:, 1, :]]
        u_shape, u_spec = (n, 512), tok(512)
    else:
        scratch = [pltpu.VMEM((8, 512), F32)]
        u_shape, u_spec = (n // tm * 8, 512), pl.BlockSpec((8, 512), lambda i: (i, 0))
    out_shape = [jax.ShapeDtypeStruct((n, 512), BF16), jax.ShapeDtypeStruct((n, 512), BF16),
                 jax.ShapeDtypeStruct((n, 512), F32), jax.ShapeDtypeStruct((n, 256), F32),
                 jax.ShapeDtypeStruct((n, 512), BF16), jax.ShapeDtypeStruct((n, 512), BF16),
                 jax.ShapeDtypeStruct(u_shape, F32), jax.ShapeDtypeStruct((n, 512), BF16),
                 jax.ShapeDtypeStruct((n, LANES), F32)]
    out_specs = [tok(512), tok(512), tok(512), tok(256), tok(512), tok(512), u_spec, tok(512),
                 tok(LANES)]
    return pl.pallas_call(
        functools.partial(_prep_kernel, sample, tps),
        grid=grid, in_specs=in_specs, out_specs=out_specs, out_shape=out_shape,
        scratch_shapes=scratch,
        compiler_params=pltpu.CompilerParams(dimension_semantics=("arbitrary",),
                                             vmem_limit_bytes=VMEM_LIMIT),
        name="prep_sample" if sample else "prep_prompt",
    )(*args)


def _compress(load_xs, n_sub, w1_ref, pe_ref, w2_ref, kg_ref, e64k):
    outs = []
    for kv in range(2):
        acc = [None, None]
        for s in range(0, CMP_STRIDE, 2):
            xa, xb = load_xs(kv, s), load_xs(kv, s + 1)
            for r in range(CMP_RATIO):
                idx = (kv * CMP_RATIO + r) * CMP_STRIDE + s
                lhs = jnp.concatenate([(xa + pe_ref[idx:idx + 1, :]).astype(BF16),
                                       (xb + pe_ref[idx + 1:idx + 2, :]).astype(BF16)], axis=1)
                d = _dot(lhs, w1_ref[idx // 2])
                acc[r] = d if acc[r] is None else acc[r] + d
        h = acc[0] + pltpu.roll(acc[1], n_sub - 1, 0)
        o = _dot(jax.nn.silu(h).astype(BF16), w2_ref[kv])
        outs.append(o)
    kc = _group_rms(outs[0], e64k, HEAD_DIM, kg_ref[...])
    return kc, outs[1]


def _compress_prompt_kernel(xk_ref, xv_ref, w1_ref, pe_ref, w2_ref, kg_ref, e64_ref,
                            kc_out, vc_out):
    n_sub = kc_out.shape[1]
    x_refs = (xk_ref, xv_ref)
    kc, vc = _compress(lambda kv, s: x_refs[kv][0, pl.ds(s, n_sub, stride=CMP_STRIDE), :],
                       n_sub, w1_ref, pe_ref, w2_ref, kg_ref, e64_ref[...])
    kc_out[0] = kc.astype(BF16)
    vc_out[0] = vc.astype(BF16)


def _compress_prompt_call(rows3, pw):
    b, t, _ = rows3.shape
    n_sub = t // CMP_STRIDE
    out = jax.ShapeDtypeStruct((b, n_sub, LANES), BF16)
    return pl.pallas_call(
        _compress_prompt_kernel,
        grid=(b,),
        in_specs=[pl.BlockSpec((1, t, LANES), lambda i: (i, 0, 0)),
                  pl.BlockSpec((1, t, LANES), lambda i: (i, 0, 1)),
                  _const_spec((CMP_RATIO * CMP_STRIDE, 2 * LANES, 2 * CMP_HIDDEN)),
                  _const_spec((2 * CMP_RATIO * CMP_STRIDE, LANES)),
                  _const_spec((2, 2 * CMP_HIDDEN, LANES)),
                  _const_spec((1, LANES)), _const_spec((LANES, LANES))],
        out_specs=[pl.BlockSpec((1, n_sub, LANES), lambda i: (i, 0, 0))] * 2,
        out_shape=[out, out],
        compiler_params=pltpu.CompilerParams(dimension_semantics=("arbitrary",),
                                             vmem_limit_bytes=VMEM_LIMIT),
        name="compress_prompt",
    )(rows3, rows3, pw['cmp_w1'], pw['cmp_pe'], pw['cmp_w2'], pw['kc_gain'], pw['e64k'])


def _flash_init(m_ref, l_ref, acc_ref):
    m_ref[...] = jnp.full(m_ref.shape, M_INIT, F32)
    l_ref[...] = jnp.zeros(l_ref.shape, F32)
    acc_ref[...] = jnp.zeros(acc_ref.shape, F32)


def _flash_step(s, v, m_ref, l_ref, acc_ref):
    m_prev = m_ref[...]
    m_next = jnp.maximum(m_prev, jnp.max(s, axis=1, keepdims=True))
    p = jnp.exp(s - jnp.concatenate([m_next] * (s.shape[1] // LANES), axis=1))
    alpha = jnp.exp(m_prev - m_next)
    l_ref[...] = alpha * l_ref[...] + jnp.sum(p, axis=1, keepdims=True)
    acc_ref[...] = alpha * acc_ref[...] + _dot(p.astype(BF16), v)
    m_ref[...] = m_next


def _take_top(score_t, n_take):
    nb = score_t.shape[0]
    jidx = lax.broadcasted_iota(jnp.int32, score_t.shape, 0).astype(F32)
    taken = jnp.zeros(score_t.shape, F32)
    s = score_t
    for _ in range(n_take):
        mx = jnp.max(s, axis=0, keepdims=True)
        first = jnp.min(jnp.where(s == mx, jidx, float(nb)), axis=0, keepdims=True)
        pick = jidx == first
        taken = jnp.where(pick, 1.0, taken)
        s = jnp.where(pick, -jnp.inf, s)
    return taken


def _nsa_prompt_kernel(q_ref, qr_ref, kvb_ref, kc_ref, vc_ref, ng_ref, ovt_ref, cm_ref, gx_ref,
                       tri_ref, o_ref, m_ref, l_ref, acc_ref, ob_ref, qa_ref):
    tq = q_ref.shape[1]
    n_cmp = kc_ref.shape[1]
    n_blk = kvb_ref.shape[1] // SLC_BLOCK
    i = pl.program_id(1)
    q0 = i * tq
    lane = lax.broadcasted_iota(jnp.int32, (tq, LANES), 1)
    low = lane < HEAD_DIM

    def stack(qb, g):
        keep = low if g == 0 else jnp.logical_not(low)
        parts = [jnp.where(keep, qb[:, LANES * j:LANES * (j + 1)], jnp.zeros((tq, LANES), BF16))
                 for j in range(GROUP)]
        return jnp.concatenate(parts, axis=0)

    q = q_ref[0]
    qr = qr_ref[0]
    kc = kc_ref[0]
    vc = vc_ref[0]

    rowc = lax.broadcasted_iota(jnp.int32, (tq, n_cmp), 0)
    colc = lax.broadcasted_iota(jnp.int32, (tq, n_cmp), 1)
    vis1 = jnp.where(colc * CMP_STRIDE + (CMP_LEN - 1) <= q0 + rowc, 1.0, 0.0)
    vis = jnp.concatenate([vis1] * GROUP, axis=0) > 0.5

    for g in range(N_KV):
        s = _dot_nt(stack(q, g), kc)
        sm = jnp.where(vis, s, NEG)
        e = jnp.where(vis, jnp.exp(sm - jnp.max(sm, axis=1, keepdims=True)), 0.0)
        lsum = jnp.sum(e, axis=1, keepdims=True)
        pr = e / jnp.where(lsum > 0.0, lsum, 1.0)
        ob_ref[g] = _dot(pr.astype(BF16), vc)
        imp = pr[0:tq]
        for hh in range(1, GROUP):
            imp = imp + pr[hh * tq:(hh + 1) * tq]

        ih, il = _split_bf16(imp)
        sc_t = (_dot_nt(ovt_ref[...], ih) + _dot_nt(ovt_ref[...], il))[0:n_blk]
        blk_t = lax.broadcasted_iota(jnp.int32, sc_t.shape, 0)
        tpos_t = q0 + lax.broadcasted_iota(jnp.int32, sc_t.shape, 1)
        qblk_t = tpos_t // SLC_BLOCK
        forced_t = (blk_t == 0) | (blk_t == qblk_t) | (blk_t == qblk_t - 1)
        elig_t = blk_t * SLC_BLOCK <= tpos_t
        rest_t = _take_top(jnp.where(elig_t & jnp.logical_not(forced_t), sc_t, -jnp.inf),
                           N_SELECT - 3)
        keep_t = elig_t & (forced_t | (rest_t > 0.5))
        drop_t = jnp.where(keep_t, 0.0, 1.0)
        if n_blk < LANES:
            drop_t = jnp.concatenate([drop_t, jnp.zeros((LANES - n_blk, tq), F32)], axis=0)
        drop = jnp.transpose(drop_t).astype(BF16)
        qa_ref[g, :, 0:LANES] = stack(qr, g)
        qa_ref[g, :, LANES:2 * LANES] = jnp.concatenate([drop] * GROUP, axis=0)

    def flash(g):
        return m_ref.at[g], l_ref.at[g], acc_ref.at[g]

    def tile4(bias):
        return jnp.concatenate([bias] * GROUP, axis=0)

    for g in range(N_KV):
        _flash_init(*flash(g))

    def slc_tile(k0, diagonal):
        ka = jnp.concatenate([kvb_ref[0, pl.ds(k0, tq), 0:LANES], cm_ref[pl.ds(k0, tq), :]],
                             axis=1)
        v = kvb_ref[0, pl.ds(k0, tq), LANES:2 * LANES]
        for g in range(N_KV):
            st = _dot_nt(qa_ref[g], ka)
            if diagonal:
                st = st + tile4(tri_ref[0])
            _flash_step(st, v, *flash(g))

    def slc_body(kk, carry):
        slc_tile(pl.multiple_of(kk * tq, tq), False)
        return carry

    lax.fori_loop(0, i, slc_body, 0)
    slc_tile(pl.multiple_of(q0, tq), True)
    for g in range(N_KV):
        ob_ref[N_KV + g] = acc_ref[g] / l_ref[g]

    for g in range(N_KV):
        _flash_init(*flash(g))

    def win_tile(k0, bias):
        k = kvb_ref[0, pl.ds(k0, tq), 2 * LANES:3 * LANES]
        v = kvb_ref[0, pl.ds(k0, tq), 3 * LANES:4 * LANES]
        for g in range(N_KV):
            st = _dot_nt(qa_ref[g, :, 0:LANES], k)
            if bias is not None:
                st = st + tile4(tri_ref[bias])
            _flash_step(st, v, *flash(g))

    win_tile(pl.multiple_of(q0, tq), 0)
    n_back = WINDOW // tq
    for back in range(1, n_back + 1):
        @pl.when(i >= back)
        def _():
            win_tile(pl.multiple_of(q0 - back * tq, tq), 1 if back == n_back else None)
    for g in range(N_KV):
        ob_ref[2 * N_KV + g] = acc_ref[g] / l_ref[g]

    gh, gl = _split_bf16(ng_ref[0])
    for j in range(GROUP):
        tot = None
        for br in range(3):
            gx = gx_ref[:, (br * GROUP + j) * LANES:(br * GROUP + j + 1) * LANES]
            gate = _dot(gh, gx) + _dot(gl, gx)
            o_lo = ob_ref[br * N_KV + 0, j * tq:(j + 1) * tq, :]
            o_hi = ob_ref[br * N_KV + 1, j * tq:(j + 1) * tq, :]
            term = gate * jnp.where(low, o_lo, o_hi)
            tot = term if tot is None else tot + term
        o_ref[0, :, j * LANES:(j + 1) * LANES] = tot.astype(BF16)


def _nsa_prompt_call(q3, qr3, kvb3, kc, vc, ng3, pw):
    b, t, _ = q3.shape
    tq = min(TQ, t)
    n_cmp = kc.shape[1]
    blk = lambda w: pl.BlockSpec((1, tq, w), lambda bi, i: (bi, i, 0))
    per_b = lambda r, w: pl.BlockSpec((1, r, w), lambda bi, i: (bi, 0, 0))
    return pl.pallas_call(
        _nsa_prompt_kernel,
        grid=(b, t // tq),
        in_specs=[blk(512), blk(512), per_b(t, 512), per_b(n_cmp, LANES), per_b(n_cmp, LANES),
                  blk(LANES), _const_spec((LANES, n_cmp)), _const_spec((t, LANES)),
                  _const_spec((LANES, 3 * GROUP * LANES)), _const_spec((2, tq, tq))],
        out_specs=blk(512),
        out_shape=jax.ShapeDtypeStruct((b, t, 512), BF16),
        scratch_shapes=[pltpu.VMEM((N_KV, GROUP * tq, LANES), F32)] * 3
        + [pltpu.VMEM((3 * N_KV, GROUP * tq, LANES), F32),
           pltpu.VMEM((N_KV, GROUP * tq, 2 * LANES), BF16)],
        compiler_params=pltpu.CompilerParams(dimension_semantics=("arbitrary", "arbitrary"),
                                             vmem_limit_bytes=VMEM_LIMIT),
        name="nsa_prompt",
    )(q3, qr3, kvb3, kc, vc, ng3, pw['ov_p'], pw['cm_p'], pw['gate_x'],
      jnp.asarray(_band_bias(tq), F32))


def _memkv_kernel(mem_ref, nm_ref, w_ref, kg_ref, kv_out):
    x = mem_ref[0]
    ms = jnp.sum(x * x, axis=-1, keepdims=True) * (1.0 / D_MODEL)
    xn = (x * lax.rsqrt(ms + EPS) * nm_ref[...]).astype(BF16)
    kv = _dot(xn, w_ref[...])
    for h in range(MEM_HEADS):
        k = kv[:, h * LANES:(h + 1) * LANES]
        kms = jnp.sum(k * k, axis=-1, keepdims=True) * (1.0 / MEM_HEAD_DIM)
        kv_out[0, :, h * LANES:(h + 1) * LANES] = k * lax.rsqrt(kms + EPS) * kg_ref[...]
    kv_out[0, :, BRANCH_W:2 * BRANCH_W] = kv[:, BRANCH_W:2 * BRANCH_W]


def _memkv_call(mem, pw):
    b, m, _ = mem.shape
    return pl.pallas_call(
        _memkv_kernel,
        grid=(b,),
        in_specs=[pl.BlockSpec((1, m, D_MODEL), lambda i: (i, 0, 0)), _const_spec((1, D_MODEL)),
                  _const_spec((D_MODEL, 2 * BRANCH_W)), _const_spec((1, LANES))],
        out_specs=pl.BlockSpec((1, m, 2 * BRANCH_W), lambda i: (i, 0, 0)),
        out_shape=jax.ShapeDtypeStruct((b, m, 2 * BRANCH_W), F32),
        compiler_params=pltpu.CompilerParams(dimension_semantics=("arbitrary",),
                                             vmem_limit_bytes=VMEM_LIMIT),
        name="mem_kv",
    )(mem, pw['nmem'], pw['w_mem'], pw['mk_gain'])


def _mem_attend(mq, kv_ref):
    outs = []
    for h in range(MEM_HEADS):
        k = kv_ref[:, h * LANES:(h + 1) * LANES].astype(BF16)
        v = kv_ref[:, BRANCH_W + h * LANES:BRANCH_W + (h + 1) * LANES].astype(BF16)
        s = _dot_nt(mq[:, h * LANES:(h + 1) * LANES], k) * MEM_HEAD_DIM ** -0.5
        e = jnp.exp(s - jnp.max(s, axis=1, keepdims=True))
        pr = e / jnp.sum(e, axis=1, keepdims=True)
        outs.append(_dot(pr.astype(BF16), v))
    return jnp.concatenate(outs, axis=1)


def _mem_attn_kernel(mq_ref, kv_ref, o_ref):
    o_ref[0] = _mem_attend(mq_ref[0], kv_ref.at[0]).astype(BF16)


def _mem_attn_call(mq3, mkv):
    b, t, _ = mq3.shape
    tq = min(TM, t)
    return pl.pallas_call(
        _mem_attn_kernel,
        grid=(b, t // tq),
        in_specs=[pl.BlockSpec((1, tq, 512), lambda bi, i: (bi, i, 0)),
                  pl.BlockSpec((1, MEM_LEN, 2 * BRANCH_W), lambda bi, i: (bi, 0, 0))],
        out_specs=pl.BlockSpec((1, tq, 512), lambda bi, i: (bi, i, 0)),
        out_shape=jax.ShapeDtypeStruct((b, t, 512), BF16),
        compiler_params=pltpu.CompilerParams(dimension_semantics=("arbitrary", "arbitrary"),
                                             vmem_limit_bytes=VMEM_LIMIT),
        name="mem_attn",
    )(mq3, mkv)


def _finish_kernel(x_ref, on_ref, z_ref, om_ref, nmix_ref, wmg_ref, wbr_ref, wout_ref, h_out):
    x = x_ref[...]
    ms = jnp.sum(x * x, axis=-1, keepdims=True) * (1.0 / D_MODEL)
    xn = (x * lax.rsqrt(ms + EPS) * nmix_ref[...]).astype(BF16)
    merged = None
    for n, br_ref in enumerate((on_ref, z_ref, om_ref)):
        gate = jax.nn.sigmoid(_dot(xn, wmg_ref[:, n * D_MODEL:(n + 1) * D_MODEL]))
        term = gate * _dot(br_ref[...], wbr_ref[n])
        merged = term if merged is None else merged + term
    h_out[...] = x + _dot(merged.astype(BF16), wout_ref[...])


def _finish_call(x, o_nsa, z, o_mem, pw):
    n = x.shape[0]
    tm = min(TM, n)
    tok = lambda w: pl.BlockSpec((tm, w), lambda i: (i, 0))
    return pl.pallas_call(
        _finish_kernel,
        grid=(n // tm,),
        in_specs=[tok(D_MODEL), tok(512), tok(512), tok(512), _const_spec((1, D_MODEL)),
                  _const_spec((D_MODEL, N_BRANCH * D_MODEL)),
                  _const_spec((N_BRANCH, BRANCH_W, D_MODEL)), _const_spec((D_MODEL, D_MODEL))],
        out_specs=tok(D_MODEL),
        out_shape=jax.ShapeDtypeStruct((n, D_MODEL), F32),
        compiler_params=pltpu.CompilerParams(dimension_semantics=("arbitrary",),
                                             vmem_limit_bytes=VMEM_LIMIT),
        name="finish",
    )(x, o_nsa, z, o_mem, pw['nmix'], pw['w_mg'], pw['w_br'], pw['w_out'])


def _ffn_kernel(h_ref, nf_ref, wgu_ref, wd_ref, y_out):
    h = h_ref[...]
    ms = jnp.sum(h * h, axis=-1, keepdims=True) * (1.0 / D_MODEL)
    hn = (h * lax.rsqrt(ms + EPS) * nf_ref[...]).astype(BF16)
    acc = h
    for c in range(D_FF // FF_CHUNK):
        g = _dot(hn, wgu_ref[:, c * FF_CHUNK:(c + 1) * FF_CHUNK])
        u = _dot(hn, wgu_ref[:, D_FF + c * FF_CHUNK:D_FF + (c + 1) * FF_CHUNK])
        a = (jax.nn.silu(g) * u).astype(BF16)
        acc = acc + _dot(a, wd_ref[c * FF_CHUNK:(c + 1) * FF_CHUNK, :])
    y_out[...] = acc


def _ffn_call(h, pw):
    n = h.shape[0]
    tm = min(TM, n)
    tok = pl.BlockSpec((tm, D_MODEL), lambda i: (i, 0))
    return pl.pallas_call(
        _ffn_kernel,
        grid=(n // tm,),
        in_specs=[tok, _const_spec((1, D_MODEL)), _const_spec((D_MODEL, 2 * D_FF)),
                  _const_spec((D_FF, D_MODEL))],
        out_specs=tok,
        out_shape=jax.ShapeDtypeStruct((n, D_MODEL), F32),
        compiler_params=pltpu.CompilerParams(dimension_semantics=("arbitrary",),
                                             vmem_limit_bytes=VMEM_LIMIT),
        name="ffn",
    )(h, pw['nffn'], pw['w_gu'], pw['w_down'])


def _nsa_sample_kernel(layer_base, pt_ref, pool_ref, q_ref, qr_ref, rows_ref, wnew_ref,
                       wbuf_ref, mq_ref, mkv_ref, ng_ref, w1_ref, pe_ref, w2_ref, kg_ref, e64_ref,
                       ov_ref, gx_ref,
                       o_ref, wout_ref, om_ref,
                       tbuf_ref, kraw_ref, vraw_ref, gk_ref, gv_ref, idx_vm, idx_sm, cmp_sem,
                       gat_sem, idx_sem):
    b = pl.program_id(0)
    nb = pl.num_programs(0)
    n_pages = pt_ref.shape[1]
    n_past = n_pages * PAGE_SIZE
    n_sub = n_past // CMP_STRIDE
    n_gather = gk_ref.shape[1]
    q_blk = n_past // SLC_BLOCK

    def cmp_copies(bb, slot, j):
        page = layer_base + pt_ref[bb, j]
        return [pltpu.make_async_copy(pool_ref.at[page, kv], tbuf_ref.at[slot, kv, j],
                                      cmp_sem.at[slot]) for kv in range(2)]

    def start_cmp(bb, slot):
        def body(j, c):
            for cp in cmp_copies(bb, slot, j):
                cp.start()
            return c
        lax.fori_loop(0, n_pages, body, 0)

    def wait_cmp(bb, slot):
        def body(j, c):
            for cp in cmp_copies(bb, slot, j):
                cp.wait()
            return c
        lax.fori_loop(0, n_pages, body, 0)

    slot = b % 2
    nslot = 1 - slot
    raw_refs = (kraw_ref, vraw_ref)

    def rows_from_tiles(src_slot):
        def body(j, c):
            dst = pl.ds(pl.multiple_of(j * PAGE_SIZE, PAGE_SIZE), PAGE_SIZE)
            for kv in range(2):
                tile = tbuf_ref[src_slot, kv, j].reshape(N_KV * HEAD_DIM, PAGE_SIZE)
                raw_refs[kv][src_slot, dst, :] = jnp.transpose(tile)
            return c
        lax.fori_loop(0, n_pages, body, 0, unroll=8)

    @pl.when(b == 0)
    def _():
        start_cmp(0, 0)
        wait_cmp(0, 0)
        rows_from_tiles(0)

    @pl.when(b + 1 < nb)
    def _():
        start_cmp(b + 1, nslot)

    row8 = lax.broadcasted_iota(jnp.int32, (8, LANES), 0)
    lane8 = lax.broadcasted_iota(jnp.int32, (8, LANES), 1)
    low = lane8 < HEAD_DIM
    own_half = (row8 < GROUP) == low

    def stack8(qrow):
        out = jnp.zeros((8, LANES), F32)
        for j in range(GROUP):
            pj = jnp.broadcast_to(qrow[:, j * LANES:(j + 1) * LANES], (8, LANES))
            out = jnp.where((row8 & (GROUP - 1)) == j, pj, out)
        return jnp.where(own_half, out, 0.0).astype(BF16)

    q8 = stack8(q_ref[0])
    qr8 = stack8(qr_ref[0])

    kc, vc = _compress(
        lambda kv, s: raw_refs[kv][slot, pl.ds(s, n_sub, stride=CMP_STRIDE), :],
        n_sub, w1_ref, pe_ref, w2_ref, kg_ref, e64_ref[...])
    s = _dot_nt(q8, kc.astype(BF16))
    col = lax.broadcasted_iota(jnp.int32, s.shape, 1)
    rowc = lax.broadcasted_iota(jnp.int32, s.shape, 0)
    vis = col < n_sub - CMP_RATIO + 1
    sm = jnp.where(vis, s, NEG)
    e = jnp.where(vis, jnp.exp(sm - jnp.max(sm, axis=1, keepdims=True)), 0.0)
    pr = e / jnp.sum(e, axis=1, keepdims=True)
    o_c = _dot(pr.astype(BF16), vc.astype(BF16))
    imp8 = jnp.zeros(s.shape, F32)
    for g in range(N_KV):
        imp_g = jnp.sum(jnp.where(rowc // GROUP == g, pr, 0.0), axis=0, keepdims=True)
        imp8 = jnp.where(rowc == g, jnp.broadcast_to(imp_g, s.shape), imp8)

    n_blk_pad = ov_ref.shape[1]
    sc = _dot_exact01(imp8, ov_ref[...])
    bl = lax.broadcasted_iota(jnp.int32, sc.shape, 1)
    forced = (bl == 0) | (bl == q_blk) | (bl == q_blk - 1)
    sc = jnp.where(forced, jnp.inf, jnp.where(bl <= q_blk, sc, -jnp.inf))
    sc_t = jnp.transpose(sc)
    blf = bl[0:1].astype(F32)
    idx8 = jnp.zeros((8, LANES), F32)
    for g in range(N_KV):
        colv = jnp.broadcast_to(sc_t[:, g:g + 1], (n_blk_pad, n_blk_pad))
        rowv = jnp.broadcast_to(sc[g:g + 1, :], (n_blk_pad, n_blk_pad))
        ii = lax.broadcasted_iota(jnp.int32, colv.shape, 0)
        jj = lax.broadcasted_iota(jnp.int32, colv.shape, 1)
        tie = jnp.where(ii < jj, 1.0, 0.0)
        beats = jnp.where(colv > rowv, 1.0, jnp.where(colv == rowv, tie, 0.0))
        rank = jnp.sum(beats, axis=0, keepdims=True)
        for k in range(N_SELECT):
            blk_k = jnp.sum(jnp.where(rank == float(k), blf, 0.0), axis=1, keepdims=True)
            idx8 = jnp.where((row8 == g) & (lane8 == k), blk_k, idx8)
    idx_vm[...] = idx8.astype(jnp.int32)
    idx_copy = pltpu.make_async_copy(idx_vm, idx_sm, idx_sem.at[0])
    idx_copy.start()

    w_len = wbuf_ref.shape[-1]
    wlane = lax.broadcasted_iota(jnp.int32, (HEAD_DIM, w_len), 1)
    wnew = wnew_ref[0]
    w_tiles = [[None] * N_KV for _ in range(2)]
    for kv in range(2):
        cols = jnp.transpose(jnp.broadcast_to(wnew[:, kv * LANES:(kv + 1) * LANES], (8, LANES)))
        for g in range(N_KV):
            col = cols[g * HEAD_DIM:(g + 1) * HEAD_DIM, 0:1]
            upd = jnp.where(wlane == w_len - 1, col,
                            pltpu.roll(wbuf_ref[0, 0, kv, g], w_len - 1, 1))
            wout_ref[0, kv, g] = upd
            w_tiles[kv][g] = upd.astype(BF16)
    o_w_parts = []
    for g in range(N_KV):
        s_w = _dot(qr8[:, g * HEAD_DIM:(g + 1) * HEAD_DIM], w_tiles[0][g])
        e_w = jnp.exp(s_w - jnp.max(s_w, axis=1, keepdims=True))
        pr_w = e_w / jnp.sum(e_w, axis=1, keepdims=True)
        o_w_parts.append(_dot_nt(pr_w.astype(BF16), w_tiles[1][g]))
    o_w = jnp.concatenate(o_w_parts, axis=1)

    idx_copy.wait()
    past_ranks = [k for k in range(N_SELECT) if k != 2]
    gathers, halves = [], [[], []]
    for g in range(N_KV):
        for kk, k in enumerate(past_ranks):
            blk = jnp.minimum(idx_sm[g, k], q_blk - 1)
            page = layer_base + pt_ref[b, lax.shift_right_logical(blk, 1)]
            halves[g].append(blk & 1)
            gathers.append(pltpu.make_async_copy(pool_ref.at[page, 2, g], gk_ref.at[g, kk],
                                                 gat_sem.at[0]))
            gathers.append(pltpu.make_async_copy(pool_ref.at[page, 3, g], gv_ref.at[g, kk],
                                                 gat_sem.at[0]))
    for cp in gathers:
        cp.start()

    mq = mq_ref[0]
    hrow = lax.broadcasted_iota(jnp.int32, (MEM_HEADS, LANES), 0)
    q4 = jnp.zeros((MEM_HEADS, LANES), F32)
    for h in range(MEM_HEADS):
        q4 = jnp.where(hrow == h, jnp.broadcast_to(mq[:, h * LANES:(h + 1) * LANES],
                                                   (MEM_HEADS, LANES)), q4)
    s_m = jnp.sum(mkv_ref[0, 0, :, 0] * q4[None], axis=-1, keepdims=True) * MEM_HEAD_DIM ** -0.5
    e_m = jnp.exp(s_m - jnp.max(s_m, axis=0, keepdims=True))
    p_m = e_m / jnp.sum(e_m, axis=0, keepdims=True)
    om_ref[0] = jnp.sum(p_m * mkv_ref[0, 0, :, 1], axis=0)

    @pl.when(b + 1 < nb)
    def _():
        wait_cmp(b + 1, nslot)
        rows_from_tiles(nslot)

    for cp in gathers:
        cp.wait()
    rows = rows_ref[0]
    k_new = rows[:, 2 * LANES:3 * LANES].astype(BF16).astype(F32)
    v_new = rows[:, 3 * LANES:4 * LANES].astype(BF16).astype(F32)
    s_new = jnp.sum(qr8.astype(F32) * k_new, axis=1, keepdims=True)
    lane_blk = lane8 // SLC_BLOCK
    o_parts, p_new = [], []
    for g in range(N_KV):
        q_g = qr8[:, g * HEAD_DIM:(g + 1) * HEAD_DIM]
        ss = []
        for kk in range(n_gather):
            st = _dot(q_g, gk_ref[g, kk].astype(BF16))
            ss.append(jnp.where(lane_blk == halves[g][kk], st, NEG))
        s_past = jnp.concatenate(ss, axis=1)
        m = jnp.maximum(jnp.max(s_past, axis=1, keepdims=True), s_new)
        e_past = jnp.exp(s_past - m)
        e_new = jnp.exp(s_new - m)
        den = jnp.sum(e_past, axis=1, keepdims=True) + e_new
        pr_past = (e_past / den).astype(BF16)
        o_g = None
        for kk in range(n_gather):
            d = _dot_nt(pr_past[:, kk * LANES:(kk + 1) * LANES], gv_ref[g, kk].astype(BF16))
            o_g = d if o_g is None else o_g + d
        o_parts.append(o_g)
        p_new.append((e_new / den).astype(BF16).astype(F32))
    pr_new = jnp.where(row8[:, 0:1] < GROUP, p_new[0], p_new[1])
    o_s = jnp.concatenate(o_parts, axis=1) + pr_new * v_new

    gh, gl = _split_bf16(jnp.broadcast_to(ng_ref[0], (8, LANES)))
    tot = jnp.zeros((8, LANES), F32)
    for br, o8 in enumerate((o_c, o_s, o_w)):
        pair = jnp.where(low, o8, pltpu.roll(o8, GROUP, 0))
        gate8 = jnp.zeros((8, LANES), F32)
        for j in range(GROUP):
            gx = gx_ref[:, (br * GROUP + j) * LANES:(br * GROUP + j + 1) * LANES]
            gate8 = jnp.where(row8 == j, _dot(gh, gx) + _dot(gl, gx), gate8)
        tot = tot + gate8 * pair
    o_ref[0] = tot


def _nsa_sample_call(layer, page_table, pool5, q4, qr4, rows_s, win_new, win_buf, mq4, mkv,
                     ng_s, pw):
    nb, n_pages = page_table.shape
    n_past = n_pages * PAGE_SIZE
    n_sub = n_past // CMP_STRIDE
    n_phys = pool5.shape[0] // DEPTH
    w_buf = win_buf.shape[-1]
    n_blk_pad = pw['ov_s'].shape[1]
    per_b = lambda *shape: pl.BlockSpec((1,) + shape, lambda b, pt: (b,) + (0,) * len(shape))
    per_lb = lambda *shape: pl.BlockSpec((1, 1) + shape,
                                         lambda b, pt: (layer, b) + (0,) * len(shape))
    grid_spec = pltpu.PrefetchScalarGridSpec(
        num_scalar_prefetch=1,
        grid=(nb,),
        in_specs=[pl.BlockSpec(memory_space=pl.ANY),
                  per_b(1, 512), per_b(1, 512), per_b(1, 512), per_b(1, 256),
                  per_lb(2, N_KV, HEAD_DIM, w_buf), per_b(1, 512),
                  per_lb(MEM_LEN, 2, MEM_HEADS, MEM_HEAD_DIM),
                  per_b(1, LANES),
                  _const_spec((CMP_RATIO * CMP_STRIDE, 2 * LANES, 2 * CMP_HIDDEN)),
                  _const_spec((2 * CMP_RATIO * CMP_STRIDE, LANES)),
                  _const_spec((2, 2 * CMP_HIDDEN, LANES)),
                  _const_spec((1, LANES)), _const_spec((LANES, LANES)),
                  _const_spec((n_sub, n_blk_pad)),
                  _const_spec((LANES, 3 * GROUP * LANES))],
        out_specs=[per_b(8, LANES), per_b(2, N_KV, HEAD_DIM, w_buf),
                   per_b(MEM_HEADS, MEM_HEAD_DIM)],
        scratch_shapes=[
            pltpu.VMEM((2, 2, n_pages, N_KV, HEAD_DIM, PAGE_SIZE), F32),
            pltpu.VMEM((2, n_past, LANES), F32), pltpu.VMEM((2, n_past, LANES), F32),
            pltpu.VMEM((N_KV, N_SELECT - 1, HEAD_DIM, PAGE_SIZE), F32),
            pltpu.VMEM((N_KV, N_SELECT - 1, HEAD_DIM, PAGE_SIZE), F32),
            pltpu.VMEM((8, LANES), jnp.int32), pltpu.SMEM((8, LANES), jnp.int32),
            pltpu.SemaphoreType.DMA((2,)), pltpu.SemaphoreType.DMA((1,)),
            pltpu.SemaphoreType.DMA((1,))],
    )
    return pl.pallas_call(
        functools.partial(_nsa_sample_kernel, layer * n_phys),
        grid_spec=grid_spec,
        out_shape=[jax.ShapeDtypeStruct((nb, 8, LANES), F32),
                   jax.ShapeDtypeStruct((nb, 2, N_KV, HEAD_DIM, w_buf), F32),
                   jax.ShapeDtypeStruct((nb, MEM_HEADS, MEM_HEAD_DIM), F32)],
        compiler_params=pltpu.CompilerParams(dimension_semantics=("arbitrary",),
                                             vmem_limit_bytes=VMEM_LIMIT),
        name="nsa_sample",
    )(page_table, pool5, q4, qr4, rows_s, win_new, win_buf, mq4, mkv, ng_s,
      pw['cmp_w1'], pw['cmp_pe'], pw['cmp_w2'], pw['kc_gain'], pw['e64k'], pw['ov_s'],
      pw['gate_x'])


def _pair_perm():
    p = np.arange(BRANCH_W)
    j, ln = p // LANES, p % LANES
    head = np.where(ln < HEAD_DIM, j, GROUP + j)
    return head * HEAD_DIM + ln % HEAD_DIM


def _block_diag_ones(width, group):
    i = np.arange(width)
    return (i[:, None] // group == i[None, :] // group).astype(np.float32)


def _overlap(n_cmp_rows, n_cmp_valid, n_blk_cols):
    i = np.arange(n_cmp_rows)[:, None]
    j = np.arange(n_blk_cols)[None, :]
    ov = (i * CMP_STRIDE < (j + 1) * SLC_BLOCK) & (i * CMP_STRIDE + CMP_LEN > j * SLC_BLOCK)
    return (ov & (i < n_cmp_valid)).astype(np.float32)


def _block_mask_rows(n_keys):
    k = np.arange(n_keys)[:, None]
    j = np.arange(LANES)[None, :]
    return np.where(k // SLC_BLOCK == j, NEG, 0.0).astype(np.float32)


def _band_bias(tq):
    r = np.arange(tq)[:, None]
    c = np.arange(tq)[None, :]
    return np.stack([np.where(c <= r, 0.0, NEG), np.where(c > r, 0.0, NEG)]).astype(np.float32)


def _gate_expand():
    x = np.zeros((LANES, 3 * GROUP * LANES), np.float32)
    for br in range(3):
        for j in range(GROUP):
            for ln in range(LANES):
                head = j if ln < HEAD_DIM else GROUP + j
                x[br * N_HEADS + head, (br * GROUP + j) * LANES + ln] = 1.0
    return x


def _rope_tables(pos):
    half = ROT_DIM // 2
    inv = ROPE_THETA ** (-jnp.arange(half, dtype=F32) / half)
    ang = pos.astype(F32)[:, None] * inv[None, :]
    cos, sin = jnp.cos(ang), jnp.sin(ang)
    n = pos.shape[0]
    one = jnp.ones((n, HEAD_DIM - ROT_DIM), F32)
    zero = jnp.zeros((n, HEAD_DIM - ROT_DIM), F32)
    zh = jnp.zeros((n, half), F32)
    c = jnp.concatenate([cos, cos, one], axis=1)
    s1 = jnp.concatenate([zh, sin, zero], axis=1)
    s2 = jnp.concatenate([-sin, zh, zero], axis=1)
    tab = jnp.stack([c, s1, s2])
    return jnp.concatenate([tab, tab], axis=2)


def _layer_params(l, seq_len, n_past, norm_mix, w_in, q_norm, k_norm, cmp_pe, cmp_w1, cmp_w2,
                  conv_w, norm_mem, w_mem_kv, mem_q_norm, mem_k_norm, w_branch, w_out, norm_ffn,
                  w_gate_up, w_down):
    perm = _pair_perm()
    wi = w_in[l]
    o_kv, o_ng = BRANCH_W, BRANCH_W + 768
    o_cx = o_ng + 3 * N_HEADS
    o_cb, o_cc, o_mq, o_mg = o_cx + 512, o_cx + 1024, o_cx + 1536, o_cx + 2048
    w_a = jnp.concatenate([
        wi[:, perm], wi[:, o_kv:o_ng], wi[:, o_cx:o_cb], wi[:, o_cb:o_cc], wi[:, o_cc:o_mq],
        wi[:, o_mq:o_mg], wi[:, o_ng:o_cx], jnp.zeros((D_MODEL, LANES - 3 * N_HEADS), F32)],
        axis=1).astype(BF16)
    w1 = cmp_w1[l].reshape(2, CMP_RATIO, CMP_STRIDE, HEAD_DIM, CMP_HIDDEN)
    zw = jnp.zeros_like(w1)
    w1bd = jnp.concatenate([jnp.concatenate([w1, zw], axis=-1),
                            jnp.concatenate([zw, w1], axis=-1)], axis=-2)
    w2 = cmp_w2[l]
    z2 = jnp.zeros_like(w2)
    w2bd = jnp.concatenate([jnp.concatenate([w2, z2], axis=-1),
                            jnp.concatenate([z2, w2], axis=-1)], axis=-2)
    pe = cmp_pe[l].reshape(2 * CMP_RATIO * CMP_STRIDE, HEAD_DIM)
    n_sub_p = seq_len // CMP_STRIDE
    n_sub_s = n_past // CMP_STRIDE
    n_blk_s = -(-(n_past // SLC_BLOCK + 1) // LANES) * LANES
    wbr = w_branch[l]
    return dict(
        nmix=norm_mix[l][None, :], w_a=w_a, w_mg=wi[:, o_mg:].astype(BF16),
        q_gain=jnp.tile(q_norm[l], N_HEADS)[None, :],
        k_gain=jnp.stack([jnp.tile(k_norm[l, 1], N_KV), jnp.tile(k_norm[l, 2], N_KV)]),
        kc_gain=jnp.tile(k_norm[l, 0], N_KV)[None, :],
        mq_gain=jnp.tile(mem_q_norm[l], MEM_HEADS)[None, :],
        mk_gain=mem_k_norm[l][None, :],
        conv_w=conv_w[l],
        cmp_w1=w1bd.reshape(CMP_RATIO * CMP_STRIDE, 2 * LANES, 2 * CMP_HIDDEN).astype(BF16),
        cmp_pe=jnp.concatenate([pe, pe], axis=1),
        cmp_w2=w2bd.astype(BF16),
        nmem=norm_mem[l][None, :], w_mem=w_mem_kv[l].astype(BF16),
        w_br=jnp.concatenate([wbr[0][perm][None], wbr[1:]], axis=0).astype(BF16),
        w_out=w_out[l].astype(BF16),
        nffn=norm_ffn[l][None, :], w_gu=w_gate_up[l].astype(BF16), w_down=w_down[l].astype(BF16),
        e64=jnp.asarray(_block_diag_ones(512, HEAD_DIM), BF16),
        e64k=jnp.asarray(_block_diag_ones(LANES, HEAD_DIM), BF16),
        e128=jnp.asarray(_block_diag_ones(512, MEM_HEAD_DIM), BF16),
        ov_p=jnp.asarray(_overlap(n_sub_p, n_sub_p - CMP_RATIO + 1, LANES).T, BF16),
        cm_p=jnp.asarray(_block_mask_rows(seq_len), BF16),
        ov_s=jnp.asarray(_overlap(n_sub_s, n_sub_s - CMP_RATIO + 1, n_blk_s), BF16),
        gate_x=jnp.asarray(_gate_expand(), BF16),
    )


def kernel(x_prompt, x_sample, cache_nsa_kv, cache_win_kv, state_conv, cache_mem_kv, page_table,
           mem_prompt, norm_mix, w_in, q_norm, k_norm, cmp_pe, cmp_w1, cmp_w2, conv_w, norm_mem,
           w_mem_kv, mem_q_norm, mem_k_norm, w_branch, w_out, norm_ffn, w_gate_up, w_down):
    bp, t, _ = x_prompt.shape
    bs, ts, _ = x_sample.shape
    assert ts == 1 and t % TM == 0 and t % TQ == 0
    n_pages = page_table.shape[1]
    n_past = n_pages * PAGE_SIZE
    w_buf = cache_win_kv.shape[2]
    assert w_buf == WINDOW and n_past >= WINDOW and (n_past // SLC_BLOCK) == LANES

    xp = x_prompt.reshape(bp * t, D_MODEL)
    xs = x_sample.reshape(bs, D_MODEL)
    rope_p = _rope_tables(jnp.arange(t))
    rope_s = _rope_tables(jnp.full((bs,), n_past, jnp.int32))
    n_phys = cache_nsa_kv.shape[1]
    win_t = jnp.transpose(cache_win_kv, (0, 1, 3, 4, 5, 2))
    pool5 = jnp.transpose(cache_nsa_kv, (0, 1, 3, 4, 5, 2)).reshape(
        DEPTH * n_phys, 4, N_KV, HEAD_DIM, PAGE_SIZE)

    rows_p, rows_s, win_p, win_s, conv_p, conv_s, mem_p = [], [], [], [], [], [], []
    for l in range(DEPTH):
        pw = _layer_params(l, t, n_past, norm_mix, w_in, q_norm, k_norm, cmp_pe, cmp_w1, cmp_w2,
                           conv_w, norm_mem, w_mem_kv, mem_q_norm, mem_k_norm, w_branch, w_out,
                           norm_ffn, w_gate_up, w_down)
        q, qr, rows, win, kvb, z, utail, mq, ng = _prep_call(xp, pw, rope_p, False, t)
        rows3 = rows.reshape(bp, t, 512)
        kc, vc = _compress_prompt_call(rows3, pw)
        o_nsa = _nsa_prompt_call(q.reshape(bp, t, 512), qr.reshape(bp, t, 512),
                                 kvb.reshape(bp, t, 512), kc, vc, ng.reshape(bp, t, LANES), pw)
        mkv = _memkv_call(mem_prompt, pw)
        o_mem = _mem_attn_call(mq.reshape(bp, t, 512), mkv)
        h = _finish_call(xp, o_nsa.reshape(bp * t, 512), z, o_mem.reshape(bp * t, 512), pw)
        xp = _ffn_call(h, pw)
        rows_p.append(rows3.reshape(bp, t, 4, N_KV, HEAD_DIM))
        win_p.append(win.reshape(bp, t, 2, N_KV, HEAD_DIM)[:, t - min(WINDOW, t):])
        conv_p.append(utail.reshape(bp, t // TM, 8, 512)[:, -1, 8 - (CONV_W - 1):])
        mem_p.append(mkv.reshape(bp, MEM_LEN, 2, MEM_HEADS, MEM_HEAD_DIM))
        q, qr, rows, win, _, z, u, mq, ng = _prep_call(xs, pw, rope_s, True, 1,
                                                       state=state_conv[l])
        o_nsa, new_win, o_mem = _nsa_sample_call(
            l, page_table, pool5, q.astype(F32).reshape(bs, 1, 512),
            qr.astype(F32).reshape(bs, 1, 512), rows.reshape(bs, 1, 512),
            win.reshape(bs, 1, 256), win_t,
            mq.astype(F32).reshape(bs, 1, 512),
            cache_mem_kv, ng.reshape(bs, 1, LANES), pw)
        o_nsa = o_nsa[:, 0:GROUP, :].reshape(bs, 512).astype(BF16)
        o_mem = o_mem.reshape(bs, BRANCH_W).astype(BF16)
        h = _finish_call(xs, o_nsa, z, o_mem, pw)
        xs = _ffn_call(h, pw)
        rows_s.append(rows.reshape(bs, 1, 4, N_KV, HEAD_DIM))
        win_s.append(new_win)
        conv_s.append(jnp.stack([state_conv[l][:, 1, :], u], axis=1))

    return (xp.reshape(bp, t, D_MODEL), xs.reshape(bs, 1, D_MODEL), jnp.stack(rows_p),
            jnp.stack(rows_s), jnp.stack(win_p),
            jnp.transpose(jnp.stack(win_s), (0, 1, 5, 2, 3, 4)), jnp.stack(conv_p),
            jnp.stack(conv_s), jnp.stack(mem_p))
```
